```python
import math
import jax
import jax.numpy as jnp
from jax import lax
import numpy as np

D_MODEL = 1024
BATCH = 16
SEQ = 2048
DEPTH = 4

GRID_W = 64
CTX_LEN = 256
N_MIXERS = 2
DA_HEADS = 8
DA_HEAD_DIM = 64
DA_V_DIM = 2 * DA_HEAD_DIM
ROPE_BASE = 10000.0
Q_BLOCK = 128
CONV_WIDTH = 3
N_EXPERTS = 32
TOP_K = 4
D_FF = D_MODEL
SWIGLU_LIMIT = 7.0
SWIGLU_ALPHA = 1.702
MOE_BLOCK = 128
N_MOD = 6
EPS = 1e-5

kernel_name = 'hybrid_diffattn_shortconv_moe_dit'


def rmsnorm(x, g):
    xf = x.astype(jnp.float32)
    xf = xf * lax.rsqrt(jnp.mean(xf * xf, axis=-1, keepdims=True) + EPS)
    return xf.astype(x.dtype) * g


def modulate(h, shift, scale):
    return h * (1 + scale) + shift


def axial_rope_tables(n_tokens):
    rows = n_tokens // GRID_W
    row_pos = jnp.repeat(jnp.arange(rows, dtype=jnp.float32), GRID_W)
    col_pos = jnp.tile(jnp.arange(GRID_W, dtype=jnp.float32), rows)
    axis_dims = DA_HEAD_DIM // 2
    inv_freq = ROPE_BASE ** (-jnp.arange(0, axis_dims, 2, dtype=jnp.float32) / axis_dims)
    ang = jnp.concatenate([row_pos[:, None] * inv_freq, col_pos[:, None] * inv_freq], axis=-1)
    return jnp.cos(ang), jnp.sin(ang)


def apply_rope(x, cos, sin):
    cos = cos.astype(x.dtype)
    sin = sin.astype(x.dtype)
    x1, x2 = jnp.split(x, 2, axis=-1)
    return jnp.concatenate([x1 * cos - x2 * sin, x2 * cos + x1 * sin], axis=-1)


def diff_attention(h_lat, h_ctx, w_qkv, w_o, lam_params, subln_g, lam_init, ctx_out):
    B, N, _ = h_lat.shape
    H, d = DA_HEADS, DA_HEAD_DIM

    def project(h):
        n = h.shape[1]
        q, k, v = jnp.split(h @ w_qkv, 3, axis=-1)
        return (q.reshape(B, n, H, 2, d), k.reshape(B, n, H, 2, d), v.reshape(B, n, H, DA_V_DIM))

    lp = lam_params.astype(jnp.float32)
    lam = jnp.exp(jnp.sum(lp[0] * lp[1])) - jnp.exp(jnp.sum(lp[2] * lp[3])) + lam_init

    def attend(q, k, v):
        s = jnp.einsum('bqhcd,bkhcd->bhcqk', q.astype(jnp.float32), k.astype(jnp.float32)) * (d ** -0.5)
        p = jax.nn.softmax(s, axis=-1)
        a = p[:, :, 0] - lam * p[:, :, 1]
        return jnp.einsum('bhqk,bkhe->bqhe', a.astype(v.dtype), v)

    def head_out(o):
        o = rmsnorm(o, subln_g) * (1.0 - lam_init)
        return o.reshape(o.shape[0], o.shape[1], H * DA_V_DIM) @ w_o

    q_l, k_l, v_l = project(h_lat)
    q_c, k_c, v_c = project(h_ctx)
    cos, sin = axial_rope_tables(N)
    cos, sin = cos[:, None, None, :], sin[:, None, None, :]
    q_l = apply_rope(q_l, cos, sin)
    k_l = apply_rope(k_l, cos, sin)
    k_all = jnp.concatenate([k_c, k_l], axis=1)
    v_all = jnp.concatenate([v_c, v_l], axis=1)
    nb = N // Q_BLOCK
    q_blocks = q_l.reshape(B, nb, Q_BLOCK, H, 2, d).swapaxes(0, 1)
    o_l = lax.map(lambda qb: attend(qb, k_all, v_all), q_blocks)
    o_l = o_l.swapaxes(0, 1).reshape(B, N, H, DA_V_DIM)
    y_lat = head_out(o_l)
    y_ctx = head_out(attend(q_c, k_c, v_c)) if ctx_out else None
    return y_lat, y_ctx


def short_conv(h, w_in, conv_w, w_out):
    gate_b, gate_c, xt = jnp.split(h @ w_in, 3, axis=-1)
    u = gate_c * xt
    u = lax.conv_general_dilated(u, conv_w[:, None, :].astype(u.dtype), window_strides=(1,),
                                 padding=((CONV_WIDTH // 2, CONV_WIDTH // 2),),
                                 dimension_numbers=('NWC', 'WIO', 'NWC'),
                                 feature_group_count=u.shape[-1])
    return (gate_b * u) @ w_out


def moe_ffn(h, router_w, router_b, w_gu, b_gu, w_down, b_down):
    T, D = h.shape
    logits = (h @ router_w + router_b).astype(jnp.float32)
    top_logits, top_idx = lax.top_k(logits, TOP_K)
    gates = jax.nn.softmax(top_logits, axis=-1).astype(h.dtype)
    n_assign = T * TOP_K
    flat_e = top_idx.reshape(-1).astype(jnp.int32)
    order = jnp.argsort(flat_e)
    sorted_e = flat_e[order]
    counts = jnp.bincount(flat_e, length=N_EXPERTS).astype(jnp.int32)
    padded = (counts + MOE_BLOCK - 1) // MOE_BLOCK * MOE_BLOCK
    pad_end = jnp.cumsum(padded)
    pad_start = pad_end - padded
    start = jnp.cumsum(counts) - counts
    dest_sorted = pad_start[sorted_e] + jnp.arange(n_assign, dtype=jnp.int32) - start[sorted_e]
    n_blocks = -(-(n_assign + N_EXPERTS * (MOE_BLOCK - 1)) // MOE_BLOCK)
    n_rows = n_blocks * MOE_BLOCK
    row_token = jnp.zeros((n_rows,), jnp.int32).at[dest_sorted].set((order // TOP_K).astype(jnp.int32))
    block_start = jnp.arange(n_blocks, dtype=jnp.int32) * MOE_BLOCK
    block_expert = jnp.minimum(jnp.searchsorted(pad_end, block_start, side='right'), N_EXPERTS - 1)
    xb = h[row_token].reshape(n_blocks, MOE_BLOCK, D)

    def expert_block(args):
        xe, e = args
        gu = xe @ w_gu[e] + b_gu[e]
        x_glu = jnp.minimum(gu[:, ::2], SWIGLU_LIMIT)
        x_lin = jnp.clip(gu[:, 1::2], -SWIGLU_LIMIT, SWIGLU_LIMIT)
        act = x_glu * jax.nn.sigmoid(SWIGLU_ALPHA * x_glu) * (x_lin + 1)
        return act @ w_down[e] + b_down[e]

    yb = lax.map(expert_block, (xb, block_expert)).reshape(n_rows, D)
    dest = jnp.zeros((n_assign,), jnp.int32).at[order].set(dest_sorted)
    y = yb[dest].reshape(T, TOP_K, D)
    return jnp.einsum('tk,tkd->td', gates, y)


def setup_inputs(seed: int = 0) -> dict:
    key = jax.random.key(seed)
    ks = jax.random.split(key, 24)
    D, E, F = D_MODEL, N_EXPERTS, D_FF
    na = len(range(0, DEPTH, N_MIXERS))
    nc = DEPTH - na

    def nrm(k, shape, s):
        return jax.random.normal(k, shape, jnp.float32) * s

    return {
        'x': nrm(ks[0], (BATCH, SEQ, D), 1.0),
        'c': nrm(ks[1], (BATCH, D), 1.0),
        'ctx': nrm(ks[2], (BATCH, CTX_LEN, D), 1.0),
        'c_ctx': nrm(ks[3], (D,), 1.0),
        'mod_w': nrm(ks[4], (DEPTH, D, N_MOD * D), 0.02),
        'mod_b': nrm(ks[5], (DEPTH, N_MOD * D), 0.02),
        'norm_g': 1.0 + nrm(ks[6], (DEPTH, 2, D), 0.02),
        'attn_w_qkv': nrm(ks[7], (na, D, 3 * D), D ** -0.5),
        'attn_w_o': nrm(ks[8], (na, DA_HEADS * DA_V_DIM, D), (DA_HEADS * DA_V_DIM) ** -0.5),
        'attn_lambda': nrm(ks[9], (na, 4, DA_HEAD_DIM), 0.1),
        'attn_subln_g': 1.0 + nrm(ks[10], (na, DA_V_DIM), 0.02),
        'conv_w_in': nrm(ks[11], (nc, D, 3 * D), D ** -0.5),
        'conv_w': nrm(ks[12], (nc, CONV_WIDTH, D), CONV_WIDTH ** -0.5),
        'conv_w_out': nrm(ks[13], (nc, D, D), D ** -0.5),
        'router_w': nrm(ks[14], (DEPTH, D, E), D ** -0.5),
        'router_b': nrm(ks[15], (DEPTH, E), 0.01),
        'moe_w_gu': nrm(ks[16], (DEPTH, E, D, 2 * F), D ** -0.5),
        'moe_b_gu': nrm(ks[17], (DEPTH, E, 2 * F), 0.02),
        'moe_w_down': nrm(ks[18], (DEPTH, E, F, D), F ** -0.5),
        'moe_b_down': nrm(ks[19], (DEPTH, E, D), 0.02),
        'final_g': 1.0 + nrm(ks[20], (D,), 0.02),
    }


def reference(x, c, ctx, c_ctx, mod_w, mod_b, norm_g, attn_w_qkv, attn_w_o, attn_lambda, attn_subln_g,
              conv_w_in, conv_w, conv_w_out, router_w, router_b, moe_w_gu, moe_b_gu, moe_w_down, moe_b_down,
              final_g):
    B, N, D = x.shape
    for i in range(DEPTH):
        update_ctx = i < DEPTH - 1
        attn_layer = i % N_MIXERS == 0
        j = i // N_MIXERS
        mod_lat = (jax.nn.silu(c) @ mod_w[i] + mod_b[i])[:, None, :]
        sh1, sc1, g1, sh2, sc2, g2 = jnp.split(mod_lat, N_MOD, axis=-1)
        if update_ctx or attn_layer:
            mod_ctx = jax.nn.silu(c_ctx) @ mod_w[i] + mod_b[i]
            csh1, csc1, cg1, csh2, csc2, cg2 = jnp.split(mod_ctx, N_MOD, axis=-1)
            h_ctx = modulate(rmsnorm(ctx, norm_g[i, 0]), csh1, csc1)
        h_lat = modulate(rmsnorm(x, norm_g[i, 0]), sh1, sc1)
        if attn_layer:
            lam_init = 0.8 - 0.6 * math.exp(-0.3 * i)
            y_lat, y_ctx = diff_attention(h_lat, h_ctx, attn_w_qkv[j], attn_w_o[j], attn_lambda[j],
                                          attn_subln_g[j], lam_init, update_ctx)
        else:
            y_lat = short_conv(h_lat, conv_w_in[j], conv_w[j], conv_w_out[j])
            y_ctx = short_conv(h_ctx, conv_w_in[j], conv_w[j], conv_w_out[j]) if update_ctx else None
        x = x + g1 * y_lat
        h_lat = modulate(rmsnorm(x, norm_g[i, 1]), sh2, sc2).reshape(B * N, D)
        if update_ctx:
            ctx = ctx + cg1 * y_ctx
            h_ctx = modulate(rmsnorm(ctx, norm_g[i, 1]), csh2, csc2).reshape(-1, D)
            out = moe_ffn(jnp.concatenate([h_lat, h_ctx], axis=0), router_w[i], router_b[i], moe_w_gu[i],
                          moe_b_gu[i], moe_w_down[i], moe_b_down[i])
            x = x + g2 * out[:B * N].reshape(B, N, D)
            ctx = ctx + cg2 * out[B * N:].reshape(ctx.shape)
        else:
            out = moe_ffn(h_lat, router_w[i], router_b[i], moe_w_gu[i], moe_b_gu[i], moe_w_down[i], moe_b_down[i])
            x = x + g2 * out.reshape(B, N, D)
    return rmsnorm(x, final_g)
```

```python
import functools
import math

import jax
import jax.numpy as jnp
from jax import lax
from jax.experimental import pallas as pl
from jax.experimental.pallas import tpu as pltpu

F32 = jnp.float32
BF16 = jnp.bfloat16
I32 = jnp.int32

GRID_W = 64
HEAD_DIM = 64
ROPE_BASE = 10000.0
N_MIXERS = 2
TOP_K = 4
SWIGLU_LIMIT = 7.0
SWIGLU_ALPHA = 1.702
N_MOD = 6
EPS = 1e-5

LANES = 128
SUBLANES = 8
BF16_SUBLANES = 16
VMEM_LIMIT_BYTES = 48 * 1024 * 1024

ROW_TILE = 512
ATTN_Q_TILE = 256
MOE_TILE = 256

_NT = (((1,), (1,)), ((), ()))


def _round_up(a, b):
    return (a + b - 1) // b * b


def _params(*sem):
    return pltpu.CompilerParams(dimension_semantics=sem, vmem_limit_bytes=VMEM_LIMIT_BYTES)


def _norm_mod(x, g, shift, scale):
    ms = jnp.mean(x * x, axis=-1, keepdims=True)
    return (x * lax.rsqrt(ms + EPS) * g) * (1.0 + scale) + shift


def _mod_body(c_ref, w_ref, b_ref, o_ref):
    c = c_ref[...]
    s = c * jax.nn.sigmoid(c)
    o_ref[0] = jnp.dot(s, w_ref[0], preferred_element_type=F32, precision=lax.Precision.HIGHEST) + b_ref[0]


def _modulation(c, c_ctx, mod_w, mod_b):
    depth, d, nd = mod_w.shape
    b = c.shape[0]
    r = _round_up(b + 1, SUBLANES)
    cv = jnp.concatenate([c, c_ctx[None, :], jnp.zeros((r - b - 1, d), F32)], axis=0)
    tn = d
    out = pl.pallas_call(
        _mod_body,
        grid=(depth, nd // tn),
        in_specs=[
            pl.BlockSpec((r, d), lambda l, j: (0, 0)),
            pl.BlockSpec((1, d, tn), lambda l, j: (l, 0, j)),
            pl.BlockSpec((1, 1, tn), lambda l, j: (l, 0, j)),
        ],
        out_specs=pl.BlockSpec((1, r, tn), lambda l, j: (l, 0, j)),
        out_shape=jax.ShapeDtypeStruct((depth, r, nd), F32),
        compiler_params=_params("parallel", "parallel"),
        name="modulation",
    )(cv, mod_w, mod_b.reshape(depth, 1, nd))
    return out.reshape(depth * r, 1, nd), r


def _mod_spec(layer, chunk, r, d, n_lat_tiles, tm, n_seq, b):
    def index(i, *_):
        row = jnp.where(i < n_lat_tiles, (i * tm) // n_seq, b)
        return (layer * r + row, 0, chunk)

    return pl.BlockSpec((1, 1, d), index)


def _rope_tables(n, tm):
    rows = n // GRID_W
    row_pos = jnp.repeat(jnp.arange(rows, dtype=F32), GRID_W)
    col_pos = jnp.tile(jnp.arange(GRID_W, dtype=F32), rows)
    axis_dims = HEAD_DIM // 2
    inv_freq = ROPE_BASE ** (-jnp.arange(0, axis_dims, 2, dtype=F32) / axis_dims)
    ang = jnp.concatenate([row_pos[:, None] * inv_freq, col_pos[:, None] * inv_freq], axis=-1)
    cos, sin = jnp.cos(ang), jnp.sin(ang)
    reps = LANES // (HEAD_DIM // 2)
    cos_t = jnp.tile(cos, (1, reps))
    sign = jnp.where((jnp.arange(LANES) % HEAD_DIM) < HEAD_DIM // 2, -1.0, 1.0).astype(F32)
    sin_t = jnp.tile(sin, (1, reps)) * sign
    cos_t = jnp.concatenate([cos_t, jnp.ones((tm, LANES), F32)], axis=0)
    sin_t = jnp.concatenate([sin_t, jnp.zeros((tm, LANES), F32)], axis=0)
    return cos_t, sin_t


def _qkv_body(x_ref, sh_ref, sc_ref, g_ref, w_ref, cos_ref, sin_ref, o_ref, *, d):
    h = _norm_mod(x_ref[...], g_ref[...], sh_ref[0], sc_ref[0])
    acc = jnp.dot(h.astype(BF16), w_ref[...], preferred_element_type=F32)
    cos = cos_ref[...]
    sin = sin_ref[...]
    lane = lax.broadcasted_iota(I32, (1, LANES), 1)
    first_half = (lane % HEAD_DIM) < HEAD_DIM // 2
    q_scale = HEAD_DIM**-0.5
    for gi in range(2 * d // LANES):
        xg = acc[:, gi * LANES:(gi + 1) * LANES]
        partner = jnp.where(first_half, pltpu.roll(xg, LANES - HEAD_DIM // 2, 1), pltpu.roll(xg, HEAD_DIM // 2, 1))
        r = xg * cos + partner * sin
        if gi < d // LANES:
            r = r * q_scale
        o_ref[:, gi * LANES:(gi + 1) * LANES] = r.astype(BF16)
    o_ref[:, 2 * d:] = acc[:, 2 * d:].astype(BF16)


def _qkv_proj(xa, mod, r, layer, g, w, n_rows, n_lat_rows, n_seq, b):
    d = xa.shape[1]
    tm = ROW_TILE
    n_lat_tiles = n_lat_rows // tm
    tiles_per_seq = n_seq // tm
    cos_t, sin_t = _rope_tables(n_seq, tm)

    def rope_index(i):
        return (jnp.where(i < n_lat_tiles, i % tiles_per_seq, tiles_per_seq), 0)

    return pl.pallas_call(
        functools.partial(_qkv_body, d=d),
        grid=(n_rows // tm,),
        in_specs=[
            pl.BlockSpec((tm, d), lambda i: (i, 0)),
            _mod_spec(layer, 0, r, d, n_lat_tiles, tm, n_seq, b),
            _mod_spec(layer, 1, r, d, n_lat_tiles, tm, n_seq, b),
            pl.BlockSpec((1, d), lambda i: (0, 0)),
            pl.BlockSpec((d, 3 * d), lambda i: (0, 0)),
            pl.BlockSpec((tm, LANES), rope_index),
            pl.BlockSpec((tm, LANES), rope_index),
        ],
        out_specs=pl.BlockSpec((tm, 3 * d), lambda i: (i, 0)),
        out_shape=jax.ShapeDtypeStruct((n_rows, 3 * d), BF16),
        compiler_params=_params("parallel"),
        name="qkv_proj",
    )(xa, mod, mod, g[None, :], w.astype(BF16), cos_t, sin_t)


def _diff_attention(lam_ref, sg_ref, q_ref, k_refs, v_refs, o_ref, lam_init):
    q = q_ref[...]
    lane = lax.broadcasted_iota(I32, (1, LANES), 1)
    lp = lam_ref[...]
    lam = (jnp.exp(jnp.sum(lp[0:1] * lp[1:2], axis=-1, keepdims=True))
           - jnp.exp(jnp.sum(lp[2:3] * lp[3:4], axis=-1, keepdims=True)) + lam_init)
    probs = []
    for comp in range(2):
        in_comp = (lane < HEAD_DIM) if comp == 0 else (lane >= HEAD_DIM)
        qc = jnp.where(in_comp, q, jnp.zeros_like(q))
        ss = [lax.dot_general(qc, k[...], _NT, preferred_element_type=F32) for k in k_refs]
        m = functools.reduce(jnp.maximum, [jnp.max(s, axis=-1, keepdims=True) for s in ss])
        es = [jnp.exp(s - m) for s in ss]
        den = functools.reduce(jnp.add, [jnp.sum(e, axis=-1, keepdims=True) for e in es])
        probs.append([e / den for e in es])
    o = None
    for j in range(len(k_refs)):
        a = probs[0][j] - lam * probs[1][j]
        part = jnp.dot(a.astype(BF16), v_refs[j][...], preferred_element_type=F32)
        o = part if o is None else o + part
    ms = jnp.mean(o * o, axis=-1, keepdims=True)
    o_ref[...] = (o * lax.rsqrt(ms + EPS) * sg_ref[...] * (1.0 - lam_init)).astype(BF16)


def _attn_body(lam_ref, sg_ref, q_ref, kc_ref, kl_ref, vc_ref, vl_ref, o_ref, *, lam_init, q_tiles):
    i = pl.program_id(2)

    @pl.when(i < q_tiles)
    def _():
        _diff_attention(lam_ref, sg_ref, q_ref, [kc_ref, kl_ref], [vc_ref, vl_ref], o_ref, lam_init)

    @pl.when(i >= q_tiles)
    def _():
        _diff_attention(lam_ref, sg_ref, q_ref, [kc_ref], [vc_ref], o_ref, lam_init)


def _attention(qkv, lam_p, subln_g, lam_init, b, n_seq, n_ctx, n_lat_rows, update_ctx):
    d = qkv.shape[1] // 3
    heads = d // LANES
    tq = ATTN_Q_TILE
    assert n_seq % tq == 0 and n_ctx == tq
    q_tiles = n_seq // tq
    ctx_blk0 = n_lat_rows // n_ctx
    n_rows = n_lat_rows + (b * n_ctx if update_ctx else 0)
    steps = q_tiles + (1 if update_ctx else 0)

    def q_index(bi, h, i):
        return (jnp.where(i < q_tiles, bi * q_tiles + i, ctx_blk0 + bi), h)

    return pl.pallas_call(
        functools.partial(_attn_body, lam_init=lam_init, q_tiles=q_tiles),
        grid=(b, heads, steps),
        in_specs=[
            pl.BlockSpec((4, HEAD_DIM), lambda bi, h, i: (0, 0)),
            pl.BlockSpec((1, LANES), lambda bi, h, i: (0, 0)),
            pl.BlockSpec((tq, LANES), q_index),
            pl.BlockSpec((n_ctx, LANES), lambda bi, h, i: (ctx_blk0 + bi, heads + h)),
            pl.BlockSpec((n_seq, LANES), lambda bi, h, i: (bi, heads + h)),
            pl.BlockSpec((n_ctx, LANES), lambda bi, h, i: (ctx_blk0 + bi, 2 * heads + h)),
            pl.BlockSpec((n_seq, LANES), lambda bi, h, i: (bi, 2 * heads + h)),
        ],
        out_specs=pl.BlockSpec((tq, LANES), q_index),
        out_shape=jax.ShapeDtypeStruct((n_rows, d), BF16),
        compiler_params=_params("parallel", "parallel", "arbitrary"),
        name="diff_attention",
    )(lam_p, subln_g[None, :], qkv, qkv, qkv, qkv, qkv)


def _oproj_body(o_ref, w_ref, x_ref, g1_ref, out_ref):
    y = jnp.dot(o_ref[...], w_ref[...], preferred_element_type=F32)
    out_ref[...] = x_ref[...] + g1_ref[0] * y


def _out_proj(o, w, xa, mod, r, layer, n_lat_rows, n_seq, b):
    n_rows, d = o.shape
    tm = ROW_TILE
    return pl.pallas_call(
        _oproj_body,
        grid=(n_rows // tm,),
        in_specs=[
            pl.BlockSpec((tm, d), lambda i: (i, 0)),
            pl.BlockSpec((d, d), lambda i: (0, 0)),
            pl.BlockSpec((tm, d), lambda i: (i, 0)),
            _mod_spec(layer, 2, r, d, n_lat_rows // tm, tm, n_seq, b),
        ],
        out_specs=pl.BlockSpec((tm, d), lambda i: (i, 0)),
        out_shape=jax.ShapeDtypeStruct((n_rows, d), F32),
        compiler_params=_params("parallel"),
        name="attn_out_proj",
    )(o, w.astype(BF16), xa, mod)


def _cproj_body(x_ref, sh_ref, sc_ref, g_ref, w_ref, gb_ref, u_ref, *, d):
    h = _norm_mod(x_ref[...], g_ref[...], sh_ref[0], sc_ref[0])
    acc = jnp.dot(h.astype(BF16), w_ref[...], preferred_element_type=F32)
    gb_ref[...] = acc[:, :d].astype(BF16)
    u_ref[...] = (acc[:, d:2 * d] * acc[:, 2 * d:]).astype(BF16)


def _conv_proj(xa, mod, r, layer, g, w, n_rows, n_lat_rows, n_seq, b):
    d = xa.shape[1]
    tm = ROW_TILE
    n_lat_tiles = n_lat_rows // tm
    return pl.pallas_call(
        functools.partial(_cproj_body, d=d),
        grid=(n_rows // tm,),
        in_specs=[
            pl.BlockSpec((tm, d), lambda i: (i, 0)),
            _mod_spec(layer, 0, r, d, n_lat_tiles, tm, n_seq, b),
            _mod_spec(layer, 1, r, d, n_lat_tiles, tm, n_seq, b),
            pl.BlockSpec((1, d), lambda i: (0, 0)),
            pl.BlockSpec((d, 3 * d), lambda i: (0, 0)),
        ],
        out_specs=[pl.BlockSpec((tm, d), lambda i: (i, 0)), pl.BlockSpec((tm, d), lambda i: (i, 0))],
        out_shape=[jax.ShapeDtypeStruct((n_rows, d), BF16), jax.ShapeDtypeStruct((n_rows, d), BF16)],
        compiler_params=_params("parallel"),
        name="conv_in_proj",
    )(xa, mod, mod, g[None, :], w.astype(BF16))


def _cout_body(u_ref, up_ref, un_ref, gb_ref, cw_ref, w_ref, x_ref, g1_ref, out_ref, *, tm, n_lat_rows, n_seq, n_ctx):
    i = pl.program_id(0)
    u = u_ref[...].astype(F32)
    prev_last = up_ref[BF16_SUBLANES - 1:BF16_SUBLANES, :].astype(F32)
    next_first = un_ref[0:1, :].astype(F32)
    row = lax.broadcasted_iota(I32, (tm, 1), 0)
    grow = i * tm + row
    is_lat = grow < n_lat_rows
    pos = jnp.where(is_lat, lax.rem(grow, n_seq), lax.rem(jnp.maximum(grow - n_lat_rows, 0), n_ctx))
    seq_len = jnp.where(is_lat, n_seq, n_ctx)
    um = jnp.where(row == 0, prev_last, pltpu.roll(u, 1, 0))
    um = jnp.where(pos == 0, 0.0, um)
    un = jnp.where(row == tm - 1, next_first, pltpu.roll(u, tm - 1, 0))
    un = jnp.where(pos == seq_len - 1, 0.0, un)
    cw = cw_ref[...]
    conv = um * cw[0:1] + u * cw[1:2] + un * cw[2:3]
    z = gb_ref[...].astype(F32) * conv
    y = jnp.dot(z.astype(BF16), w_ref[...], preferred_element_type=F32)
    out_ref[...] = x_ref[...] + g1_ref[0] * y


def _conv_out(u, gb, conv_w, w, xa, mod, r, layer, n_lat_rows, n_seq, n_ctx, b):
    n_rows, d = u.shape
    tm = ROW_TILE
    halo = BF16_SUBLANES
    n_halo_blocks = n_rows // halo
    body = functools.partial(_cout_body, tm=tm, n_lat_rows=n_lat_rows, n_seq=n_seq, n_ctx=n_ctx)
    return pl.pallas_call(
        body,
        grid=(n_rows // tm,),
        in_specs=[
            pl.BlockSpec((tm, d), lambda i: (i, 0)),
            pl.BlockSpec((halo, d), lambda i: (jnp.maximum(i * (tm // halo) - 1, 0), 0)),
            pl.BlockSpec((halo, d), lambda i: (jnp.minimum((i + 1) * (tm // halo), n_halo_blocks - 1), 0)),
            pl.BlockSpec((tm, d), lambda i: (i, 0)),
            pl.BlockSpec((3, d), lambda i: (0, 0)),
            pl.BlockSpec((d, d), lambda i: (0, 0)),
            pl.BlockSpec((tm, d), lambda i: (i, 0)),
            _mod_spec(layer, 2, r, d, n_lat_rows // tm, tm, n_seq, b),
        ],
        out_specs=pl.BlockSpec((tm, d), lambda i: (i, 0)),
        out_shape=jax.ShapeDtypeStruct((n_rows, d), F32),
        compiler_params=_params("parallel"),
        name="conv_out_proj",
    )(u, u, u, gb, conv_w, w.astype(BF16), xa, mod)


def _router_body(x_ref, sh_ref, sc_ref, g_ref, rwt_ref, rb_ref, h_ref, idx_ref, gate_ref, *, tm, d, n_exp):
    h = _norm_mod(x_ref[...], g_ref[...], sh_ref[0], sc_ref[0])
    slabs = d // LANES
    for s in range(slabs):
        h_ref[pl.ds(s, tm, stride=slabs), :] = h[:, s * LANES:(s + 1) * LANES]
    logits = lax.dot_general(rwt_ref[...], h, _NT, preferred_element_type=F32,
                             precision=lax.Precision.HIGHEST) + rb_ref[...]
    eidx = lax.broadcasted_iota(I32, (n_exp, tm), 0)
    tops, sels = [], []
    for _ in range(TOP_K):
        m = jnp.max(logits, axis=0, keepdims=True)
        sel = jnp.min(jnp.where(logits == m, eidx, n_exp), axis=0, keepdims=True)
        tops.append(m)
        sels.append(sel)
        logits = jnp.where(eidx == sel, -jnp.inf, logits)
    es = [jnp.exp(m - tops[0]) for m in tops]
    den = functools.reduce(jnp.add, es)
    idx_ref[...] = jnp.concatenate(sels, axis=0)
    gate_ref[...] = jnp.concatenate([e / den for e in es], axis=0)


def _router(xa, mod, r, layer, g, router_w, router_b, n_rows, n_lat_rows, n_seq, b):
    d = xa.shape[1]
    n_exp = router_w.shape[1]
    tm = ROW_TILE
    slabs = d // LANES
    n_lat_tiles = n_lat_rows // tm
    body = functools.partial(_router_body, tm=tm, d=d, n_exp=n_exp)
    return pl.pallas_call(
        body,
        grid=(n_rows // tm,),
        in_specs=[
            pl.BlockSpec((tm, d), lambda i: (i, 0)),
            _mod_spec(layer, 3, r, d, n_lat_tiles, tm, n_seq, b),
            _mod_spec(layer, 4, r, d, n_lat_tiles, tm, n_seq, b),
            pl.BlockSpec((1, d), lambda i: (0, 0)),
            pl.BlockSpec((n_exp, d), lambda i: (0, 0)),
            pl.BlockSpec((n_exp, 1), lambda i: (0, 0)),
        ],
        out_specs=[
            pl.BlockSpec((tm * slabs, LANES), lambda i: (i, 0)),
            pl.BlockSpec((TOP_K, tm), lambda i: (0, i)),
            pl.BlockSpec((TOP_K, tm), lambda i: (0, i)),
        ],
        out_shape=[
            jax.ShapeDtypeStruct((n_rows * slabs, LANES), F32),
            jax.ShapeDtypeStruct((TOP_K, n_rows), I32),
            jax.ShapeDtypeStruct((TOP_K, n_rows), F32),
        ],
        compiler_params=_params("parallel"),
        name="moe_router",
    )(xa, mod, mod, g[None, :], router_w.T, router_b[:, None])


def _dispatch_plan(top_idx, n_exp, tile):
    n_tok = top_idx.shape[1]
    n_assign = TOP_K * n_tok
    n_tiles = n_assign // tile
    flat_e = top_idx.T.reshape(-1)
    order = jnp.argsort(flat_e).astype(I32)
    tok = order // TOP_K
    row_token = tok.reshape(n_tiles, 1, tile)
    row_slot = ((order % TOP_K) * n_tok + tok).reshape(n_tiles, 1, tile)
    counts = jnp.bincount(flat_e, length=n_exp).astype(I32)
    end = jnp.cumsum(counts)
    start = end - counts
    first_tile = start // tile
    visits = jnp.where(counts > 0, (end - 1) // tile - first_tile + 1, 0)
    visit_end = jnp.cumsum(visits)
    visit_start = visit_end - visits
    n_visits = visit_end[-1]
    max_visits = n_tiles + n_exp - 1
    v = jnp.arange(max_visits, dtype=I32)
    valid = v < n_visits
    e_of = jnp.minimum(jnp.searchsorted(visit_end, v, side="right"), n_exp - 1).astype(I32)
    t_of = first_tile[e_of] + v - visit_start[e_of]
    lo = jnp.maximum(start[e_of], t_of * tile) - t_of * tile
    hi = jnp.minimum(end[e_of], (t_of + 1) * tile) - t_of * tile
    last = jnp.maximum(n_visits - 1, 0)
    e_of = jnp.where(valid, e_of, e_of[last])
    t_of = jnp.where(valid, t_of, t_of[last])
    lo = jnp.where(valid, lo, 0)
    hi = jnp.where(valid, hi, 0)
    return row_token, row_slot, t_of.astype(I32), e_of, lo.astype(I32), hi.astype(I32), n_visits.reshape(1).astype(I32)


def _expert_body(tile_ref, exp_ref, lo_ref, hi_ref, nv_ref,
                 tok_ref, tokn_ref, slot_ref, h_hbm, wg_ref, wl_ref, wd_ref, bg_ref, bl_ref, bd_ref,
                 y_hbm, gbuf, xs, yacc, ybuf, gsem, ssem, *, tile, slabs, n_tiles):
    del exp_ref
    v = pl.program_id(0)
    n_visits = nv_ref[0]
    t = tile_ref[v]
    first_visit = jnp.logical_or(v == 0, tile_ref[jnp.maximum(v - 1, 0)] != t)
    last_visit = jnp.logical_or(v == n_visits - 1, tile_ref[jnp.minimum(v + 1, pl.num_programs(0) - 1)] != t)
    rows = tile * slabs

    def start_gather(idx_ref):
        def issue(r, carry):
            src = pl.multiple_of(idx_ref[0, 0, r] * slabs, slabs)
            dst = pl.multiple_of(r * slabs, slabs)
            pltpu.make_async_copy(h_hbm.at[pl.ds(src, slabs), :], gbuf.at[pl.ds(dst, slabs), :], gsem.at[0]).start()
            return carry
        lax.fori_loop(0, tile, issue, 0, unroll=8)

    def wait_gather():
        pltpu.make_async_copy(h_hbm.at[pl.ds(0, rows), :], gbuf, gsem.at[0]).wait()

    def start_scatter():
        def issue(r, carry):
            src = pl.multiple_of(r * slabs, slabs)
            dst = pl.multiple_of(slot_ref[0, 0, r] * slabs, slabs)
            pltpu.make_async_copy(ybuf.at[pl.ds(src, slabs), :], y_hbm.at[pl.ds(dst, slabs), :], ssem.at[0]).start()
            return carry
        lax.fori_loop(0, tile, issue, 0, unroll=8)

    def wait_scatter():
        pltpu.make_async_copy(ybuf, y_hbm.at[pl.ds(0, rows), :], ssem.at[0]).wait()

    @pl.when(v < n_visits)
    def _():
        @pl.when(v == 0)
        def _():
            start_gather(tok_ref)

        @pl.when(first_visit)
        def _():
            wait_gather()
            for s in range(slabs):
                xs[:, s * LANES:(s + 1) * LANES] = gbuf[pl.ds(s, tile, stride=slabs), :].astype(BF16)

            @pl.when(t + 1 < n_tiles)
            def _():
                start_gather(tokn_ref)

        x = xs[...]
        gl = jnp.dot(x, wg_ref[0], preferred_element_type=F32) + bg_ref[0]
        ln = jnp.dot(x, wl_ref[0], preferred_element_type=F32) + bl_ref[0]
        x_glu = jnp.minimum(gl, SWIGLU_LIMIT)
        x_lin = jnp.clip(ln, -SWIGLU_LIMIT, SWIGLU_LIMIT)
        act = x_glu * jax.nn.sigmoid(SWIGLU_ALPHA * x_glu) * (x_lin + 1.0)
        y = jnp.dot(act.astype(BF16), wd_ref[0], preferred_element_type=F32) + bd_ref[0]

        @pl.when(first_visit)
        def _():
            yacc[...] = y

        @pl.when(jnp.logical_not(first_visit))
        def _():
            row = lax.broadcasted_iota(I32, (tile, 1), 0)
            mine = jnp.logical_and(row >= lo_ref[v], row < hi_ref[v])
            yacc[...] = jnp.where(mine, y, yacc[...])

        @pl.when(last_visit)
        def _():
            @pl.when(t > 0)
            def _():
                wait_scatter()

            for s in range(slabs):
                ybuf[pl.ds(s, tile, stride=slabs), :] = yacc[:, s * LANES:(s + 1) * LANES]
            start_scatter()

            @pl.when(v == n_visits - 1)
            def _():
                wait_scatter()


def _experts(h_slab, plan, wg, wl, wd, bg, bl, bd, n_tok):
    row_token, row_slot, t_of, e_of, lo, hi, n_visits = plan
    n_tiles, _, tile = row_token.shape
    n_exp, d, f = wg.shape
    slabs = d // LANES
    max_visits = t_of.shape[0]
    body = functools.partial(_expert_body, tile=tile, slabs=slabs, n_tiles=n_tiles)
    smem_tile = functools.partial(pl.BlockSpec, (1, 1, tile), memory_space=pltpu.SMEM)
    grid_spec = pltpu.PrefetchScalarGridSpec(
        num_scalar_prefetch=5,
        grid=(max_visits,),
        in_specs=[
            smem_tile(lambda v, tl, ex, lo_, hi_, nv: (tl[v], 0, 0)),
            smem_tile(lambda v, tl, ex, lo_, hi_, nv: (jnp.minimum(tl[v] + 1, n_tiles - 1), 0, 0)),
            smem_tile(lambda v, tl, ex, lo_, hi_, nv: (tl[v], 0, 0)),
            pl.BlockSpec(memory_space=pl.ANY),
            pl.BlockSpec((1, d, f), lambda v, tl, ex, lo_, hi_, nv: (ex[v], 0, 0)),
            pl.BlockSpec((1, d, f), lambda v, tl, ex, lo_, hi_, nv: (ex[v], 0, 0)),
            pl.BlockSpec((1, f, d), lambda v, tl, ex, lo_, hi_, nv: (ex[v], 0, 0)),
            pl.BlockSpec((1, 1, f), lambda v, tl, ex, lo_, hi_, nv: (ex[v], 0, 0)),
            pl.BlockSpec((1, 1, f), lambda v, tl, ex, lo_, hi_, nv: (ex[v], 0, 0)),
            pl.BlockSpec((1, 1, d), lambda v, tl, ex, lo_, hi_, nv: (ex[v], 0, 0)),
        ],
        out_specs=pl.BlockSpec(memory_space=pl.ANY),
        scratch_shapes=[
            pltpu.VMEM((tile * slabs, LANES), F32),
            pltpu.VMEM((tile, d), BF16),
            pltpu.VMEM((tile, d), F32),
            pltpu.VMEM((tile * slabs, LANES), F32),
            pltpu.SemaphoreType.DMA((1,)),
            pltpu.SemaphoreType.DMA((1,)),
        ],
    )
    return pl.pallas_call(
        body,
        grid_spec=grid_spec,
        out_shape=jax.ShapeDtypeStruct((TOP_K * n_tok * slabs, LANES), F32),
        compiler_params=_params("arbitrary"),
        name="moe_experts",
    )(t_of, e_of, lo, hi, n_visits, row_token, row_token, row_slot, h_slab, wg, wl, wd, bg, bl, bd)


def _combine_body(x_ref, g2_ref, gate_ref, *refs, tm, slabs):
    y_refs, out_ref = refs[:TOP_K], refs[TOP_K]
    gates = gate_ref[...]
    g2 = g2_ref[0]
    for s in range(slabs):
        acc = None
        for k in range(TOP_K):
            part = gates[:, k:k + 1] * y_refs[k][pl.ds(s, tm, stride=slabs), :]
            acc = part if acc is None else acc + part
        cols = slice(s * LANES, (s + 1) * LANES)
        out_ref[:, cols] = x_ref[:, cols] + g2[:, cols] * acc


def _combine(xa, y_slab, gates_t, mod, r, layer, n_rows, n_lat_rows, n_seq, b):
    d = xa.shape[1]
    tm = ROW_TILE
    slabs = d // LANES
    n_tiles = n_rows // tm
    body = functools.partial(_combine_body, tm=tm, slabs=slabs)
    y_specs = [pl.BlockSpec((tm * slabs, LANES), functools.partial(lambda i, k: (k * n_tiles + i, 0), k=k))
               for k in range(TOP_K)]
    return pl.pallas_call(
        body,
        grid=(n_tiles,),
        in_specs=[
            pl.BlockSpec((tm, d), lambda i: (i, 0)),
            _mod_spec(layer, 5, r, d, n_lat_rows // tm, tm, n_seq, b),
            pl.BlockSpec((tm, TOP_K), lambda i: (i, 0)),
        ] + y_specs,
        out_specs=pl.BlockSpec((tm, d), lambda i: (i, 0)),
        out_shape=jax.ShapeDtypeStruct((n_rows, d), F32),
        compiler_params=_params("parallel"),
        name="moe_combine",
    )(xa, mod, gates_t, *([y_slab] * TOP_K))


def _moe(xa, mod, r, layer, g, router_w, router_b, w_gu, b_gu, w_down, b_down, n_rows, n_lat_rows, n_seq, b):
    n_exp = router_w.shape[1]
    h_slab, top_idx, gates = _router(xa, mod, r, layer, g, router_w, router_b, n_rows, n_lat_rows, n_seq, b)
    plan = _dispatch_plan(top_idx, n_exp, MOE_TILE)
    wg = w_gu[:, :, 0::2].astype(BF16)
    wl = w_gu[:, :, 1::2].astype(BF16)
    bg = b_gu[:, None, 0::2]
    bl = b_gu[:, None, 1::2]
    y_slab = _experts(h_slab, plan, wg, wl, w_down.astype(BF16), bg, bl, b_down[:, None, :], n_rows)
    return _combine(xa, y_slab, gates.T, mod, r, layer, n_rows, n_lat_rows, n_seq, b)


def _final_body(x_ref, g_ref, o_ref):
    x = x_ref[...]
    ms = jnp.mean(x * x, axis=-1, keepdims=True)
    o_ref[...] = x * lax.rsqrt(ms + EPS) * g_ref[...]


def _final_norm(xa, g, n_rows):
    d = xa.shape[1]
    tm = ROW_TILE
    return pl.pallas_call(
        _final_body,
        grid=(n_rows // tm,),
        in_specs=[pl.BlockSpec((tm, d), lambda i: (i, 0)), pl.BlockSpec((1, d), lambda i: (0, 0))],
        out_specs=pl.BlockSpec((tm, d), lambda i: (i, 0)),
        out_shape=jax.ShapeDtypeStruct((n_rows, d), F32),
        compiler_params=_params("parallel"),
        name="final_norm",
    )(xa, g[None, :])


def kernel(x, c, ctx, c_ctx, mod_w, mod_b, norm_g, attn_w_qkv, attn_w_o, attn_lambda, attn_subln_g, conv_w_in, conv_w, conv_w_out, router_w, router_b, moe_w_gu, moe_b_gu, moe_w_down, moe_b_down, final_g):
    b, n_seq, d = x.shape
    n_ctx = ctx.shape[1]
    depth = mod_w.shape[0]
    n_lat_rows = b * n_seq
    n_all_rows = n_lat_rows + b * n_ctx
    assert d % LANES == 0 and d == moe_w_down.shape[-1]
    assert n_seq % ROW_TILE == 0 and (b * n_ctx) % ROW_TILE == 0 and n_seq % GRID_W == 0
    assert ROW_TILE % n_ctx == 0 or n_ctx % ROW_TILE == 0
    assert n_lat_rows % n_ctx == 0 and n_ctx % BF16_SUBLANES == 0
    assert (TOP_K * n_lat_rows) % MOE_TILE == 0 and (TOP_K * n_all_rows) % MOE_TILE == 0

    xa = jnp.concatenate([x.reshape(n_lat_rows, d), ctx.reshape(b * n_ctx, d)], axis=0)
    mod, r = _modulation(c, c_ctx, mod_w, mod_b)
    for i in range(depth):
        update_ctx = i < depth - 1
        attn_layer = i % N_MIXERS == 0
        j = i // N_MIXERS
        n_out = n_all_rows if update_ctx else n_lat_rows
        if attn_layer:
            lam_init = 0.8 - 0.6 * math.exp(-0.3 * i)
            qkv = _qkv_proj(xa, mod, r, i, norm_g[i, 0], attn_w_qkv[j], n_all_rows, n_lat_rows, n_seq, b)
            o = _attention(qkv, attn_lambda[j], attn_subln_g[j], lam_init, b, n_seq, n_ctx, n_lat_rows, update_ctx)
            xa = _out_proj(o, attn_w_o[j], xa, mod, r, i, n_lat_rows, n_seq, b)
        else:
            gb, u = _conv_proj(xa, mod, r, i, norm_g[i, 0], conv_w_in[j], n_out, n_lat_rows, n_seq, b)
            xa = _conv_out(u, gb, conv_w[j], conv_w_out[j], xa, mod, r, i, n_lat_rows, n_seq, n_ctx, b)
        xa = _moe(xa, mod, r, i, norm_g[i, 1], router_w[i], router_b[i], moe_w_gu[i], moe_b_gu[i],
                  moe_w_down[i], moe_b_down[i], n_out, n_lat_rows, n_seq, b)
    return _final_norm(xa, final_g, n_lat_rows).reshape(b, n_seq, d)
```

```python
import functools
import math

import jax
import jax.numpy as jnp
from jax import lax
from jax.experimental import pallas as pl
from jax.experimental.pallas import tpu as pltpu

F32 = jnp.float32
BF16 = jnp.bfloat16
I32 = jnp.int32

GRID_W = 64
HEAD_DIM = 64
ROPE_BASE = 10000.0
N_MIXERS = 2
TOP_K = 4
SWIGLU_LIMIT = 7.0
SWIGLU_ALPHA = 1.702
N_MOD = 6
EPS = 1e-5

LANES = 128
SUBLANES = 8
BF16_SUBLANES = 16
VMEM_LIMIT_BYTES = 48 * 1024 * 1024

ROW_TILE = 512
ATTN_Q_TILE = 256
MOE_TILE = 256

_NT = (((1,), (1,)), ((), ()))


def _round_up(a, b):
    return (a + b - 1) // b * b


def _params(*sem):
    return pltpu.CompilerParams(dimension_semantics=sem, vmem_limit_bytes=VMEM_LIMIT_BYTES)


def _norm_mod(x, g, shift, scale):
    ms = jnp.mean(x * x, axis=-1, keepdims=True)
    return (x * lax.rsqrt(ms + EPS) * g) * (1.0 + scale) + shift


def _mod_body(c_ref, w_ref, b_ref, o_ref):
    c = c_ref[...]
    s = c * jax.nn.sigmoid(c)
    o_ref[0] = jnp.dot(s, w_ref[0], preferred_element_type=F32, precision=lax.Precision.HIGHEST) + b_ref[0]


def _modulation(c, c_ctx, mod_w, mod_b):
    depth, d, nd = mod_w.shape
    b = c.shape[0]
    r = _round_up(b + 1, SUBLANES)
    cv = jnp.concatenate([c, c_ctx[None, :], jnp.zeros((r - b - 1, d), F32)], axis=0)
    tn = d
    out = pl.pallas_call(
        _mod_body,
        grid=(depth, nd // tn),
        in_specs=[
            pl.BlockSpec((r, d), lambda l, j: (0, 0)),
            pl.BlockSpec((1, d, tn), lambda l, j: (l, 0, j)),
            pl.BlockSpec((1, 1, tn), lambda l, j: (l, 0, j)),
        ],
        out_specs=pl.BlockSpec((1, r, tn), lambda l, j: (l, 0, j)),
        out_shape=jax.ShapeDtypeStruct((depth, r, nd), F32),
        compiler_params=_params("parallel", "parallel"),
        name="modulation",
    )(cv, mod_w, mod_b.reshape(depth, 1, nd))
    return out.reshape(depth * r, 1, nd), r


def _mod_spec(layer, chunk, r, d, n_lat_tiles, tm, n_seq, b):
    def index(i, *_):
        row = jnp.where(i < n_lat_tiles, (i * tm) // n_seq, b)
        return (layer * r + row, 0, chunk)

    return pl.BlockSpec((1, 1, d), index)


def _rope_tables(n, tm):
    rows = n // GRID_W
    row_pos = jnp.repeat(jnp.arange(rows, dtype=F32), GRID_W)
    col_pos = jnp.tile(jnp.arange(GRID_W, dtype=F32), rows)
    axis_dims = HEAD_DIM // 2
    inv_freq = ROPE_BASE ** (-jnp.arange(0, axis_dims, 2, dtype=F32) / axis_dims)
    ang = jnp.concatenate([row_pos[:, None] * inv_freq, col_pos[:, None] * inv_freq], axis=-1)
    cos, sin = jnp.cos(ang), jnp.sin(ang)
    reps = LANES // (HEAD_DIM // 2)
    cos_t = jnp.tile(cos, (1, reps))
    sign = jnp.where((jnp.arange(LANES) % HEAD_DIM) < HEAD_DIM // 2, -1.0, 1.0).astype(F32)
    sin_t = jnp.tile(sin, (1, reps)) * sign
    cos_t = jnp.concatenate([cos_t, jnp.ones((tm, LANES), F32)], axis=0)
    sin_t = jnp.concatenate([sin_t, jnp.zeros((tm, LANES), F32)], axis=0)
    return cos_t, sin_t


def _qkv_body(x_ref, sh_ref, sc_ref, g_ref, w_ref, cos_ref, sin_ref, o_ref, *, d):
    h = _norm_mod(x_ref[...], g_ref[...], sh_ref[0], sc_ref[0])
    acc = jnp.dot(h.astype(BF16), w_ref[...], preferred_element_type=F32)
    cos = cos_ref[...]
    sin = sin_ref[...]
    lane = lax.broadcasted_iota(I32, (1, LANES), 1)
    first_half = (lane % HEAD_DIM) < HEAD_DIM // 2
    q_scale = HEAD_DIM**-0.5
    for gi in range(2 * d // LANES):
        xg = acc[:, gi * LANES:(gi + 1) * LANES]
        partner = jnp.where(first_half, pltpu.roll(xg, LANES - HEAD_DIM // 2, 1), pltpu.roll(xg, HEAD_DIM // 2, 1))
        r = xg * cos + partner * sin
        if gi < d // LANES:
            r = r * q_scale
        o_ref[:, gi * LANES:(gi + 1) * LANES] = r.astype(BF16)
    o_ref[:, 2 * d:] = acc[:, 2 * d:].astype(BF16)


def _qkv_proj(xa, mod, r, layer, g, w, n_rows, n_lat_rows, n_seq, b):
    d = xa.shape[1]
    tm = ROW_TILE
    n_lat_tiles = n_lat_rows // tm
    tiles_per_seq = n_seq // tm
    cos_t, sin_t = _rope_tables(n_seq, tm)

    def rope_index(i):
        return (jnp.where(i < n_lat_tiles, i % tiles_per_seq, tiles_per_seq), 0)

    return pl.pallas_call(
        functools.partial(_qkv_body, d=d),
        grid=(n_rows // tm,),
        in_specs=[
            pl.BlockSpec((tm, d), lambda i: (i, 0)),
            _mod_spec(layer, 0, r, d, n_lat_tiles, tm, n_seq, b),
            _mod_spec(layer, 1, r, d, n_lat_tiles, tm, n_seq, b),
            pl.BlockSpec((1, d), lambda i: (0, 0)),
            pl.BlockSpec((d, 3 * d), lambda i: (0, 0)),
            pl.BlockSpec((tm, LANES), rope_index),
            pl.BlockSpec((tm, LANES), rope_index),
        ],
        out_specs=pl.BlockSpec((tm, 3 * d), lambda i: (i, 0)),
        out_shape=jax.ShapeDtypeStruct((n_rows, 3 * d), BF16),
        compiler_params=_params("parallel"),
        name="qkv_proj",
    )(xa, mod, mod, g[None, :], w.astype(BF16), cos_t, sin_t)


def _diff_attention(lam_ref, sg_ref, q_ref, k_refs, v_refs, o_ref, lam_init):
    q = q_ref[...]
    lane = lax.broadcasted_iota(I32, (1, LANES), 1)
    lp = lam_ref[...]
    lam = (jnp.exp(jnp.sum(lp[0:1] * lp[1:2], axis=-1, keepdims=True))
           - jnp.exp(jnp.sum(lp[2:3] * lp[3:4], axis=-1, keepdims=True)) + lam_init)
    probs = []
    for comp in range(2):
        in_comp = (lane < HEAD_DIM) if comp == 0 else (lane >= HEAD_DIM)
        qc = jnp.where(in_comp, q, jnp.zeros_like(q))
        ss = [lax.dot_general(qc, k[...], _NT, preferred_element_type=F32) for k in k_refs]
        m = functools.reduce(jnp.maximum, [jnp.max(s, axis=-1, keepdims=True) for s in ss])
        es = [jnp.exp(s - m) for s in ss]
        den = functools.reduce(jnp.add, [jnp.sum(e, axis=-1, keepdims=True) for e in es])
        row_scale = (1.0 if comp == 0 else lam) / den
        probs.append([e * row_scale for e in es])
    o = None
    for j in range(len(k_refs)):
        a = probs[0][j] - probs[1][j]
        part = jnp.dot(a.astype(BF16), v_refs[j][...], preferred_element_type=F32)
        o = part if o is None else o + part
    ms = jnp.mean(o * o, axis=-1, keepdims=True)
    o_ref[...] = (o * lax.rsqrt(ms + EPS) * sg_ref[...] * (1.0 - lam_init)).astype(BF16)


def _attn_body(lam_ref, sg_ref, q_ref, kc_ref, kl_ref, vc_ref, vl_ref, o_ref, *, lam_init, q_tiles):
    i = pl.program_id(2)

    @pl.when(i < q_tiles)
    def _():
        _diff_attention(lam_ref, sg_ref, q_ref, [kc_ref, kl_ref], [vc_ref, vl_ref], o_ref, lam_init)

    @pl.when(i >= q_tiles)
    def _():
        _diff_attention(lam_ref, sg_ref, q_ref, [kc_ref], [vc_ref], o_ref, lam_init)


def _attention(qkv, lam_p, subln_g, lam_init, b, n_seq, n_ctx, n_lat_rows, update_ctx):
    d = qkv.shape[1] // 3
    heads = d // LANES
    tq = ATTN_Q_TILE
    assert n_seq % tq == 0 and n_ctx == tq
    q_tiles = n_seq // tq
    ctx_blk0 = n_lat_rows // n_ctx
    n_rows = n_lat_rows + (b * n_ctx if update_ctx else 0)
    steps = q_tiles + (1 if update_ctx else 0)

    def q_index(bi, h, i):
        return (jnp.where(i < q_tiles, bi * q_tiles + i, ctx_blk0 + bi), h)

    return pl.pallas_call(
        functools.partial(_attn_body, lam_init=lam_init, q_tiles=q_tiles),
        grid=(b, heads, steps),
        in_specs=[
            pl.BlockSpec((4, HEAD_DIM), lambda bi, h, i: (0, 0)),
            pl.BlockSpec((1, LANES), lambda bi, h, i: (0, 0)),
            pl.BlockSpec((tq, LANES), q_index),
            pl.BlockSpec((n_ctx, LANES), lambda bi, h, i: (ctx_blk0 + bi, heads + h)),
            pl.BlockSpec((n_seq, LANES), lambda bi, h, i: (bi, heads + h)),
            pl.BlockSpec((n_ctx, LANES), lambda bi, h, i: (ctx_blk0 + bi, 2 * heads + h)),
            pl.BlockSpec((n_seq, LANES), lambda bi, h, i: (bi, 2 * heads + h)),
        ],
        out_specs=pl.BlockSpec((tq, LANES), q_index),
        out_shape=jax.ShapeDtypeStruct((n_rows, d), BF16),
        compiler_params=_params("parallel", "parallel", "arbitrary"),
        name="diff_attention",
    )(lam_p, subln_g[None, :], qkv, qkv, qkv, qkv, qkv)


def _oproj_body(o_ref, w_ref, x_ref, g1_ref, out_ref):
    y = jnp.dot(o_ref[...], w_ref[...], preferred_element_type=F32)
    out_ref[...] = x_ref[...] + g1_ref[0] * y


def _out_proj(o, w, xa, mod, r, layer, n_lat_rows, n_seq, b):
    n_rows, d = o.shape
    tm = ROW_TILE
    return pl.pallas_call(
        _oproj_body,
        grid=(n_rows // tm,),
        in_specs=[
            pl.BlockSpec((tm, d), lambda i: (i, 0)),
            pl.BlockSpec((d, d), lambda i: (0, 0)),
            pl.BlockSpec((tm, d), lambda i: (i, 0)),
            _mod_spec(layer, 2, r, d, n_lat_rows // tm, tm, n_seq, b),
        ],
        out_specs=pl.BlockSpec((tm, d), lambda i: (i, 0)),
        out_shape=jax.ShapeDtypeStruct((n_rows, d), F32),
        compiler_params=_params("parallel"),
        name="attn_out_proj",
    )(o, w.astype(BF16), xa, mod)


def _cproj_body(x_ref, sh_ref, sc_ref, g_ref, w_ref, gb_ref, u_ref, *, d):
    h = _norm_mod(x_ref[...], g_ref[...], sh_ref[0], sc_ref[0])
    acc = jnp.dot(h.astype(BF16), w_ref[...], preferred_element_type=F32)
    gb_ref[...] = acc[:, :d].astype(BF16)
    u_ref[...] = (acc[:, d:2 * d] * acc[:, 2 * d:]).astype(BF16)


def _conv_proj(xa, mod, r, layer, g, w, n_rows, n_lat_rows, n_seq, b):
    d = xa.shape[1]
    tm = ROW_TILE
    n_lat_tiles = n_lat_rows // tm
    return pl.pallas_call(
        functools.partial(_cproj_body, d=d),
        grid=(n_rows // tm,),
        in_specs=[
            pl.BlockSpec((tm, d), lambda i: (i, 0)),
            _mod_spec(layer, 0, r, d, n_lat_tiles, tm, n_seq, b),
            _mod_spec(layer, 1, r, d, n_lat_tiles, tm, n_seq, b),
            pl.BlockSpec((1, d), lambda i: (0, 0)),
            pl.BlockSpec((d, 3 * d), lambda i: (0, 0)),
        ],
        out_specs=[pl.BlockSpec((tm, d), lambda i: (i, 0)), pl.BlockSpec((tm, d), lambda i: (i, 0))],
        out_shape=[jax.ShapeDtypeStruct((n_rows, d), BF16), jax.ShapeDtypeStruct((n_rows, d), BF16)],
        compiler_params=_params("parallel"),
        name="conv_in_proj",
    )(xa, mod, mod, g[None, :], w.astype(BF16))


def _cout_body(u_ref, up_ref, un_ref, gb_ref, cw_ref, w_ref, x_ref, g1_ref, out_ref, *, tm, n_lat_rows, n_seq, n_ctx):
    i = pl.program_id(0)
    u = u_ref[...].astype(F32)
    prev_last = up_ref[BF16_SUBLANES - 1:BF16_SUBLANES, :].astype(F32)
    next_first = un_ref[0:1, :].astype(F32)
    row = lax.broadcasted_iota(I32, (tm, 1), 0)
    grow = i * tm + row
    is_lat = grow < n_lat_rows
    pos = jnp.where(is_lat, lax.rem(grow, n_seq), lax.rem(jnp.maximum(grow - n_lat_rows, 0), n_ctx))
    seq_len = jnp.where(is_lat, n_seq, n_ctx)
    um = jnp.where(row == 0, prev_last, pltpu.roll(u, 1, 0))
    um = jnp.where(pos == 0, 0.0, um)
    un = jnp.where(row == tm - 1, next_first, pltpu.roll(u, tm - 1, 0))
    un = jnp.where(pos == seq_len - 1, 0.0, un)
    cw = cw_ref[...]
    conv = um * cw[0:1] + u * cw[1:2] + un * cw[2:3]
    z = gb_ref[...].astype(F32) * conv
    y = jnp.dot(z.astype(BF16), w_ref[...], preferred_element_type=F32)
    out_ref[...] = x_ref[...] + g1_ref[0] * y


def _conv_out(u, gb, conv_w, w, xa, mod, r, layer, n_lat_rows, n_seq, n_ctx, b):
    n_rows, d = u.shape
    tm = ROW_TILE
    halo = BF16_SUBLANES
    n_halo_blocks = n_rows // halo
    body = functools.partial(_cout_body, tm=tm, n_lat_rows=n_lat_rows, n_seq=n_seq, n_ctx=n_ctx)
    return pl.pallas_call(
        body,
        grid=(n_rows // tm,),
        in_specs=[
            pl.BlockSpec((tm, d), lambda i: (i, 0)),
            pl.BlockSpec((halo, d), lambda i: (jnp.maximum(i * (tm // halo) - 1, 0), 0)),
            pl.BlockSpec((halo, d), lambda i: (jnp.minimum((i + 1) * (tm // halo), n_halo_blocks - 1), 0)),
            pl.BlockSpec((tm, d), lambda i: (i, 0)),
            pl.BlockSpec((3, d), lambda i: (0, 0)),
            pl.BlockSpec((d, d), lambda i: (0, 0)),
            pl.BlockSpec((tm, d), lambda i: (i, 0)),
            _mod_spec(layer, 2, r, d, n_lat_rows // tm, tm, n_seq, b),
        ],
        out_specs=pl.BlockSpec((tm, d), lambda i: (i, 0)),
        out_shape=jax.ShapeDtypeStruct((n_rows, d), F32),
        compiler_params=_params("parallel"),
        name="conv_out_proj",
    )(u, u, u, gb, conv_w, w.astype(BF16), xa, mod)


def _router_body(x_ref, sh_ref, sc_ref, g_ref, rwt_ref, rb_ref, h_ref, idx_ref, gate_ref, *, tm, d, n_exp):
    h = _norm_mod(x_ref[...], g_ref[...], sh_ref[0], sc_ref[0])
    slabs = d // LANES
    for s in range(slabs):
        h_ref[pl.ds(s, tm, stride=slabs), :] = h[:, s * LANES:(s + 1) * LANES]
    logits = lax.dot_general(rwt_ref[...], h, _NT, preferred_element_type=F32,
                             precision=lax.Precision.HIGHEST) + rb_ref[...]
    eidx = lax.broadcasted_iota(I32, (n_exp, tm), 0)
    tops, sels = [], []
    for _ in range(TOP_K):
        m = jnp.max(logits, axis=0, keepdims=True)
        sel = jnp.min(jnp.where(logits == m, eidx, n_exp), axis=0, keepdims=True)
        tops.append(m)
        sels.append(sel)
        logits = jnp.where(eidx == sel, -jnp.inf, logits)
    es = [jnp.exp(m - tops[0]) for m in tops]
    den = functools.reduce(jnp.add, es)
    idx_ref[...] = jnp.concatenate(sels, axis=0)
    gate_ref[...] = jnp.concatenate([e / den for e in es], axis=0)


def _router(xa, mod, r, layer, g, router_w, router_b, n_rows, n_lat_rows, n_seq, b):
    d = xa.shape[1]
    n_exp = router_w.shape[1]
    tm = ROW_TILE
    slabs = d // LANES
    n_lat_tiles = n_lat_rows // tm
    body = functools.partial(_router_body, tm=tm, d=d, n_exp=n_exp)
    return pl.pallas_call(
        body,
        grid=(n_rows // tm,),
        in_specs=[
            pl.BlockSpec((tm, d), lambda i: (i, 0)),
            _mod_spec(layer, 3, r, d, n_lat_tiles, tm, n_seq, b),
            _mod_spec(layer, 4, r, d, n_lat_tiles, tm, n_seq, b),
            pl.BlockSpec((1, d), lambda i: (0, 0)),
            pl.BlockSpec((n_exp, d), lambda i: (0, 0)),
            pl.BlockSpec((n_exp, 1), lambda i: (0, 0)),
        ],
        out_specs=[
            pl.BlockSpec((tm * slabs, LANES), lambda i: (i, 0)),
            pl.BlockSpec((TOP_K, tm), lambda i: (0, i)),
            pl.BlockSpec((TOP_K, tm), lambda i: (0, i)),
        ],
        out_shape=[
            jax.ShapeDtypeStruct((n_rows * slabs, LANES), F32),
            jax.ShapeDtypeStruct((TOP_K, n_rows), I32),
            jax.ShapeDtypeStruct((TOP_K, n_rows), F32),
        ],
        compiler_params=_params("parallel"),
        name="moe_router",
    )(xa, mod, mod, g[None, :], router_w.T, router_b[:, None])


def _dispatch_plan(top_idx, n_exp, tile, slabs):
    n_tok = top_idx.shape[1]
    n_assign = TOP_K * n_tok
    n_tiles = n_assign // tile
    flat_e = top_idx.T.reshape(-1)
    order = jnp.argsort(flat_e).astype(I32)
    tok = order // TOP_K
    row_token = (tok * slabs).reshape(n_tiles, 1, tile)
    row_slot = (((order % TOP_K) * n_tok + tok) * slabs).reshape(n_tiles, 1, tile)
    counts = jnp.sum((flat_e[:, None] == jnp.arange(n_exp, dtype=I32)[None, :]).astype(I32), axis=0)
    end = jnp.cumsum(counts)
    start = end - counts
    first_tile = start // tile
    visits = jnp.where(counts > 0, (end - 1) // tile - first_tile + 1, 0)
    visit_end = jnp.cumsum(visits)
    visit_start = visit_end - visits
    n_visits = visit_end[-1]
    max_visits = n_tiles + n_exp - 1
    v = jnp.arange(max_visits, dtype=I32)
    valid = v < n_visits
    e_of = jnp.minimum(jnp.sum((v[:, None] >= visit_end[None, :]).astype(I32), axis=1), n_exp - 1)
    t_of = first_tile[e_of] + v - visit_start[e_of]
    lo = jnp.maximum(start[e_of], t_of * tile) - t_of * tile
    hi = jnp.minimum(end[e_of], (t_of + 1) * tile) - t_of * tile
    last = jnp.maximum(n_visits - 1, 0)
    e_of = jnp.where(valid, e_of, e_of[last])
    t_of = jnp.where(valid, t_of, t_of[last])
    lo = jnp.where(valid, lo, 0)
    hi = jnp.where(valid, hi, 0)
    return row_token, row_slot, t_of.astype(I32), e_of, lo.astype(I32), hi.astype(I32), n_visits.reshape(1).astype(I32)


def _expert_body(tile_ref, exp_ref, lo_ref, hi_ref, nv_ref,
                 tok_ref, tokn_ref, slot_ref, h_hbm, wg_ref, wl_ref, wd_ref, bg_ref, bl_ref, bd_ref,
                 y_hbm, gbuf, xs, yacc, ybuf, gsem, ssem, *, tile, slabs, n_tiles):
    del exp_ref
    v = pl.program_id(0)
    n_visits = nv_ref[0]
    t = tile_ref[v]
    first_visit = jnp.logical_or(v == 0, tile_ref[jnp.maximum(v - 1, 0)] != t)
    last_visit = jnp.logical_or(v == n_visits - 1, tile_ref[jnp.minimum(v + 1, pl.num_programs(0) - 1)] != t)
    rows = tile * slabs

    def start_gather(idx_ref):
        for r in range(tile):
            src = pl.multiple_of(idx_ref[0, 0, r], slabs)
            pltpu.make_async_copy(h_hbm.at[pl.ds(src, slabs), :], gbuf.at[pl.ds(r * slabs, slabs), :],
                                  gsem.at[0]).start()

    def wait_gather():
        pltpu.make_async_copy(h_hbm.at[pl.ds(0, rows), :], gbuf, gsem.at[0]).wait()

    def start_scatter():
        for r in range(tile):
            dst = pl.multiple_of(slot_ref[0, 0, r], slabs)
            pltpu.make_async_copy(ybuf.at[pl.ds(r * slabs, slabs), :], y_hbm.at[pl.ds(dst, slabs), :],
                                  ssem.at[0]).start()

    def wait_scatter():
        pltpu.make_async_copy(ybuf, y_hbm.at[pl.ds(0, rows), :], ssem.at[0]).wait()

    @pl.when(v < n_visits)
    def _():
        @pl.when(v == 0)
        def _():
            start_gather(tok_ref)

        @pl.when(first_visit)
        def _():
            wait_gather()
            for s in range(slabs):
                xs[:, s * LANES:(s + 1) * LANES] = gbuf[pl.ds(s, tile, stride=slabs), :].astype(BF16)

            @pl.when(t + 1 < n_tiles)
            def _():
                start_gather(tokn_ref)

        x = xs[...]
        gl = lax.dot_general(x, wg_ref[0], _NT, preferred_element_type=F32) + bg_ref[0]
        ln = lax.dot_general(x, wl_ref[0], _NT, preferred_element_type=F32) + bl_ref[0]
        x_glu = jnp.minimum(gl, SWIGLU_LIMIT)
        x_lin = jnp.clip(ln, -SWIGLU_LIMIT, SWIGLU_LIMIT)
        act = x_glu * jax.nn.sigmoid(SWIGLU_ALPHA * x_glu) * (x_lin + 1.0)
        y = jnp.dot(act.astype(BF16), wd_ref[0], preferred_element_type=F32) + bd_ref[0]

        @pl.when(first_visit)
        def _():
            yacc[...] = y

        @pl.when(jnp.logical_not(first_visit))
        def _():
            row = lax.broadcasted_iota(I32, (tile, 1), 0)
            mine = jnp.logical_and(row >= lo_ref[v], row < hi_ref[v])
            yacc[...] = jnp.where(mine, y, yacc[...])

        @pl.when(last_visit)
        def _():
            @pl.when(t > 0)
            def _():
                wait_scatter()

            for s in range(slabs):
                ybuf[pl.ds(s, tile, stride=slabs), :] = yacc[:, s * LANES:(s + 1) * LANES]
            start_scatter()

            @pl.when(v == n_visits - 1)
            def _():
                wait_scatter()


def _experts(h_slab, plan, wg, wl, wd, bg, bl, bd, n_tok):
    row_token, row_slot, t_of, e_of, lo, hi, n_visits = plan
    n_tiles, _, tile = row_token.shape
    n_exp, f, d = wg.shape
    slabs = d // LANES
    max_visits = t_of.shape[0]
    body = functools.partial(_expert_body, tile=tile, slabs=slabs, n_tiles=n_tiles)
    smem_tile = functools.partial(pl.BlockSpec, (1, 1, tile), memory_space=pltpu.SMEM)
    grid_spec = pltpu.PrefetchScalarGridSpec(
        num_scalar_prefetch=5,
        grid=(max_visits,),
        in_specs=[
            smem_tile(lambda v, tl, ex, lo_, hi_, nv: (tl[v], 0, 0)),
            smem_tile(lambda v, tl, ex, lo_, hi_, nv: (jnp.minimum(tl[v] + 1, n_tiles - 1), 0, 0)),
            smem_tile(lambda v, tl, ex, lo_, hi_, nv: (tl[v], 0, 0)),
            pl.BlockSpec(memory_space=pl.ANY),
            pl.BlockSpec((1, f, d), lambda v, tl, ex, lo_, hi_, nv: (ex[v], 0, 0)),
            pl.BlockSpec((1, f, d), lambda v, tl, ex, lo_, hi_, nv: (ex[v], 0, 0)),
            pl.BlockSpec((1, f, d), lambda v, tl, ex, lo_, hi_, nv: (ex[v], 0, 0)),
            pl.BlockSpec((1, 1, f), lambda v, tl, ex, lo_, hi_, nv: (ex[v], 0, 0)),
            pl.BlockSpec((1, 1, f), lambda v, tl, ex, lo_, hi_, nv: (ex[v], 0, 0)),
            pl.BlockSpec((1, 1, d), lambda v, tl, ex, lo_, hi_, nv: (ex[v], 0, 0)),
        ],
        out_specs=pl.BlockSpec(memory_space=pl.ANY),
        scratch_shapes=[
            pltpu.VMEM((tile * slabs, LANES), F32),
            pltpu.VMEM((tile, d), BF16),
            pltpu.VMEM((tile, d), F32),
            pltpu.VMEM((tile * slabs, LANES), F32),
            pltpu.SemaphoreType.DMA((1,)),
            pltpu.SemaphoreType.DMA((1,)),
        ],
    )
    return pl.pallas_call(
        body,
        grid_spec=grid_spec,
        out_shape=jax.ShapeDtypeStruct((TOP_K * n_tok * slabs, LANES), F32),
        compiler_params=_params("arbitrary"),
        name="moe_experts",
    )(t_of, e_of, lo, hi, n_visits, row_token, row_token, row_slot, h_slab, wg, wl, wd, bg, bl, bd)


def _combine_body(x_ref, g2_ref, gate_ref, *refs, tm, slabs):
    y_refs, out_ref = refs[:TOP_K], refs[TOP_K]
    gates = gate_ref[...]
    g2 = g2_ref[0]
    for s in range(slabs):
        acc = None
        for k in range(TOP_K):
            part = gates[:, k:k + 1] * y_refs[k][pl.ds(s, tm, stride=slabs), :]
            acc = part if acc is None else acc + part
        cols = slice(s * LANES, (s + 1) * LANES)
        out_ref[:, cols] = x_ref[:, cols] + g2[:, cols] * acc


def _combine(xa, y_slab, gates_t, mod, r, layer, n_rows, n_lat_rows, n_seq, b):
    d = xa.shape[1]
    tm = ROW_TILE
    slabs = d // LANES
    n_tiles = n_rows // tm
    body = functools.partial(_combine_body, tm=tm, slabs=slabs)
    y_specs = [pl.BlockSpec((tm * slabs, LANES), functools.partial(lambda i, k: (k * n_tiles + i, 0), k=k))
               for k in range(TOP_K)]
    return pl.pallas_call(
        body,
        grid=(n_tiles,),
        in_specs=[
            pl.BlockSpec((tm, d), lambda i: (i, 0)),
            _mod_spec(layer, 5, r, d, n_lat_rows // tm, tm, n_seq, b),
            pl.BlockSpec((tm, TOP_K), lambda i: (i, 0)),
        ] + y_specs,
        out_specs=pl.BlockSpec((tm, d), lambda i: (i, 0)),
        out_shape=jax.ShapeDtypeStruct((n_rows, d), F32),
        compiler_params=_params("parallel"),
        name="moe_combine",
    )(xa, mod, gates_t, *([y_slab] * TOP_K))


def _moe(xa, mod, r, layer, g, router_w, router_b, w_gu, b_gu, w_down, b_down, n_rows, n_lat_rows, n_seq, b):
    n_exp = router_w.shape[1]
    h_slab, top_idx, gates = _router(xa, mod, r, layer, g, router_w, router_b, n_rows, n_lat_rows, n_seq, b)
    plan = _dispatch_plan(top_idx, n_exp, MOE_TILE, xa.shape[1] // LANES)
    w_gu_t = jnp.swapaxes(w_gu, 1, 2)
    wg = w_gu_t[:, 0::2, :].astype(BF16)
    wl = w_gu_t[:, 1::2, :].astype(BF16)
    bg = b_gu[:, None, 0::2]
    bl = b_gu[:, None, 1::2]
    y_slab = _experts(h_slab, plan, wg, wl, w_down.astype(BF16), bg, bl, b_down[:, None, :], n_rows)
    return _combine(xa, y_slab, gates.T, mod, r, layer, n_rows, n_lat_rows, n_seq, b)


def _final_body(x_ref, g_ref, o_ref):
    x = x_ref[...]
    ms = jnp.mean(x * x, axis=-1, keepdims=True)
    o_ref[...] = x * lax.rsqrt(ms + EPS) * g_ref[...]


def _final_norm(xa, g, n_rows):
    d = xa.shape[1]
    tm = ROW_TILE
    return pl.pallas_call(
        _final_body,
        grid=(n_rows // tm,),
        in_specs=[pl.BlockSpec((tm, d), lambda i: (i, 0)), pl.BlockSpec((1, d), lambda i: (0, 0))],
        out_specs=pl.BlockSpec((tm, d), lambda i: (i, 0)),
        out_shape=jax.ShapeDtypeStruct((n_rows, d), F32),
        compiler_params=_params("parallel"),
        name="final_norm",
    )(xa, g[None, :])


def kernel(x, c, ctx, c_ctx, mod_w, mod_b, norm_g, attn_w_qkv, attn_w_o, attn_lambda, attn_subln_g, conv_w_in, conv_w, conv_w_out, router_w, router_b, moe_w_gu, moe_b_gu, moe_w_down, moe_b_down, final_g):
    b, n_seq, d = x.shape
    n_ctx = ctx.shape[1]
    depth = mod_w.shape[0]
    n_lat_rows = b * n_seq
    n_all_rows = n_lat_rows + b * n_ctx
    assert d % LANES == 0 and d == moe_w_down.shape[-1]
    assert n_seq % ROW_TILE == 0 and (b * n_ctx) % ROW_TILE == 0 and n_seq % GRID_W == 0
    assert ROW_TILE % n_ctx == 0 or n_ctx % ROW_TILE == 0
    assert n_lat_rows % n_ctx == 0 and n_ctx % BF16_SUBLANES == 0
    assert (TOP_K * n_lat_rows) % MOE_TILE == 0 and (TOP_K * n_all_rows) % MOE_TILE == 0

    xa = jnp.concatenate([x.reshape(n_lat_rows, d), ctx.reshape(b * n_ctx, d)], axis=0)
    mod, r = _modulation(c, c_ctx, mod_w, mod_b)
    for i in range(depth):
        update_ctx = i < depth - 1
        attn_layer = i % N_MIXERS == 0
        j = i // N_MIXERS
        n_out = n_all_rows if update_ctx else n_lat_rows
        if attn_layer:
            lam_init = 0.8 - 0.6 * math.exp(-0.3 * i)
            qkv = _qkv_proj(xa, mod, r, i, norm_g[i, 0], attn_w_qkv[j], n_all_rows, n_lat_rows, n_seq, b)
            o = _attention(qkv, attn_lambda[j], attn_subln_g[j], lam_init, b, n_seq, n_ctx, n_lat_rows, update_ctx)
            xa = _out_proj(o, attn_w_o[j], xa, mod, r, i, n_lat_rows, n_seq, b)
        else:
            gb, u = _conv_proj(xa, mod, r, i, norm_g[i, 0], conv_w_in[j], n_out, n_lat_rows, n_seq, b)
            xa = _conv_out(u, gb, conv_w[j], conv_w_out[j], xa, mod, r, i, n_lat_rows, n_seq, n_ctx, b)
        xa = _moe(xa, mod, r, i, norm_g[i, 1], router_w[i], router_b[i], moe_w_gu[i], moe_b_gu[i],
                  moe_w_down[i], moe_b_down[i], n_out, n_lat_rows, n_seq, b)
    return _final_norm(xa, final_g, n_lat_rows).reshape(b, n_seq, d)
```

```python
import functools
import math

import jax
import jax.numpy as jnp
from jax import lax
from jax.experimental import pallas as pl
from jax.experimental.pallas import tpu as pltpu

F32 = jnp.float32
BF16 = jnp.bfloat16
I32 = jnp.int32

GRID_W = 64
HEAD_DIM = 64
ROPE_BASE = 10000.0
N_MIXERS = 2
TOP_K = 4
SWIGLU_LIMIT = 7.0
SWIGLU_ALPHA = 1.702
N_MOD = 6
EPS = 1e-5

LANES = 128
SUBLANES = 8
BF16_SUBLANES = 16
VMEM_LIMIT_BYTES = 48 * 1024 * 1024

ROW_TILE = 512
ATTN_Q_TILE = 256
MOE_TILE = 256

_NT = (((1,), (1,)), ((), ()))


def _round_up(a, b):
    return (a + b - 1) // b * b


def _params(*sem):
    return pltpu.CompilerParams(dimension_semantics=sem, vmem_limit_bytes=VMEM_LIMIT_BYTES)


def _norm_mod(x, g, shift, scale):
    ms = jnp.mean(x * x, axis=-1, keepdims=True)
    return (x * lax.rsqrt(ms + EPS) * g) * (1.0 + scale) + shift


def _mod_body(c_ref, w_ref, b_ref, o_ref):
    c = c_ref[...]
    s = c * jax.nn.sigmoid(c)
    o_ref[0] = jnp.dot(s, w_ref[0], preferred_element_type=F32, precision=lax.Precision.HIGHEST) + b_ref[0]


def _modulation(c, c_ctx, mod_w, mod_b):
    depth, d, nd = mod_w.shape
    b = c.shape[0]
    r = _round_up(b + 1, SUBLANES)
    cv = jnp.concatenate([c, c_ctx[None, :], jnp.zeros((r - b - 1, d), F32)], axis=0)
    tn = d
    out = pl.pallas_call(
        _mod_body,
        grid=(depth, nd // tn),
        in_specs=[
            pl.BlockSpec((r, d), lambda l, j: (0, 0)),
            pl.BlockSpec((1, d, tn), lambda l, j: (l, 0, j)),
            pl.BlockSpec((1, 1, tn), lambda l, j: (l, 0, j)),
        ],
        out_specs=pl.BlockSpec((1, r, tn), lambda l, j: (l, 0, j)),
        out_shape=jax.ShapeDtypeStruct((depth, r, nd), F32),
        compiler_params=_params("parallel", "parallel"),
        name="modulation",
    )(cv, mod_w, mod_b.reshape(depth, 1, nd))
    return out.reshape(depth * r, 1, nd), r


def _mod_spec(layer, chunk, r, d, n_lat_tiles, tm, n_seq, b):
    def index(i, *_):
        row = jnp.where(i < n_lat_tiles, (i * tm) // n_seq, b)
        return (layer * r + row, 0, chunk)

    return pl.BlockSpec((1, 1, d), index)


def _rope_tables(n, tm):
    rows = n // GRID_W
    row_pos = jnp.repeat(jnp.arange(rows, dtype=F32), GRID_W)
    col_pos = jnp.tile(jnp.arange(GRID_W, dtype=F32), rows)
    axis_dims = HEAD_DIM // 2
    inv_freq = ROPE_BASE ** (-jnp.arange(0, axis_dims, 2, dtype=F32) / axis_dims)
    ang = jnp.concatenate([row_pos[:, None] * inv_freq, col_pos[:, None] * inv_freq], axis=-1)
    cos, sin = jnp.cos(ang), jnp.sin(ang)
    reps = LANES // (HEAD_DIM // 2)
    cos_t = jnp.tile(cos, (1, reps))
    sign = jnp.where((jnp.arange(LANES) % HEAD_DIM) < HEAD_DIM // 2, -1.0, 1.0).astype(F32)
    sin_t = jnp.tile(sin, (1, reps)) * sign
    cos_t = jnp.concatenate([cos_t, jnp.ones((tm, LANES), F32)], axis=0)
    sin_t = jnp.concatenate([sin_t, jnp.zeros((tm, LANES), F32)], axis=0)
    return cos_t, sin_t


def _qkv_body(x_ref, sh_ref, sc_ref, g_ref, w_ref, cos_ref, sin_ref, o_ref, *, d):
    h = _norm_mod(x_ref[...], g_ref[...], sh_ref[0], sc_ref[0])
    acc = jnp.dot(h.astype(BF16), w_ref[...], preferred_element_type=F32)
    cos = cos_ref[...]
    sin = sin_ref[...]
    lane = lax.broadcasted_iota(I32, (1, LANES), 1)
    first_half = (lane % HEAD_DIM) < HEAD_DIM // 2
    q_scale = HEAD_DIM**-0.5
    for gi in range(2 * d // LANES):
        xg = acc[:, gi * LANES:(gi + 1) * LANES]
        partner = jnp.where(first_half, pltpu.roll(xg, LANES - HEAD_DIM // 2, 1), pltpu.roll(xg, HEAD_DIM // 2, 1))
        r = xg * cos + partner * sin
        if gi < d // LANES:
            r = r * q_scale
        o_ref[:, gi * LANES:(gi + 1) * LANES] = r.astype(BF16)
    o_ref[:, 2 * d:] = acc[:, 2 * d:].astype(BF16)


def _qkv_proj(xa, mod, r, layer, g, w, n_rows, n_lat_rows, n_seq, b):
    d = xa.shape[1]
    tm = ROW_TILE
    n_lat_tiles = n_lat_rows // tm
    tiles_per_seq = n_seq // tm
    cos_t, sin_t = _rope_tables(n_seq, tm)

    def rope_index(i):
        return (jnp.where(i < n_lat_tiles, i % tiles_per_seq, tiles_per_seq), 0)

    return pl.pallas_call(
        functools.partial(_qkv_body, d=d),
        grid=(n_rows // tm,),
        in_specs=[
            pl.BlockSpec((tm, d), lambda i: (i, 0)),
            _mod_spec(layer, 0, r, d, n_lat_tiles, tm, n_seq, b),
            _mod_spec(layer, 1, r, d, n_lat_tiles, tm, n_seq, b),
            pl.BlockSpec((1, d), lambda i: (0, 0)),
            pl.BlockSpec((d, 3 * d), lambda i: (0, 0)),
            pl.BlockSpec((tm, LANES), rope_index),
            pl.BlockSpec((tm, LANES), rope_index),
        ],
        out_specs=pl.BlockSpec((tm, 3 * d), lambda i: (i, 0)),
        out_shape=jax.ShapeDtypeStruct((n_rows, 3 * d), BF16),
        compiler_params=_params("parallel"),
        name="qkv_proj",
    )(xa, mod, mod, g[None, :], w.astype(BF16), cos_t, sin_t)


def _diff_attention(lam_ref, sg_ref, q_ref, k_refs, v_refs, o_ref, lam_init):
    q = q_ref[...]
    lane = lax.broadcasted_iota(I32, (1, LANES), 1)
    lp = lam_ref[...]
    lam = (jnp.exp(jnp.sum(lp[0:1] * lp[1:2], axis=-1, keepdims=True))
           - jnp.exp(jnp.sum(lp[2:3] * lp[3:4], axis=-1, keepdims=True)) + lam_init)
    probs = []
    for comp in range(2):
        in_comp = (lane < HEAD_DIM) if comp == 0 else (lane >= HEAD_DIM)
        qc = jnp.where(in_comp, q, jnp.zeros_like(q))
        ss = [lax.dot_general(qc, k[...], _NT, preferred_element_type=F32) for k in k_refs]
        m = functools.reduce(jnp.maximum, [jnp.max(s, axis=-1, keepdims=True) for s in ss])
        es = [jnp.exp(s - m) for s in ss]
        den = functools.reduce(jnp.add, [jnp.sum(e, axis=-1, keepdims=True) for e in es])
        row_scale = (1.0 if comp == 0 else lam) / den
        probs.append([e * row_scale for e in es])
    o = None
    for j in range(len(k_refs)):
        a = probs[0][j] - probs[1][j]
        part = jnp.dot(a.astype(BF16), v_refs[j][...], preferred_element_type=F32)
        o = part if o is None else o + part
    ms = jnp.mean(o * o, axis=-1, keepdims=True)
    o_ref[...] = (o * lax.rsqrt(ms + EPS) * sg_ref[...] * (1.0 - lam_init)).astype(BF16)


def _attn_body(lam_ref, sg_ref, q_ref, kc_ref, kl_ref, vc_ref, vl_ref, o_ref, *, lam_init, q_tiles):
    i = pl.program_id(2)

    @pl.when(i < q_tiles)
    def _():
        _diff_attention(lam_ref, sg_ref, q_ref, [kc_ref, kl_ref], [vc_ref, vl_ref], o_ref, lam_init)

    @pl.when(i >= q_tiles)
    def _():
        _diff_attention(lam_ref, sg_ref, q_ref, [kc_ref], [vc_ref], o_ref, lam_init)


def _attention(qkv, lam_p, subln_g, lam_init, b, n_seq, n_ctx, n_lat_rows, update_ctx):
    d = qkv.shape[1] // 3
    heads = d // LANES
    tq = ATTN_Q_TILE
    assert n_seq % tq == 0 and n_ctx == tq
    q_tiles = n_seq // tq
    ctx_blk0 = n_lat_rows // n_ctx
    n_rows = n_lat_rows + (b * n_ctx if update_ctx else 0)
    steps = q_tiles + (1 if update_ctx else 0)

    def q_index(bi, h, i):
        return (jnp.where(i < q_tiles, bi * q_tiles + i, ctx_blk0 + bi), h)

    return pl.pallas_call(
        functools.partial(_attn_body, lam_init=lam_init, q_tiles=q_tiles),
        grid=(b, heads, steps),
        in_specs=[
            pl.BlockSpec((4, HEAD_DIM), lambda bi, h, i: (0, 0)),
            pl.BlockSpec((1, LANES), lambda bi, h, i: (0, 0)),
            pl.BlockSpec((tq, LANES), q_index),
            pl.BlockSpec((n_ctx, LANES), lambda bi, h, i: (ctx_blk0 + bi, heads + h)),
            pl.BlockSpec((n_seq, LANES), lambda bi, h, i: (bi, heads + h)),
            pl.BlockSpec((n_ctx, LANES), lambda bi, h, i: (ctx_blk0 + bi, 2 * heads + h)),
            pl.BlockSpec((n_seq, LANES), lambda bi, h, i: (bi, 2 * heads + h)),
        ],
        out_specs=pl.BlockSpec((tq, LANES), q_index),
        out_shape=jax.ShapeDtypeStruct((n_rows, d), BF16),
        compiler_params=_params("parallel", "parallel", "arbitrary"),
        name="diff_attention",
    )(lam_p, subln_g[None, :], qkv, qkv, qkv, qkv, qkv)


def _oproj_body(o_ref, w_ref, x_ref, g1_ref, out_ref):
    y = jnp.dot(o_ref[...], w_ref[...], preferred_element_type=F32)
    out_ref[...] = x_ref[...] + g1_ref[0] * y


def _out_proj(o, w, xa, mod, r, layer, n_lat_rows, n_seq, b):
    n_rows, d = o.shape
    tm = ROW_TILE
    return pl.pallas_call(
        _oproj_body,
        grid=(n_rows // tm,),
        in_specs=[
            pl.BlockSpec((tm, d), lambda i: (i, 0)),
            pl.BlockSpec((d, d), lambda i: (0, 0)),
            pl.BlockSpec((tm, d), lambda i: (i, 0)),
            _mod_spec(layer, 2, r, d, n_lat_rows // tm, tm, n_seq, b),
        ],
        out_specs=pl.BlockSpec((tm, d), lambda i: (i, 0)),
        out_shape=jax.ShapeDtypeStruct((n_rows, d), F32),
        compiler_params=_params("parallel"),
        name="attn_out_proj",
    )(o, w.astype(BF16), xa, mod)


def _cproj_body(x_ref, sh_ref, sc_ref, g_ref, w_ref, gb_ref, u_ref, *, d):
    h = _norm_mod(x_ref[...], g_ref[...], sh_ref[0], sc_ref[0])
    acc = jnp.dot(h.astype(BF16), w_ref[...], preferred_element_type=F32)
    gb_ref[...] = acc[:, :d].astype(BF16)
    u_ref[...] = (acc[:, d:2 * d] * acc[:, 2 * d:]).astype(BF16)


def _conv_proj(xa, mod, r, layer, g, w, n_rows, n_lat_rows, n_seq, b):
    d = xa.shape[1]
    tm = ROW_TILE
    n_lat_tiles = n_lat_rows // tm
    return pl.pallas_call(
        functools.partial(_cproj_body, d=d),
        grid=(n_rows // tm,),
        in_specs=[
            pl.BlockSpec((tm, d), lambda i: (i, 0)),
            _mod_spec(layer, 0, r, d, n_lat_tiles, tm, n_seq, b),
            _mod_spec(layer, 1, r, d, n_lat_tiles, tm, n_seq, b),
            pl.BlockSpec((1, d), lambda i: (0, 0)),
            pl.BlockSpec((d, 3 * d), lambda i: (0, 0)),
        ],
        out_specs=[pl.BlockSpec((tm, d), lambda i: (i, 0)), pl.BlockSpec((tm, d), lambda i: (i, 0))],
        out_shape=[jax.ShapeDtypeStruct((n_rows, d), BF16), jax.ShapeDtypeStruct((n_rows, d), BF16)],
        compiler_params=_params("parallel"),
        name="conv_in_proj",
    )(xa, mod, mod, g[None, :], w.astype(BF16))


def _cout_body(u_ref, up_ref, un_ref, gb_ref, cw_ref, w_ref, x_ref, g1_ref, out_ref, *, tm, n_lat_rows, n_seq, n_ctx):
    i = pl.program_id(0)
    u = u_ref[...].astype(F32)
    prev_last = up_ref[BF16_SUBLANES - 1:BF16_SUBLANES, :].astype(F32)
    next_first = un_ref[0:1, :].astype(F32)
    row = lax.broadcasted_iota(I32, (tm, 1), 0)
    grow = i * tm + row
    is_lat = grow < n_lat_rows
    pos = jnp.where(is_lat, lax.rem(grow, n_seq), lax.rem(jnp.maximum(grow - n_lat_rows, 0), n_ctx))
    seq_len = jnp.where(is_lat, n_seq, n_ctx)
    um = jnp.where(row == 0, prev_last, pltpu.roll(u, 1, 0))
    um = jnp.where(pos == 0, 0.0, um)
    un = jnp.where(row == tm - 1, next_first, pltpu.roll(u, tm - 1, 0))
    un = jnp.where(pos == seq_len - 1, 0.0, un)
    cw = cw_ref[...]
    conv = um * cw[0:1] + u * cw[1:2] + un * cw[2:3]
    z = gb_ref[...].astype(F32) * conv
    y = jnp.dot(z.astype(BF16), w_ref[...], preferred_element_type=F32)
    out_ref[...] = x_ref[...] + g1_ref[0] * y


def _conv_out(u, gb, conv_w, w, xa, mod, r, layer, n_lat_rows, n_seq, n_ctx, b):
    n_rows, d = u.shape
    tm = ROW_TILE
    halo = BF16_SUBLANES
    n_halo_blocks = n_rows // halo
    body = functools.partial(_cout_body, tm=tm, n_lat_rows=n_lat_rows, n_seq=n_seq, n_ctx=n_ctx)
    return pl.pallas_call(
        body,
        grid=(n_rows // tm,),
        in_specs=[
            pl.BlockSpec((tm, d), lambda i: (i, 0)),
            pl.BlockSpec((halo, d), lambda i: (jnp.maximum(i * (tm // halo) - 1, 0), 0)),
            pl.BlockSpec((halo, d), lambda i: (jnp.minimum((i + 1) * (tm // halo), n_halo_blocks - 1), 0)),
            pl.BlockSpec((tm, d), lambda i: (i, 0)),
            pl.BlockSpec((3, d), lambda i: (0, 0)),
            pl.BlockSpec((d, d), lambda i: (0, 0)),
            pl.BlockSpec((tm, d), lambda i: (i, 0)),
            _mod_spec(layer, 2, r, d, n_lat_rows // tm, tm, n_seq, b),
        ],
        out_specs=pl.BlockSpec((tm, d), lambda i: (i, 0)),
        out_shape=jax.ShapeDtypeStruct((n_rows, d), F32),
        compiler_params=_params("parallel"),
        name="conv_out_proj",
    )(u, u, u, gb, conv_w, w.astype(BF16), xa, mod)


def _router_body(x_ref, sh_ref, sc_ref, g_ref, rwt_ref, rb_ref, h_ref, idx_ref, gate_ref, *, tm, d, n_exp):
    h = _norm_mod(x_ref[...], g_ref[...], sh_ref[0], sc_ref[0])
    slabs = d // LANES
    for s in range(slabs):
        h_ref[pl.ds(s, tm, stride=slabs), :] = h[:, s * LANES:(s + 1) * LANES]
    logits = lax.dot_general(rwt_ref[...], h, _NT, preferred_element_type=F32,
                             precision=lax.Precision.HIGHEST) + rb_ref[...]
    eidx = lax.broadcasted_iota(I32, (n_exp, tm), 0)
    tops, sels = [], []
    for _ in range(TOP_K):
        m = jnp.max(logits, axis=0, keepdims=True)
        sel = jnp.min(jnp.where(logits == m, eidx, n_exp), axis=0, keepdims=True)
        tops.append(m)
        sels.append(sel)
        logits = jnp.where(eidx == sel, -jnp.inf, logits)
    es = [jnp.exp(m - tops[0]) for m in tops]
    den = functools.reduce(jnp.add, es)
    idx_ref[...] = jnp.concatenate(sels, axis=0)
    gate_ref[...] = jnp.concatenate([e / den for e in es], axis=0)


def _router(xa, mod, r, layer, g, router_w, router_b, n_rows, n_lat_rows, n_seq, b):
    d = xa.shape[1]
    n_exp = router_w.shape[1]
    tm = ROW_TILE
    slabs = d // LANES
    n_lat_tiles = n_lat_rows // tm
    body = functools.partial(_router_body, tm=tm, d=d, n_exp=n_exp)
    return pl.pallas_call(
        body,
        grid=(n_rows // tm,),
        in_specs=[
            pl.BlockSpec((tm, d), lambda i: (i, 0)),
            _mod_spec(layer, 3, r, d, n_lat_tiles, tm, n_seq, b),
            _mod_spec(layer, 4, r, d, n_lat_tiles, tm, n_seq, b),
            pl.BlockSpec((1, d), lambda i: (0, 0)),
            pl.BlockSpec((n_exp, d), lambda i: (0, 0)),
            pl.BlockSpec((n_exp, 1), lambda i: (0, 0)),
        ],
        out_specs=[
            pl.BlockSpec((tm * slabs, LANES), lambda i: (i, 0)),
            pl.BlockSpec((TOP_K, tm), lambda i: (0, i)),
            pl.BlockSpec((TOP_K, tm), lambda i: (0, i)),
        ],
        out_shape=[
            jax.ShapeDtypeStruct((n_rows * slabs, LANES), F32),
            jax.ShapeDtypeStruct((TOP_K, n_rows), I32),
            jax.ShapeDtypeStruct((TOP_K, n_rows), F32),
        ],
        compiler_params=_params("parallel"),
        name="moe_router",
    )(xa, mod, mod, g[None, :], router_w.T, router_b[:, None])


def _dispatch_plan(top_idx, n_exp, tile, slabs):
    n_tok = top_idx.shape[1]
    n_assign = TOP_K * n_tok
    n_tiles = n_assign // tile
    flat_e = top_idx.T.reshape(-1)
    order = jnp.argsort(flat_e).astype(I32)
    tok = order // TOP_K
    row_token = (tok * slabs).reshape(n_tiles, 1, tile)
    row_slot = (((order % TOP_K) * n_tok + tok) * slabs).reshape(n_tiles, 1, tile)
    counts = jnp.sum((flat_e[:, None] == jnp.arange(n_exp, dtype=I32)[None, :]).astype(I32), axis=0)
    end = jnp.cumsum(counts)
    start = end - counts
    first_tile = start // tile
    visits = jnp.where(counts > 0, (end - 1) // tile - first_tile + 1, 0)
    visit_end = jnp.cumsum(visits)
    visit_start = visit_end - visits
    n_visits = visit_end[-1]
    max_visits = n_tiles + n_exp - 1
    v = jnp.arange(max_visits, dtype=I32)
    valid = v < n_visits
    e_of = jnp.minimum(jnp.sum((v[:, None] >= visit_end[None, :]).astype(I32), axis=1), n_exp - 1)
    t_of = first_tile[e_of] + v - visit_start[e_of]
    lo = jnp.maximum(start[e_of], t_of * tile) - t_of * tile
    hi = jnp.minimum(end[e_of], (t_of + 1) * tile) - t_of * tile
    last = jnp.maximum(n_visits - 1, 0)
    e_of = jnp.where(valid, e_of, e_of[last])
    t_of = jnp.where(valid, t_of, t_of[last])
    lo = jnp.where(valid, lo, 0)
    hi = jnp.where(valid, hi, 0)
    return row_token, row_slot, t_of.astype(I32), e_of, lo.astype(I32), hi.astype(I32), n_visits.reshape(1).astype(I32)


def _expert_body(tile_ref, exp_ref, lo_ref, hi_ref, nv_ref,
                 tokn_ref, slotp_ref, slot_ref, tok0_ref, h_hbm, wg_ref, wl_ref, wd_ref, bg_ref, bl_ref, bd_ref,
                 y_hbm, gbuf, xs, yacc, ybuf, gsem, ssem, *, tile, slabs):
    del exp_ref
    v = pl.program_id(0)
    n_visits = nv_ref[0]
    t = tile_ref[v]
    first_visit = jnp.logical_or(v == 0, tile_ref[jnp.maximum(v - 1, 0)] != t)
    last_visit = jnp.logical_or(v == n_visits - 1, tile_ref[jnp.minimum(v + 1, pl.num_programs(0) - 1)] != t)
    valid = v < n_visits
    straight = jnp.logical_and(jnp.logical_and(first_visit, last_visit), t >= 1)
    g_cur = lax.rem(t, 2)
    y_cur = lax.rem(t, 3)
    y_prev = lax.rem(t + 2, 3)
    rows = tile * slabs

    def start_gather(idx_ref, g):
        for r in range(tile):
            src = pl.multiple_of(idx_ref[0, 0, r], slabs)
            pltpu.make_async_copy(h_hbm.at[pl.ds(src, slabs), :], gbuf.at[g, pl.ds(r * slabs, slabs), :],
                                  gsem.at[g]).start()

    def wait_gather(g):
        pltpu.make_async_copy(h_hbm.at[pl.ds(0, rows), :], gbuf.at[g], gsem.at[g]).wait()

    def start_scatter(idx_ref, yb):
        for r in range(tile):
            dst = pl.multiple_of(idx_ref[0, 0, r], slabs)
            pltpu.make_async_copy(ybuf.at[yb, pl.ds(r * slabs, slabs), :], y_hbm.at[pl.ds(dst, slabs), :],
                                  ssem.at[yb]).start()

    def wait_scatter(yb):
        pltpu.make_async_copy(ybuf.at[yb], y_hbm.at[pl.ds(0, rows), :], ssem.at[yb]).wait()

    def begin_tile():
        start_gather(tokn_ref, 1 - g_cur)
        for s in range(slabs):
            xs[:, s * LANES:(s + 1) * LANES] = gbuf[g_cur, pl.ds(s, tile, stride=slabs), :].astype(BF16)

    def ffn():
        x = xs[...]
        gl = lax.dot_general(x, wg_ref[...], _NT, preferred_element_type=F32) + bg_ref[...]
        ln = lax.dot_general(x, wl_ref[...], _NT, preferred_element_type=F32) + bl_ref[...]
        x_glu = jnp.minimum(gl, SWIGLU_LIMIT)
        x_lin = jnp.clip(ln, -SWIGLU_LIMIT, SWIGLU_LIMIT)
        act = x_glu * jax.nn.sigmoid(SWIGLU_ALPHA * x_glu) * (x_lin + 1.0)
        return jnp.dot(act.astype(BF16), wd_ref[...], preferred_element_type=F32) + bd_ref[...]

    def stage(y_of_slab):
        for s in range(slabs):
            ybuf[y_cur, pl.ds(s, tile, stride=slabs), :] = y_of_slab(s)

    @pl.when(v == 0)
    def _():
        start_gather(tok0_ref, 0)

    @pl.when(jnp.logical_and(valid, first_visit))
    def _():
        wait_gather(g_cur)

        @pl.when(t >= 3)
        def _():
            wait_scatter(y_cur)

    @pl.when(jnp.logical_and(valid, straight))
    def _():
        start_scatter(slotp_ref, y_prev)
        begin_tile()
        y = ffn()
        stage(lambda s: y[:, s * LANES:(s + 1) * LANES])

    @pl.when(jnp.logical_and(valid, jnp.logical_not(straight)))
    def _():
        @pl.when(first_visit)
        def _():
            @pl.when(t >= 1)
            def _():
                start_scatter(slotp_ref, y_prev)

            begin_tile()

        y = ffn()

        @pl.when(first_visit)
        def _():
            yacc[...] = y

        @pl.when(jnp.logical_not(first_visit))
        def _():
            row = lax.broadcasted_iota(I32, (tile, 1), 0)
            mine = jnp.logical_and(row >= lo_ref[v], row < hi_ref[v])
            yacc[...] = jnp.where(mine, y, yacc[...])

        @pl.when(last_visit)
        def _():
            stage(lambda s: yacc[:, s * LANES:(s + 1) * LANES])

    @pl.when(v == n_visits - 1)
    def _():
        start_scatter(slot_ref, y_cur)
        wait_gather(1 - g_cur)
        wait_scatter(y_cur)

        @pl.when(t >= 1)
        def _():
            wait_scatter(y_prev)

        @pl.when(t >= 2)
        def _():
            wait_scatter(lax.rem(t + 1, 3))


def _experts(h_slab, plan, layer, wg, wl, wd, bg, bl, bd, n_tok):
    row_token, row_slot, t_of, e_of, lo, hi, n_visits = plan
    n_tiles, _, tile = row_token.shape
    _, n_exp, f, d = wg.shape
    slabs = d // LANES
    max_visits = t_of.shape[0]

    def expert_block(v, tl, ex, lo_, hi_, nv):
        return (layer, ex[v], 0, 0)

    body = functools.partial(_expert_body, tile=tile, slabs=slabs)
    smem_tile = functools.partial(pl.BlockSpec, (1, 1, tile), memory_space=pltpu.SMEM)
    grid_spec = pltpu.PrefetchScalarGridSpec(
        num_scalar_prefetch=5,
        grid=(max_visits,),
        in_specs=[
            smem_tile(lambda v, tl, ex, lo_, hi_, nv: (jnp.minimum(tl[v] + 1, n_tiles - 1), 0, 0)),
            smem_tile(lambda v, tl, ex, lo_, hi_, nv: (jnp.maximum(tl[v] - 1, 0), 0, 0)),
            smem_tile(lambda v, tl, ex, lo_, hi_, nv: (tl[v], 0, 0)),
            smem_tile(lambda v, tl, ex, lo_, hi_, nv: (0, 0, 0)),
            pl.BlockSpec(memory_space=pl.ANY),
            pl.BlockSpec((None, None, f, d), expert_block),
            pl.BlockSpec((None, None, f, d), expert_block),
            pl.BlockSpec((None, None, f, d), expert_block),
            pl.BlockSpec((None, None, 1, f), expert_block),
            pl.BlockSpec((None, None, 1, f), expert_block),
            pl.BlockSpec((None, None, 1, d), expert_block),
        ],
        out_specs=pl.BlockSpec(memory_space=pl.ANY),
        scratch_shapes=[
            pltpu.VMEM((2, tile * slabs, LANES), F32),
            pltpu.VMEM((tile, d), BF16),
            pltpu.VMEM((tile, d), F32),
            pltpu.VMEM((3, tile * slabs, LANES), F32),
            pltpu.SemaphoreType.DMA((2,)),
            pltpu.SemaphoreType.DMA((3,)),
        ],
    )
    return pl.pallas_call(
        body,
        grid_spec=grid_spec,
        out_shape=jax.ShapeDtypeStruct((TOP_K * n_tok * slabs, LANES), F32),
        compiler_params=_params("arbitrary"),
        name="moe_experts",
    )(t_of, e_of, lo, hi, n_visits, row_token, row_slot, row_slot, row_token, h_slab, wg, wl, wd, bg, bl, bd)


def _combine_body(x_ref, g2_ref, gate_ref, *refs, tm, slabs):
    y_refs, out_ref = refs[:TOP_K], refs[TOP_K]
    gates = gate_ref[...]
    g2 = g2_ref[0]
    for s in range(slabs):
        acc = None
        for k in range(TOP_K):
            part = gates[:, k:k + 1] * y_refs[k][pl.ds(s, tm, stride=slabs), :]
            acc = part if acc is None else acc + part
        cols = slice(s * LANES, (s + 1) * LANES)
        out_ref[:, cols] = x_ref[:, cols] + g2[:, cols] * acc


def _combine(xa, y_slab, gates_t, mod, r, layer, n_rows, n_lat_rows, n_seq, b):
    d = xa.shape[1]
    tm = ROW_TILE
    slabs = d // LANES
    n_tiles = n_rows // tm
    body = functools.partial(_combine_body, tm=tm, slabs=slabs)
    y_specs = [pl.BlockSpec((tm * slabs, LANES), functools.partial(lambda i, k: (k * n_tiles + i, 0), k=k))
               for k in range(TOP_K)]
    return pl.pallas_call(
        body,
        grid=(n_tiles,),
        in_specs=[
            pl.BlockSpec((tm, d), lambda i: (i, 0)),
            _mod_spec(layer, 5, r, d, n_lat_rows // tm, tm, n_seq, b),
            pl.BlockSpec((tm, TOP_K), lambda i: (i, 0)),
        ] + y_specs,
        out_specs=pl.BlockSpec((tm, d), lambda i: (i, 0)),
        out_shape=jax.ShapeDtypeStruct((n_rows, d), F32),
        compiler_params=_params("parallel"),
        name="moe_combine",
    )(xa, mod, gates_t, *([y_slab] * TOP_K))


def _wsplit_body(w_ref, wg_ref, wl_ref, ts):
    d, f2 = w_ref.shape
    half = LANES // 2
    for c in range(f2 // LANES):
        for j in range(d // LANES):
            blk = (c * (d // LANES) + j) * LANES
            ts[pl.ds(blk, LANES), :] = w_ref[j * LANES:(j + 1) * LANES, c * LANES:(c + 1) * LANES].T
            rows = slice(c * half, (c + 1) * half)
            cols = slice(j * LANES, (j + 1) * LANES)
            wg_ref[rows, cols] = ts[pl.ds(blk, half, stride=2), :].astype(BF16)
            wl_ref[rows, cols] = ts[pl.ds(blk + 1, half, stride=2), :].astype(BF16)


def _split_gate_up(w_gu):
    depth, n_exp, d, f2 = w_gu.shape
    f = f2 // 2
    out = jax.ShapeDtypeStruct((depth, n_exp, f, d), BF16)
    return pl.pallas_call(
        _wsplit_body,
        grid=(depth, n_exp),
        in_specs=[pl.BlockSpec((None, None, d, f2), lambda l, e: (l, e, 0, 0))],
        out_specs=[pl.BlockSpec((None, None, f, d), lambda l, e: (l, e, 0, 0))] * 2,
        out_shape=[out, out],
        scratch_shapes=[pltpu.VMEM((d * f2 // LANES, LANES), F32)],
        compiler_params=_params("parallel", "parallel"),
        name="moe_split_gate_up",
    )(w_gu)


def _moe(xa, mod, r, layer, g, router_w, router_b, expert_w, n_rows, n_lat_rows, n_seq, b):
    n_exp = router_w.shape[1]
    h_slab, top_idx, gates = _router(xa, mod, r, layer, g, router_w, router_b, n_rows, n_lat_rows, n_seq, b)
    plan = _dispatch_plan(top_idx, n_exp, MOE_TILE, xa.shape[1] // LANES)
    y_slab = _experts(h_slab, plan, layer, *expert_w, n_rows)
    return _combine(xa, y_slab, gates.T, mod, r, layer, n_rows, n_lat_rows, n_seq, b)


def _final_body(x_ref, g_ref, o_ref):
    x = x_ref[...]
    ms = jnp.mean(x * x, axis=-1, keepdims=True)
    o_ref[...] = x * lax.rsqrt(ms + EPS) * g_ref[...]


def _final_norm(xa, g, n_rows):
    d = xa.shape[1]
    tm = ROW_TILE
    return pl.pallas_call(
        _final_body,
        grid=(n_rows // tm,),
        in_specs=[pl.BlockSpec((tm, d), lambda i: (i, 0)), pl.BlockSpec((1, d), lambda i: (0, 0))],
        out_specs=pl.BlockSpec((tm, d), lambda i: (i, 0)),
        out_shape=jax.ShapeDtypeStruct((n_rows, d), F32),
        compiler_params=_params("parallel"),
        name="final_norm",
    )(xa, g[None, :])


def kernel(x, c, ctx, c_ctx, mod_w, mod_b, norm_g, attn_w_qkv, attn_w_o, attn_lambda, attn_subln_g, conv_w_in, conv_w, conv_w_out, router_w, router_b, moe_w_gu, moe_b_gu, moe_w_down, moe_b_down, final_g):
    b, n_seq, d = x.shape
    n_ctx = ctx.shape[1]
    depth = mod_w.shape[0]
    n_lat_rows = b * n_seq
    n_all_rows = n_lat_rows + b * n_ctx
    assert d % LANES == 0 and d == moe_w_down.shape[-1]
    assert n_seq % ROW_TILE == 0 and (b * n_ctx) % ROW_TILE == 0 and n_seq % GRID_W == 0
    assert ROW_TILE % n_ctx == 0 or n_ctx % ROW_TILE == 0
    assert n_lat_rows % n_ctx == 0 and n_ctx % BF16_SUBLANES == 0
    assert (TOP_K * n_lat_rows) % MOE_TILE == 0 and (TOP_K * n_all_rows) % MOE_TILE == 0

    xa = jnp.concatenate([x.reshape(n_lat_rows, d), ctx.reshape(b * n_ctx, d)], axis=0)
    mod, r = _modulation(c, c_ctx, mod_w, mod_b)
    wg, wl = _split_gate_up(moe_w_gu)
    expert_w = (wg, wl, moe_w_down.astype(BF16), moe_b_gu[:, :, None, 0::2], moe_b_gu[:, :, None, 1::2],
                moe_b_down[:, :, None, :])
    for i in range(depth):
        update_ctx = i < depth - 1
        attn_layer = i % N_MIXERS == 0
        j = i // N_MIXERS
        n_out = n_all_rows if update_ctx else n_lat_rows
        if attn_layer:
            lam_init = 0.8 - 0.6 * math.exp(-0.3 * i)
            qkv = _qkv_proj(xa, mod, r, i, norm_g[i, 0], attn_w_qkv[j], n_all_rows, n_lat_rows, n_seq, b)
            o = _attention(qkv, attn_lambda[j], attn_subln_g[j], lam_init, b, n_seq, n_ctx, n_lat_rows, update_ctx)
            xa = _out_proj(o, attn_w_o[j], xa, mod, r, i, n_lat_rows, n_seq, b)
        else:
            gb, u = _conv_proj(xa, mod, r, i, norm_g[i, 0], conv_w_in[j], n_out, n_lat_rows, n_seq, b)
            xa = _conv_out(u, gb, conv_w[j], conv_w_out[j], xa, mod, r, i, n_lat_rows, n_seq, n_ctx, b)
        xa = _moe(xa, mod, r, i, norm_g[i, 1], router_w[i], router_b[i], expert_w, n_out, n_lat_rows, n_seq, b)
    return _final_norm(xa, final_g, n_lat_rows).reshape(b, n_seq, d)
```

```python
import functools
import math

import jax
import jax.numpy as jnp
from jax import lax
from jax.experimental import pallas as pl
from jax.experimental.pallas import tpu as pltpu

F32 = jnp.float32
BF16 = jnp.bfloat16
I32 = jnp.int32

GRID_W = 64
HEAD_DIM = 64
ROPE_BASE = 10000.0
N_MIXERS = 2
TOP_K = 4
SWIGLU_LIMIT = 7.0
SWIGLU_ALPHA = 1.702
N_MOD = 6
EPS = 1e-5

LANES = 128
SUBLANES = 8
BF16_SUBLANES = 16
VMEM_LIMIT_BYTES = 48 * 1024 * 1024

ROW_TILE = 512
ATTN_Q_TILE = 512
MOE_TILE = 256

_NT = (((1,), (1,)), ((), ()))


def _round_up(a, b):
    return (a + b - 1) // b * b


def _params(*sem):
    return pltpu.CompilerParams(dimension_semantics=sem, vmem_limit_bytes=VMEM_LIMIT_BYTES)


def _norm_mod(x, g, shift, scale):
    ms = jnp.mean(x * x, axis=-1, keepdims=True)
    return (x * lax.rsqrt(ms + EPS) * g) * (1.0 + scale) + shift


def _mod_body(c_ref, w_ref, b_ref, o_ref):
    c = c_ref[...]
    s = c * jax.nn.sigmoid(c)
    o_ref[0] = jnp.dot(s, w_ref[0], preferred_element_type=F32, precision=lax.Precision.HIGHEST) + b_ref[0]


def _modulation(c, c_ctx, mod_w, mod_b):
    depth, d, nd = mod_w.shape
    b = c.shape[0]
    r = _round_up(b + 1, SUBLANES)
    cv = jnp.concatenate([c, c_ctx[None, :], jnp.zeros((r - b - 1, d), F32)], axis=0)
    tn = d
    out = pl.pallas_call(
        _mod_body,
        grid=(depth, nd // tn),
        in_specs=[
            pl.BlockSpec((r, d), lambda l, j: (0, 0)),
            pl.BlockSpec((1, d, tn), lambda l, j: (l, 0, j)),
            pl.BlockSpec((1, 1, tn), lambda l, j: (l, 0, j)),
        ],
        out_specs=pl.BlockSpec((1, r, tn), lambda l, j: (l, 0, j)),
        out_shape=jax.ShapeDtypeStruct((depth, r, nd), F32),
        compiler_params=_params("parallel", "parallel"),
        name="modulation",
    )(cv, mod_w, mod_b.reshape(depth, 1, nd))
    return out.reshape(depth * r, 1, nd), r


def _mod_spec(layer, chunk, r, d, n_lat_tiles, tm, n_seq, b):
    def index(i, *_):
        row = jnp.where(i < n_lat_tiles, (i * tm) // n_seq, b)
        return (layer * r + row, 0, chunk)

    return pl.BlockSpec((1, 1, d), index)


def _rope_tables(n, tm):
    rows = n // GRID_W
    row_pos = jnp.repeat(jnp.arange(rows, dtype=F32), GRID_W)
    col_pos = jnp.tile(jnp.arange(GRID_W, dtype=F32), rows)
    axis_dims = HEAD_DIM // 2
    inv_freq = ROPE_BASE ** (-jnp.arange(0, axis_dims, 2, dtype=F32) / axis_dims)
    ang = jnp.concatenate([row_pos[:, None] * inv_freq, col_pos[:, None] * inv_freq], axis=-1)
    cos, sin = jnp.cos(ang), jnp.sin(ang)
    reps = LANES // (HEAD_DIM // 2)
    cos_t = jnp.tile(cos, (1, reps))
    sign = jnp.where((jnp.arange(LANES) % HEAD_DIM) < HEAD_DIM // 2, -1.0, 1.0).astype(F32)
    sin_t = jnp.tile(sin, (1, reps)) * sign
    cos_t = jnp.concatenate([cos_t, jnp.ones((tm, LANES), F32)], axis=0)
    sin_t = jnp.concatenate([sin_t, jnp.zeros((tm, LANES), F32)], axis=0)
    return cos_t, sin_t


def _qkv_body(x_ref, sh_ref, sc_ref, g_ref, w_ref, cos_ref, sin_ref, o_ref, *, d):
    h = _norm_mod(x_ref[...], g_ref[...], sh_ref[0], sc_ref[0])
    acc = jnp.dot(h.astype(BF16), w_ref[...], preferred_element_type=F32)
    cos = cos_ref[...]
    sin = sin_ref[...]
    lane = lax.broadcasted_iota(I32, (1, LANES), 1)
    first_half = (lane % HEAD_DIM) < HEAD_DIM // 2
    q_scale = HEAD_DIM**-0.5 * math.log2(math.e)
    for gi in range(2 * d // LANES):
        xg = acc[:, gi * LANES:(gi + 1) * LANES]
        partner = jnp.where(first_half, pltpu.roll(xg, LANES - HEAD_DIM // 2, 1), pltpu.roll(xg, HEAD_DIM // 2, 1))
        r = xg * cos + partner * sin
        if gi < d // LANES:
            r = r * q_scale
        o_ref[:, gi * LANES:(gi + 1) * LANES] = r.astype(BF16)
    o_ref[:, 2 * d:] = acc[:, 2 * d:].astype(BF16)


def _qkv_proj(xa, mod, r, layer, g, w, n_rows, n_lat_rows, n_seq, b):
    d = xa.shape[1]
    tm = ROW_TILE
    n_lat_tiles = n_lat_rows // tm
    tiles_per_seq = n_seq // tm
    cos_t, sin_t = _rope_tables(n_seq, tm)

    def rope_index(i):
        return (jnp.where(i < n_lat_tiles, i % tiles_per_seq, tiles_per_seq), 0)

    return pl.pallas_call(
        functools.partial(_qkv_body, d=d),
        grid=(n_rows // tm,),
        in_specs=[
            pl.BlockSpec((tm, d), lambda i: (i, 0)),
            _mod_spec(layer, 0, r, d, n_lat_tiles, tm, n_seq, b),
            _mod_spec(layer, 1, r, d, n_lat_tiles, tm, n_seq, b),
            pl.BlockSpec((1, d), lambda i: (0, 0)),
            pl.BlockSpec((d, 3 * d), lambda i: (0, 0)),
            pl.BlockSpec((tm, LANES), rope_index),
            pl.BlockSpec((tm, LANES), rope_index),
        ],
        out_specs=pl.BlockSpec((tm, 3 * d), lambda i: (i, 0)),
        out_shape=jax.ShapeDtypeStruct((n_rows, 3 * d), BF16),
        compiler_params=_params("parallel"),
        name="qkv_proj",
    )(xa, mod, mod, g[None, :], w.astype(BF16), cos_t, sin_t)


def _diff_attention(lam_ref, sg_ref, q_ref, k_refs, va_refs, o_ref, lam_init):
    q = q_ref[...]
    lane = lax.broadcasted_iota(I32, (1, LANES), 1)
    lp = lam_ref[...]
    lam = (jnp.exp(jnp.sum(lp[0:1] * lp[1:2], axis=-1, keepdims=True))
           - jnp.exp(jnp.sum(lp[2:3] * lp[3:4], axis=-1, keepdims=True)) + lam_init)
    scores = []
    for comp in range(2):
        in_comp = (lane < HEAD_DIM) if comp == 0 else (lane >= HEAD_DIM)
        qc = jnp.where(in_comp, q, jnp.zeros_like(q))
        scores.append([lax.dot_general(qc, k[...], _NT, preferred_element_type=F32) for k in k_refs])
    outs = []
    for ss in scores:
        m = functools.reduce(jnp.maximum, [jnp.max(s, axis=-1, keepdims=True) for s in ss])
        acc = None
        for s, va in zip(ss, va_refs):
            part = jnp.dot(jnp.exp2(s - m).astype(BF16), va[...], preferred_element_type=F32)
            acc = part if acc is None else acc + part
        outs.append(acc[:, :LANES] / acc[:, LANES:])
    o = outs[0] - lam * outs[1]
    ms = jnp.mean(o * o, axis=-1, keepdims=True)
    o_ref[...] = (o * lax.rsqrt(ms + EPS) * sg_ref[...] * (1.0 - lam_init)).astype(BF16)


def _fill_value_ones(va, v_ref):
    va[:, :LANES] = v_ref[...]
    va[:, LANES:] = jnp.ones((va.shape[0], LANES), BF16)


def _attn_lat_body(lam_ref, sg_ref, q_ref, kc_ref, kl_ref, vc_ref, vl_ref, o_ref, vac, val, *, lam_init):
    @pl.when(pl.program_id(2) == 0)
    def _():
        _fill_value_ones(vac, vc_ref)
        _fill_value_ones(val, vl_ref)

    _diff_attention(lam_ref, sg_ref, q_ref, [kc_ref, kl_ref], [vac, val], o_ref, lam_init)


def _attn_ctx_body(lam_ref, sg_ref, q_ref, kc_ref, vc_ref, o_ref, vac, *, lam_init):
    _fill_value_ones(vac, vc_ref)
    _diff_attention(lam_ref, sg_ref, q_ref, [kc_ref], [vac], o_ref, lam_init)


def _attention(qkv, lam_p, subln_g, lam_init, b, n_seq, n_ctx, n_lat_rows, update_ctx):
    d = qkv.shape[1] // 3
    heads = d // LANES
    tq = ATTN_Q_TILE
    assert n_seq % tq == 0
    q_tiles = n_seq // tq
    ctx_blk0 = n_lat_rows // n_ctx
    small = [pl.BlockSpec((4, HEAD_DIM), lambda *_: (0, 0)), pl.BlockSpec((1, LANES), lambda *_: (0, 0))]
    o_lat = pl.pallas_call(
        functools.partial(_attn_lat_body, lam_init=lam_init),
        grid=(b, heads, q_tiles),
        in_specs=small + [
            pl.BlockSpec((tq, LANES), lambda bi, h, i: (bi * q_tiles + i, h)),
            pl.BlockSpec((n_ctx, LANES), lambda bi, h, i: (ctx_blk0 + bi, heads + h)),
            pl.BlockSpec((n_seq, LANES), lambda bi, h, i: (bi, heads + h)),
            pl.BlockSpec((n_ctx, LANES), lambda bi, h, i: (ctx_blk0 + bi, 2 * heads + h)),
            pl.BlockSpec((n_seq, LANES), lambda bi, h, i: (bi, 2 * heads + h)),
        ],
        out_specs=pl.BlockSpec((tq, LANES), lambda bi, h, i: (bi * q_tiles + i, h)),
        out_shape=jax.ShapeDtypeStruct((n_lat_rows, d), BF16),
        scratch_shapes=[pltpu.VMEM((n_ctx, 2 * LANES), BF16), pltpu.VMEM((n_seq, 2 * LANES), BF16)],
        compiler_params=_params("parallel", "parallel", "arbitrary"),
        name="attn_latent",
    )(lam_p, subln_g[None, :], qkv, qkv, qkv, qkv, qkv)
    if not update_ctx:
        return o_lat, None
    o_ctx = pl.pallas_call(
        functools.partial(_attn_ctx_body, lam_init=lam_init),
        grid=(b, heads),
        in_specs=small + [
            pl.BlockSpec((n_ctx, LANES), lambda bi, h: (ctx_blk0 + bi, h)),
            pl.BlockSpec((n_ctx, LANES), lambda bi, h: (ctx_blk0 + bi, heads + h)),
            pl.BlockSpec((n_ctx, LANES), lambda bi, h: (ctx_blk0 + bi, 2 * heads + h)),
        ],
        out_specs=pl.BlockSpec((n_ctx, LANES), lambda bi, h: (bi, h)),
        out_shape=jax.ShapeDtypeStruct((b * n_ctx, d), BF16),
        scratch_shapes=[pltpu.VMEM((n_ctx, 2 * LANES), BF16)],
        compiler_params=_params("parallel", "parallel"),
        name="attn_context",
    )(lam_p, subln_g[None, :], qkv, qkv, qkv)
    return o_lat, o_ctx


def _oproj_body(ol_ref, oc_ref, w_ref, x_ref, g1_ref, out_ref, *, n_lat_tiles):
    def finish(o_ref):
        y = jnp.dot(o_ref[...], w_ref[...], preferred_element_type=F32)
        out_ref[...] = x_ref[...] + g1_ref[0] * y

    @pl.when(pl.program_id(0) < n_lat_tiles)
    def _():
        finish(ol_ref)

    @pl.when(pl.program_id(0) >= n_lat_tiles)
    def _():
        finish(oc_ref)


def _out_proj(o_lat, o_ctx, w, xa, mod, r, layer, n_lat_rows, n_seq, b):
    d = o_lat.shape[1]
    tm = ROW_TILE
    n_lat_tiles = n_lat_rows // tm
    if o_ctx is None:
        o_ctx = o_lat
    n_rows = n_lat_rows + (0 if o_ctx is o_lat else o_ctx.shape[0])
    return pl.pallas_call(
        functools.partial(_oproj_body, n_lat_tiles=n_lat_tiles),
        grid=(n_rows // tm,),
        in_specs=[
            pl.BlockSpec((tm, d), lambda i: (jnp.minimum(i, n_lat_tiles - 1), 0)),
            pl.BlockSpec((tm, d), lambda i: (jnp.maximum(i - n_lat_tiles, 0), 0)),
            pl.BlockSpec((d, d), lambda i: (0, 0)),
            pl.BlockSpec((tm, d), lambda i: (i, 0)),
            _mod_spec(layer, 2, r, d, n_lat_tiles, tm, n_seq, b),
        ],
        out_specs=pl.BlockSpec((tm, d), lambda i: (i, 0)),
        out_shape=jax.ShapeDtypeStruct((n_rows, d), F32),
        compiler_params=_params("parallel"),
        name="attn_out_proj",
    )(o_lat, o_ctx, w.astype(BF16), xa, mod)


def _cproj_body(x_ref, sh_ref, sc_ref, g_ref, w_ref, gb_ref, u_ref, *, d):
    h = _norm_mod(x_ref[...], g_ref[...], sh_ref[0], sc_ref[0])
    acc = jnp.dot(h.astype(BF16), w_ref[...], preferred_element_type=F32)
    gb_ref[...] = acc[:, :d].astype(BF16)
    u_ref[...] = (acc[:, d:2 * d] * acc[:, 2 * d:]).astype(BF16)


def _conv_proj(xa, mod, r, layer, g, w, n_rows, n_lat_rows, n_seq, b):
    d = xa.shape[1]
    tm = ROW_TILE
    n_lat_tiles = n_lat_rows // tm
    return pl.pallas_call(
        functools.partial(_cproj_body, d=d),
        grid=(n_rows // tm,),
        in_specs=[
            pl.BlockSpec((tm, d), lambda i: (i, 0)),
            _mod_spec(layer, 0, r, d, n_lat_tiles, tm, n_seq, b),
            _mod_spec(layer, 1, r, d, n_lat_tiles, tm, n_seq, b),
            pl.BlockSpec((1, d), lambda i: (0, 0)),
            pl.BlockSpec((d, 3 * d), lambda i: (0, 0)),
        ],
        out_specs=[pl.BlockSpec((tm, d), lambda i: (i, 0)), pl.BlockSpec((tm, d), lambda i: (i, 0))],
        out_shape=[jax.ShapeDtypeStruct((n_rows, d), BF16), jax.ShapeDtypeStruct((n_rows, d), BF16)],
        compiler_params=_params("parallel"),
        name="conv_in_proj",
    )(xa, mod, mod, g[None, :], w.astype(BF16))


def _cout_body(u_ref, up_ref, un_ref, gb_ref, cw_ref, w_ref, x_ref, g1_ref, out_ref, *, tm, n_lat_rows, n_seq, n_ctx):
    i = pl.program_id(0)
    u = u_ref[...].astype(F32)
    prev_last = up_ref[BF16_SUBLANES - 1:BF16_SUBLANES, :].astype(F32)
    next_first = un_ref[0:1, :].astype(F32)
    row = lax.broadcasted_iota(I32, (tm, 1), 0)
    grow = i * tm + row
    is_lat = grow < n_lat_rows
    pos = jnp.where(is_lat, lax.rem(grow, n_seq), lax.rem(jnp.maximum(grow - n_lat_rows, 0), n_ctx))
    seq_len = jnp.where(is_lat, n_seq, n_ctx)
    um = jnp.where(row == 0, prev_last, pltpu.roll(u, 1, 0))
    um = jnp.where(pos == 0, 0.0, um)
    un = jnp.where(row == tm - 1, next_first, pltpu.roll(u, tm - 1, 0))
    un = jnp.where(pos == seq_len - 1, 0.0, un)
    cw = cw_ref[...]
    conv = um * cw[0:1] + u * cw[1:2] + un * cw[2:3]
    z = gb_ref[...].astype(F32) * conv
    y = jnp.dot(z.astype(BF16), w_ref[...], preferred_element_type=F32)
    out_ref[...] = x_ref[...] + g1_ref[0] * y


def _conv_out(u, gb, conv_w, w, xa, mod, r, layer, n_lat_rows, n_seq, n_ctx, b):
    n_rows, d = u.shape
    tm = ROW_TILE
    halo = BF16_SUBLANES
    n_halo_blocks = n_rows // halo
    body = functools.partial(_cout_body, tm=tm, n_lat_rows=n_lat_rows, n_seq=n_seq, n_ctx=n_ctx)
    return pl.pallas_call(
        body,
        grid=(n_rows // tm,),
        in_specs=[
            pl.BlockSpec((tm, d), lambda i: (i, 0)),
            pl.BlockSpec((halo, d), lambda i: (jnp.maximum(i * (tm // halo) - 1, 0), 0)),
            pl.BlockSpec((halo, d), lambda i: (jnp.minimum((i + 1) * (tm // halo), n_halo_blocks - 1), 0)),
            pl.BlockSpec((tm, d), lambda i: (i, 0)),
            pl.BlockSpec((3, d), lambda i: (0, 0)),
            pl.BlockSpec((d, d), lambda i: (0, 0)),
            pl.BlockSpec((tm, d), lambda i: (i, 0)),
            _mod_spec(layer, 2, r, d, n_lat_rows // tm, tm, n_seq, b),
        ],
        out_specs=pl.BlockSpec((tm, d), lambda i: (i, 0)),
        out_shape=jax.ShapeDtypeStruct((n_rows, d), F32),
        compiler_params=_params("parallel"),
        name="conv_out_proj",
    )(u, u, u, gb, conv_w, w.astype(BF16), xa, mod)


def _router_body(x_ref, sh_ref, sc_ref, g_ref, rwt_ref, rb_ref, earlier_ref, h_ref, idx_ref, gate_ref, rank_ref,
                 count_ref, *, tm, d, n_exp):
    @pl.when(pl.program_id(0) == 0)
    def _():
        count_ref[...] = jnp.zeros_like(count_ref)

    h = _norm_mod(x_ref[...], g_ref[...], sh_ref[0], sc_ref[0])
    slabs = d // LANES
    for s in range(slabs):
        h_ref[pl.ds(s, tm, stride=slabs), :] = h[:, s * LANES:(s + 1) * LANES]
    logits = lax.dot_general(rwt_ref[...], h, _NT, preferred_element_type=F32,
                             precision=lax.Precision.HIGHEST) + rb_ref[...]
    eidx = lax.broadcasted_iota(I32, (n_exp, tm), 0)
    tops, sels = [], []
    for _ in range(TOP_K):
        m = jnp.max(logits, axis=0, keepdims=True)
        sel = jnp.min(jnp.where(logits == m, eidx, n_exp), axis=0, keepdims=True)
        tops.append(m)
        sels.append(sel)
        logits = jnp.where(eidx == sel, -jnp.inf, logits)
    es = [jnp.exp(m - tops[0]) for m in tops]
    den = functools.reduce(jnp.add, es)
    idx_ref[...] = jnp.concatenate(sels, axis=0)
    gate_ref[...] = jnp.concatenate([e / den for e in es], axis=0)
    onehots = [(eidx == sel).astype(F32) for sel in sels]
    prefix = jnp.dot(jnp.concatenate(onehots, axis=0).astype(BF16), earlier_ref[...], preferred_element_type=F32)
    offset = count_ref[...]
    ranks = []
    for k in range(TOP_K):
        within = prefix[k * n_exp:(k + 1) * n_exp] + offset
        ranks.append(jnp.sum(onehots[k] * within, axis=0, keepdims=True))
        offset = offset + jnp.sum(onehots[k], axis=1, keepdims=True)
    rank_ref[...] = jnp.concatenate(ranks, axis=0).astype(I32)
    count_ref[...] = offset


def _router(xa, mod, r, layer, g, router_w, router_b, n_rows, n_lat_rows, n_seq, b):
    d = xa.shape[1]
    n_exp = router_w.shape[1]
    tm = ROW_TILE
    slabs = d // LANES
    n_lat_tiles = n_lat_rows // tm
    body = functools.partial(_router_body, tm=tm, d=d, n_exp=n_exp)
    earlier = (jnp.arange(tm)[:, None] < jnp.arange(tm)[None, :]).astype(BF16)
    return pl.pallas_call(
        body,
        grid=(n_rows // tm,),
        in_specs=[
            pl.BlockSpec((tm, d), lambda i: (i, 0)),
            _mod_spec(layer, 3, r, d, n_lat_tiles, tm, n_seq, b),
            _mod_spec(layer, 4, r, d, n_lat_tiles, tm, n_seq, b),
            pl.BlockSpec((1, d), lambda i: (0, 0)),
            pl.BlockSpec((n_exp, d), lambda i: (0, 0)),
            pl.BlockSpec((n_exp, 1), lambda i: (0, 0)),
            pl.BlockSpec((tm, tm), lambda i: (0, 0)),
        ],
        out_specs=[
            pl.BlockSpec((tm * slabs, LANES), lambda i: (i, 0)),
            pl.BlockSpec((TOP_K, tm), lambda i: (0, i)),
            pl.BlockSpec((TOP_K, tm), lambda i: (0, i)),
            pl.BlockSpec((TOP_K, tm), lambda i: (0, i)),
            pl.BlockSpec((n_exp, 1), lambda i: (0, 0)),
        ],
        out_shape=[
            jax.ShapeDtypeStruct((n_rows * slabs, LANES), F32),
            jax.ShapeDtypeStruct((TOP_K, n_rows), I32),
            jax.ShapeDtypeStruct((TOP_K, n_rows), F32),
            jax.ShapeDtypeStruct((TOP_K, n_rows), I32),
            jax.ShapeDtypeStruct((n_exp, 1), F32),
        ],
        compiler_params=_params("arbitrary"),
        name="moe_router",
    )(xa, mod, mod, g[None, :], router_w.T, router_b[:, None], earlier)


def _dispatch_plan(top_idx, rank, counts, tile, slabs):
    n_tok = top_idx.shape[1]
    n_exp = counts.shape[0]
    n_assign = TOP_K * n_tok
    n_tiles = n_assign // tile
    counts = counts[:, 0].astype(I32)
    end = jnp.cumsum(counts)
    start = end - counts
    experts = jnp.arange(n_exp, dtype=I32)
    pos = rank + jnp.sum(jnp.where(top_idx[:, :, None] == experts, start, 0), axis=-1)
    slot_of_row = jnp.zeros((n_assign,), I32).at[pos.reshape(-1)].set(
        jnp.arange(n_assign, dtype=I32), unique_indices=True, indices_are_sorted=False)
    row_token = ((slot_of_row % n_tok) * slabs).reshape(n_tiles, 1, tile)
    row_slot = (slot_of_row * slabs).reshape(n_tiles, 1, tile)
    first_tile = start // tile
    visits = jnp.where(counts > 0, (end - 1) // tile - first_tile + 1, 0)
    visit_end = jnp.cumsum(visits)
    visit_start = visit_end - visits
    n_visits = visit_end[-1]
    max_visits = n_tiles + n_exp - 1
    v = jnp.arange(max_visits, dtype=I32)
    valid = v < n_visits
    e_of = jnp.minimum(jnp.sum((v[:, None] >= visit_end[None, :]).astype(I32), axis=1), n_exp - 1)
    t_of = first_tile[e_of] + v - visit_start[e_of]
    lo = jnp.maximum(start[e_of], t_of * tile) - t_of * tile
    hi = jnp.minimum(end[e_of], (t_of + 1) * tile) - t_of * tile
    last = jnp.maximum(n_visits - 1, 0)
    e_of = jnp.where(valid, e_of, e_of[last])
    t_of = jnp.where(valid, t_of, t_of[last])
    lo = jnp.where(valid, lo, 0)
    hi = jnp.where(valid, hi, 0)
    return row_token, row_slot, t_of.astype(I32), e_of, lo.astype(I32), hi.astype(I32), n_visits.reshape(1).astype(I32)


def _expert_body(tile_ref, exp_ref, lo_ref, hi_ref, nv_ref,
                 tokn_ref, slotp_ref, slot_ref, tok0_ref, h_hbm, wg_ref, wl_ref, wd_ref, bg_ref, bl_ref, bd_ref,
                 y_hbm, gbuf, xs, yacc, ybuf, gsem, ssem, *, tile, slabs):
    del exp_ref
    v = pl.program_id(0)
    n_visits = nv_ref[0]
    t = tile_ref[v]
    first_visit = jnp.logical_or(v == 0, tile_ref[jnp.maximum(v - 1, 0)] != t)
    last_visit = jnp.logical_or(v == n_visits - 1, tile_ref[jnp.minimum(v + 1, pl.num_programs(0) - 1)] != t)
    valid = v < n_visits
    straight = jnp.logical_and(jnp.logical_and(first_visit, last_visit), t >= 1)
    g_cur = lax.rem(t, 2)
    y_cur = lax.rem(t, 3)
    y_prev = lax.rem(t + 2, 3)
    rows = tile * slabs

    def start_gather(idx_ref, g):
        for r in range(tile):
            src = pl.multiple_of(idx_ref[0, 0, r], slabs)
            pltpu.make_async_copy(h_hbm.at[pl.ds(src, slabs), :], gbuf.at[g, pl.ds(r * slabs, slabs), :],
                                  gsem.at[g]).start()

    def wait_gather(g):
        pltpu.make_async_copy(h_hbm.at[pl.ds(0, rows), :], gbuf.at[g], gsem.at[g]).wait()

    def start_scatter(idx_ref, yb):
        for r in range(tile):
            dst = pl.multiple_of(idx_ref[0, 0, r], slabs)
            pltpu.make_async_copy(ybuf.at[yb, pl.ds(r * slabs, slabs), :], y_hbm.at[pl.ds(dst, slabs), :],
                                  ssem.at[yb]).start()

    def wait_scatter(yb):
        pltpu.make_async_copy(ybuf.at[yb], y_hbm.at[pl.ds(0, rows), :], ssem.at[yb]).wait()

    def begin_tile():
        start_gather(tokn_ref, 1 - g_cur)
        for s in range(slabs):
            xs[:, s * LANES:(s + 1) * LANES] = gbuf[g_cur, pl.ds(s, tile, stride=slabs), :].astype(BF16)

    def ffn():
        x = xs[...]
        gl = lax.dot_general(x, wg_ref[...], _NT, preferred_element_type=F32) + bg_ref[...]
        ln = lax.dot_general(x, wl_ref[...], _NT, preferred_element_type=F32) + bl_ref[...]
        x_glu = jnp.minimum(gl, SWIGLU_LIMIT)
        x_lin = jnp.clip(ln, -SWIGLU_LIMIT, SWIGLU_LIMIT)
        act = x_glu * jax.nn.sigmoid(SWIGLU_ALPHA * x_glu) * (x_lin + 1.0)
        return jnp.dot(act.astype(BF16), wd_ref[...], preferred_element_type=F32) + bd_ref[...]

    def stage(y_of_slab):
        for s in range(slabs):
            ybuf[y_cur, pl.ds(s, tile, stride=slabs), :] = y_of_slab(s)

    @pl.when(v == 0)
    def _():
        start_gather(tok0_ref, 0)

    @pl.when(jnp.logical_and(valid, first_visit))
    def _():
        wait_gather(g_cur)

        @pl.when(t >= 3)
        def _():
            wait_scatter(y_cur)

    @pl.when(jnp.logical_and(valid, straight))
    def _():
        start_scatter(slotp_ref, y_prev)
        begin_tile()
        y = ffn()
        stage(lambda s: y[:, s * LANES:(s + 1) * LANES])

    @pl.when(jnp.logical_and(valid, jnp.logical_not(straight)))
    def _():
        @pl.when(first_visit)
        def _():
            @pl.when(t >= 1)
            def _():
                start_scatter(slotp_ref, y_prev)

            begin_tile()

        y = ffn()

        @pl.when(first_visit)
        def _():
            yacc[...] = y

        @pl.when(jnp.logical_not(first_visit))
        def _():
            row = lax.broadcasted_iota(I32, (tile, 1), 0)
            mine = jnp.logical_and(row >= lo_ref[v], row < hi_ref[v])
            yacc[...] = jnp.where(mine, y, yacc[...])

        @pl.when(last_visit)
        def _():
            stage(lambda s: yacc[:, s * LANES:(s + 1) * LANES])

    @pl.when(v == n_visits - 1)
    def _():
        start_scatter(slot_ref, y_cur)
        wait_gather(1 - g_cur)
        wait_scatter(y_cur)

        @pl.when(t >= 1)
        def _():
            wait_scatter(y_prev)

        @pl.when(t >= 2)
        def _():
            wait_scatter(lax.rem(t + 1, 3))


def _experts(h_slab, plan, layer, wg, wl, wd, bg, bl, bd, n_tok):
    row_token, row_slot, t_of, e_of, lo, hi, n_visits = plan
    n_tiles, _, tile = row_token.shape
    _, n_exp, f, d = wg.shape
    slabs = d // LANES
    max_visits = t_of.shape[0]

    def expert_block(v, tl, ex, lo_, hi_, nv):
        return (layer, ex[v], 0, 0)

    body = functools.partial(_expert_body, tile=tile, slabs=slabs)
    smem_tile = functools.partial(pl.BlockSpec, (1, 1, tile), memory_space=pltpu.SMEM)
    grid_spec = pltpu.PrefetchScalarGridSpec(
        num_scalar_prefetch=5,
        grid=(max_visits,),
        in_specs=[
            smem_tile(lambda v, tl, ex, lo_, hi_, nv: (jnp.minimum(tl[v] + 1, n_tiles - 1), 0, 0)),
            smem_tile(lambda v, tl, ex, lo_, hi_, nv: (jnp.maximum(tl[v] - 1, 0), 0, 0)),
            smem_tile(lambda v, tl, ex, lo_, hi_, nv: (tl[v], 0, 0)),
            smem_tile(lambda v, tl, ex, lo_, hi_, nv: (0, 0, 0)),
            pl.BlockSpec(memory_space=pl.ANY),
            pl.BlockSpec((None, None, f, d), expert_block),
            pl.BlockSpec((None, None, f, d), expert_block),
            pl.BlockSpec((None, None, f, d), expert_block),
            pl.BlockSpec((None, None, 1, f), expert_block),
            pl.BlockSpec((None, None, 1, f), expert_block),
            pl.BlockSpec((None, None, 1, d), expert_block),
        ],
        out_specs=pl.BlockSpec(memory_space=pl.ANY),
        scratch_shapes=[
            pltpu.VMEM((2, tile * slabs, LANES), F32),
            pltpu.VMEM((tile, d), BF16),
            pltpu.VMEM((tile, d), F32),
            pltpu.VMEM((3, tile * slabs, LANES), F32),
            pltpu.SemaphoreType.DMA((2,)),
            pltpu.SemaphoreType.DMA((3,)),
        ],
    )
    return pl.pallas_call(
        body,
        grid_spec=grid_spec,
        out_shape=jax.ShapeDtypeStruct((TOP_K * n_tok * slabs, LANES), F32),
        compiler_params=_params("arbitrary"),
        name="moe_experts",
    )(t_of, e_of, lo, hi, n_visits, row_token, row_slot, row_slot, row_token, h_slab, wg, wl, wd, bg, bl, bd)


def _combine_body(x_ref, g2_ref, gate_ref, *refs, tm, slabs):
    y_refs, out_ref = refs[:TOP_K], refs[TOP_K]
    gates = gate_ref[...]
    g2 = g2_ref[0]
    for s in range(slabs):
        acc = None
        for k in range(TOP_K):
            part = gates[:, k:k + 1] * y_refs[k][pl.ds(s, tm, stride=slabs), :]
            acc = part if acc is None else acc + part
        cols = slice(s * LANES, (s + 1) * LANES)
        out_ref[:, cols] = x_ref[:, cols] + g2[:, cols] * acc


def _combine(xa, y_slab, gates_t, mod, r, layer, n_rows, n_lat_rows, n_seq, b):
    d = xa.shape[1]
    tm = ROW_TILE
    slabs = d // LANES
    n_tiles = n_rows // tm
    body = functools.partial(_combine_body, tm=tm, slabs=slabs)
    y_specs = [pl.BlockSpec((tm * slabs, LANES), functools.partial(lambda i, k: (k * n_tiles + i, 0), k=k))
               for k in range(TOP_K)]
    return pl.pallas_call(
        body,
        grid=(n_tiles,),
        in_specs=[
            pl.BlockSpec((tm, d), lambda i: (i, 0)),
            _mod_spec(layer, 5, r, d, n_lat_rows // tm, tm, n_seq, b),
            pl.BlockSpec((tm, TOP_K), lambda i: (i, 0)),
        ] + y_specs,
        out_specs=pl.BlockSpec((tm, d), lambda i: (i, 0)),
        out_shape=jax.ShapeDtypeStruct((n_rows, d), F32),
        compiler_params=_params("parallel"),
        name="moe_combine",
    )(xa, mod, gates_t, *([y_slab] * TOP_K))


def _wsplit_body(w_ref, wg_ref, wl_ref, ts):
    d, f2 = w_ref.shape
    half = LANES // 2
    for c in range(f2 // LANES):
        for j in range(d // LANES):
            blk = (c * (d // LANES) + j) * LANES
            ts[pl.ds(blk, LANES), :] = w_ref[j * LANES:(j + 1) * LANES, c * LANES:(c + 1) * LANES].T
            rows = slice(c * half, (c + 1) * half)
            cols = slice(j * LANES, (j + 1) * LANES)
            wg_ref[rows, cols] = ts[pl.ds(blk, half, stride=2), :].astype(BF16)
            wl_ref[rows, cols] = ts[pl.ds(blk + 1, half, stride=2), :].astype(BF16)


def _split_gate_up(w_gu):
    depth, n_exp, d, f2 = w_gu.shape
    f = f2 // 2
    out = jax.ShapeDtypeStruct((depth, n_exp, f, d), BF16)
    return pl.pallas_call(
        _wsplit_body,
        grid=(depth, n_exp),
        in_specs=[pl.BlockSpec((None, None, d, f2), lambda l, e: (l, e, 0, 0))],
        out_specs=[pl.BlockSpec((None, None, f, d), lambda l, e: (l, e, 0, 0))] * 2,
        out_shape=[out, out],
        scratch_shapes=[pltpu.VMEM((d * f2 // LANES, LANES), F32)],
        compiler_params=_params("parallel", "parallel"),
        name="moe_split_gate_up",
    )(w_gu)


def _moe(xa, mod, r, layer, g, router_w, router_b, expert_w, n_rows, n_lat_rows, n_seq, b):
    h_slab, top_idx, gates, rank, counts = _router(xa, mod, r, layer, g, router_w, router_b, n_rows, n_lat_rows,
                                                   n_seq, b)
    plan = _dispatch_plan(top_idx, rank, counts, MOE_TILE, xa.shape[1] // LANES)
    y_slab = _experts(h_slab, plan, layer, *expert_w, n_rows)
    return _combine(xa, y_slab, gates.T, mod, r, layer, n_rows, n_lat_rows, n_seq, b)


def _final_body(x_ref, g_ref, o_ref):
    x = x_ref[...]
    ms = jnp.mean(x * x, axis=-1, keepdims=True)
    o_ref[...] = x * lax.rsqrt(ms + EPS) * g_ref[...]


def _final_norm(xa, g, n_rows):
    d = xa.shape[1]
    tm = ROW_TILE
    return pl.pallas_call(
        _final_body,
        grid=(n_rows // tm,),
        in_specs=[pl.BlockSpec((tm, d), lambda i: (i, 0)), pl.BlockSpec((1, d), lambda i: (0, 0))],
        out_specs=pl.BlockSpec((tm, d), lambda i: (i, 0)),
        out_shape=jax.ShapeDtypeStruct((n_rows, d), F32),
        compiler_params=_params("parallel"),
        name="final_norm",
    )(xa, g[None, :])


def kernel(x, c, ctx, c_ctx, mod_w, mod_b, norm_g, attn_w_qkv, attn_w_o, attn_lambda, attn_subln_g, conv_w_in, conv_w, conv_w_out, router_w, router_b, moe_w_gu, moe_b_gu, moe_w_down, moe_b_down, final_g):
    b, n_seq, d = x.shape
    n_ctx = ctx.shape[1]
    depth = mod_w.shape[0]
    n_lat_rows = b * n_seq
    n_all_rows = n_lat_rows + b * n_ctx
    assert d % LANES == 0 and d == moe_w_down.shape[-1]
    assert n_seq % ROW_TILE == 0 and (b * n_ctx) % ROW_TILE == 0 and n_seq % GRID_W == 0
    assert ROW_TILE % n_ctx == 0 or n_ctx % ROW_TILE == 0
    assert n_lat_rows % n_ctx == 0 and n_ctx % BF16_SUBLANES == 0
    assert (TOP_K * n_lat_rows) % MOE_TILE == 0 and (TOP_K * n_all_rows) % MOE_TILE == 0

    xa = jnp.concatenate([x.reshape(n_lat_rows, d), ctx.reshape(b * n_ctx, d)], axis=0)
    mod, r = _modulation(c, c_ctx, mod_w, mod_b)
    wg, wl = _split_gate_up(moe_w_gu)
    expert_w = (wg, wl, moe_w_down.astype(BF16), moe_b_gu[:, :, None, 0::2], moe_b_gu[:, :, None, 1::2],
                moe_b_down[:, :, None, :])
    for i in range(depth):
        update_ctx = i < depth - 1
        attn_layer = i % N_MIXERS == 0
        j = i // N_MIXERS
        n_out = n_all_rows if update_ctx else n_lat_rows
        if attn_layer:
            lam_init = 0.8 - 0.6 * math.exp(-0.3 * i)
            qkv = _qkv_proj(xa, mod, r, i, norm_g[i, 0], attn_w_qkv[j], n_all_rows, n_lat_rows, n_seq, b)
            o_lat, o_ctx = _attention(qkv, attn_lambda[j], attn_subln_g[j], lam_init, b, n_seq, n_ctx, n_lat_rows,
                                      update_ctx)
            xa = _out_proj(o_lat, o_ctx, attn_w_o[j], xa, mod, r, i, n_lat_rows, n_seq, b)
        else:
            gb, u = _conv_proj(xa, mod, r, i, norm_g[i, 0], conv_w_in[j], n_out, n_lat_rows, n_seq, b)
            xa = _conv_out(u, gb, conv_w[j], conv_w_out[j], xa, mod, r, i, n_lat_rows, n_seq, n_ctx, b)
        xa = _moe(xa, mod, r, i, norm_g[i, 1], router_w[i], router_b[i], expert_w, n_out, n_lat_rows, n_seq, b)
    return _final_norm(xa, final_g, n_lat_rows).reshape(b, n_seq, d)
```

```python
import functools
import math

import jax
import jax.numpy as jnp
from jax import lax
from jax.experimental import pallas as pl
from jax.experimental.pallas import tpu as pltpu

F32 = jnp.float32
BF16 = jnp.bfloat16
I32 = jnp.int32

GRID_W = 64
HEAD_DIM = 64
ROPE_BASE = 10000.0
N_MIXERS = 2
TOP_K = 4
SWIGLU_LIMIT = 7.0
SWIGLU_ALPHA = 1.702
N_MOD = 6
EPS = 1e-5

LANES = 128
SUBLANES = 8
BF16_SUBLANES = 16
VMEM_LIMIT_BYTES = 48 * 1024 * 1024

ROW_TILE = 512
ATTN_Q_TILE = 512
MOE_TILE = 256

_NT = (((1,), (1,)), ((), ()))


def _round_up(a, b):
    return (a + b - 1) // b * b


def _params(*sem):
    return pltpu.CompilerParams(dimension_semantics=sem, vmem_limit_bytes=VMEM_LIMIT_BYTES)


def _norm_mod(x, g, shift, scale):
    ms = jnp.mean(x * x, axis=-1, keepdims=True)
    return (x * lax.rsqrt(ms + EPS) * g) * (1.0 + scale) + shift


def _mod_body(c_ref, w_ref, b_ref, o_ref):
    c = c_ref[...]
    s = c * jax.nn.sigmoid(c)
    o_ref[0] = jnp.dot(s, w_ref[0], preferred_element_type=F32, precision=lax.Precision.HIGHEST) + b_ref[0]


def _modulation(c, c_ctx, mod_w, mod_b):
    depth, d, nd = mod_w.shape
    b = c.shape[0]
    r = _round_up(b + 1, SUBLANES)
    cv = jnp.concatenate([c, c_ctx[None, :], jnp.zeros((r - b - 1, d), F32)], axis=0)
    tn = d
    out = pl.pallas_call(
        _mod_body,
        grid=(depth, nd // tn),
        in_specs=[
            pl.BlockSpec((r, d), lambda l, j: (0, 0)),
            pl.BlockSpec((1, d, tn), lambda l, j: (l, 0, j)),
            pl.BlockSpec((1, 1, tn), lambda l, j: (l, 0, j)),
        ],
        out_specs=pl.BlockSpec((1, r, tn), lambda l, j: (l, 0, j)),
        out_shape=jax.ShapeDtypeStruct((depth, r, nd), F32),
        compiler_params=_params("parallel", "parallel"),
        name="modulation",
    )(cv, mod_w, mod_b.reshape(depth, 1, nd))
    return out.reshape(depth * r, 1, nd), r


def _mod_spec(layer, chunk, r, d, n_lat_tiles, tm, n_seq, b):
    def index(i, *_):
        row = jnp.where(i < n_lat_tiles, (i * tm) // n_seq, b)
        return (layer * r + row, 0, chunk)

    return pl.BlockSpec((1, 1, d), index)


def _rope_tables(n, tm):
    rows = n // GRID_W
    row_pos = jnp.repeat(jnp.arange(rows, dtype=F32), GRID_W)
    col_pos = jnp.tile(jnp.arange(GRID_W, dtype=F32), rows)
    axis_dims = HEAD_DIM // 2
    inv_freq = ROPE_BASE ** (-jnp.arange(0, axis_dims, 2, dtype=F32) / axis_dims)
    ang = jnp.concatenate([row_pos[:, None] * inv_freq, col_pos[:, None] * inv_freq], axis=-1)
    cos, sin = jnp.cos(ang), jnp.sin(ang)
    reps = LANES // (HEAD_DIM // 2)
    cos_t = jnp.tile(cos, (1, reps))
    sign = jnp.where((jnp.arange(LANES) % HEAD_DIM) < HEAD_DIM // 2, -1.0, 1.0).astype(F32)
    sin_t = jnp.tile(sin, (1, reps)) * sign
    cos_t = jnp.concatenate([cos_t, jnp.ones((tm, LANES), F32)], axis=0)
    sin_t = jnp.concatenate([sin_t, jnp.zeros((tm, LANES), F32)], axis=0)
    return cos_t, sin_t


def _qkv_body(x_ref, sh_ref, sc_ref, g_ref, w_ref, cos_ref, sin_ref, o_ref, *, d):
    h = _norm_mod(x_ref[...], g_ref[...], sh_ref[0], sc_ref[0])
    acc = jnp.dot(h.astype(BF16), w_ref[...], preferred_element_type=F32)
    cos = cos_ref[...]
    sin = sin_ref[...]
    lane = lax.broadcasted_iota(I32, (1, LANES), 1)
    first_half = (lane % HEAD_DIM) < HEAD_DIM // 2
    q_scale = HEAD_DIM**-0.5 * math.log2(math.e)
    for gi in range(2 * d // LANES):
        xg = acc[:, gi * LANES:(gi + 1) * LANES]
        partner = jnp.where(first_half, pltpu.roll(xg, LANES - HEAD_DIM // 2, 1), pltpu.roll(xg, HEAD_DIM // 2, 1))
        r = xg * cos + partner * sin
        if gi < d // LANES:
            r = r * q_scale
        o_ref[:, gi * LANES:(gi + 1) * LANES] = r.astype(BF16)
    o_ref[:, 2 * d:] = acc[:, 2 * d:].astype(BF16)


def _qkv_proj(xa, mod, r, layer, g, w, n_rows, n_lat_rows, n_seq, b):
    d = xa.shape[1]
    tm = ROW_TILE
    n_lat_tiles = n_lat_rows // tm
    tiles_per_seq = n_seq // tm
    cos_t, sin_t = _rope_tables(n_seq, tm)

    def rope_index(i):
        return (jnp.where(i < n_lat_tiles, i % tiles_per_seq, tiles_per_seq), 0)

    return pl.pallas_call(
        functools.partial(_qkv_body, d=d),
        grid=(n_rows // tm,),
        in_specs=[
            pl.BlockSpec((tm, d), lambda i: (i, 0)),
            _mod_spec(layer, 0, r, d, n_lat_tiles, tm, n_seq, b),
            _mod_spec(layer, 1, r, d, n_lat_tiles, tm, n_seq, b),
            pl.BlockSpec((1, d), lambda i: (0, 0)),
            pl.BlockSpec((d, 3 * d), lambda i: (0, 0)),
            pl.BlockSpec((tm, LANES), rope_index),
            pl.BlockSpec((tm, LANES), rope_index),
        ],
        out_specs=pl.BlockSpec((tm, 3 * d), lambda i: (i, 0)),
        out_shape=jax.ShapeDtypeStruct((n_rows, 3 * d), BF16),
        compiler_params=_params("parallel"),
        name="qkv_proj",
    )(xa, mod, mod, g[None, :], w.astype(BF16), cos_t, sin_t)


def _diff_attention(lam_ref, sg_ref, q_ref, k_refs, va_refs, o_ref, lam_init):
    q = q_ref[...]
    lane = lax.broadcasted_iota(I32, (1, LANES), 1)
    lp = lam_ref[...]
    lam = (jnp.exp(jnp.sum(lp[0:1] * lp[1:2], axis=-1, keepdims=True))
           - jnp.exp(jnp.sum(lp[2:3] * lp[3:4], axis=-1, keepdims=True)) + lam_init)
    scores = []
    for comp in range(2):
        in_comp = (lane < HEAD_DIM) if comp == 0 else (lane >= HEAD_DIM)
        qc = jnp.where(in_comp, q, jnp.zeros_like(q))
        scores.append([lax.dot_general(qc, k[...], _NT, preferred_element_type=F32) for k in k_refs])
    outs = []
    for ss in scores:
        m = functools.reduce(jnp.maximum, [jnp.max(s, axis=-1, keepdims=True) for s in ss])
        acc = None
        for s, va in zip(ss, va_refs):
            part = jnp.dot(jnp.exp2(s - m).astype(BF16), va[...], preferred_element_type=F32)
            acc = part if acc is None else acc + part
        outs.append(acc[:, :LANES] / acc[:, LANES:])
    o = outs[0] - lam * outs[1]
    ms = jnp.mean(o * o, axis=-1, keepdims=True)
    o_ref[...] = (o * lax.rsqrt(ms + EPS) * sg_ref[...] * (1.0 - lam_init)).astype(BF16)


def _fill_value_ones(va, v_ref):
    va[:, :LANES] = v_ref[...]
    va[:, LANES:] = jnp.ones((va.shape[0], LANES), BF16)


def _attn_lat_body(lam_ref, sg_ref, q_ref, kc_ref, kl_ref, vc_ref, vl_ref, o_ref, vac, val, *, lam_init):
    @pl.when(pl.program_id(2) == 0)
    def _():
        _fill_value_ones(vac, vc_ref)
        _fill_value_ones(val, vl_ref)

    _diff_attention(lam_ref, sg_ref, q_ref, [kc_ref, kl_ref], [vac, val], o_ref, lam_init)


def _attn_ctx_body(lam_ref, sg_ref, q_ref, kc_ref, vc_ref, o_ref, vac, *, lam_init):
    _fill_value_ones(vac, vc_ref)
    _diff_attention(lam_ref, sg_ref, q_ref, [kc_ref], [vac], o_ref, lam_init)


def _attention(qkv, lam_p, subln_g, lam_init, b, n_seq, n_ctx, n_lat_rows, update_ctx):
    d = qkv.shape[1] // 3
    heads = d // LANES
    tq = ATTN_Q_TILE
    assert n_seq % tq == 0
    q_tiles = n_seq // tq
    ctx_blk0 = n_lat_rows // n_ctx
    small = [pl.BlockSpec((4, HEAD_DIM), lambda *_: (0, 0)), pl.BlockSpec((1, LANES), lambda *_: (0, 0))]
    o_lat = pl.pallas_call(
        functools.partial(_attn_lat_body, lam_init=lam_init),
        grid=(b, heads, q_tiles),
        in_specs=small + [
            pl.BlockSpec((tq, LANES), lambda bi, h, i: (bi * q_tiles + i, h)),
            pl.BlockSpec((n_ctx, LANES), lambda bi, h, i: (ctx_blk0 + bi, heads + h)),
            pl.BlockSpec((n_seq, LANES), lambda bi, h, i: (bi, heads + h)),
            pl.BlockSpec((n_ctx, LANES), lambda bi, h, i: (ctx_blk0 + bi, 2 * heads + h)),
            pl.BlockSpec((n_seq, LANES), lambda bi, h, i: (bi, 2 * heads + h)),
        ],
        out_specs=pl.BlockSpec((tq, LANES), lambda bi, h, i: (bi * q_tiles + i, h)),
        out_shape=jax.ShapeDtypeStruct((n_lat_rows, d), BF16),
        scratch_shapes=[pltpu.VMEM((n_ctx, 2 * LANES), BF16), pltpu.VMEM((n_seq, 2 * LANES), BF16)],
        compiler_params=_params("parallel", "parallel", "arbitrary"),
        name="attn_latent",
    )(lam_p, subln_g[None, :], qkv, qkv, qkv, qkv, qkv)
    if not update_ctx:
        return o_lat, None
    o_ctx = pl.pallas_call(
        functools.partial(_attn_ctx_body, lam_init=lam_init),
        grid=(b, heads),
        in_specs=small + [
            pl.BlockSpec((n_ctx, LANES), lambda bi, h: (ctx_blk0 + bi, h)),
            pl.BlockSpec((n_ctx, LANES), lambda bi, h: (ctx_blk0 + bi, heads + h)),
            pl.BlockSpec((n_ctx, LANES), lambda bi, h: (ctx_blk0 + bi, 2 * heads + h)),
        ],
        out_specs=pl.BlockSpec((n_ctx, LANES), lambda bi, h: (bi, h)),
        out_shape=jax.ShapeDtypeStruct((b * n_ctx, d), BF16),
        scratch_shapes=[pltpu.VMEM((n_ctx, 2 * LANES), BF16)],
        compiler_params=_params("parallel", "parallel"),
        name="attn_context",
    )(lam_p, subln_g[None, :], qkv, qkv, qkv)
    return o_lat, o_ctx


def _oproj_body(ol_ref, oc_ref, w_ref, x_ref, g1_ref, out_ref, *, n_lat_tiles):
    def finish(o_ref):
        y = jnp.dot(o_ref[...], w_ref[...], preferred_element_type=F32)
        out_ref[...] = x_ref[...] + g1_ref[0] * y

    @pl.when(pl.program_id(0) < n_lat_tiles)
    def _():
        finish(ol_ref)

    @pl.when(pl.program_id(0) >= n_lat_tiles)
    def _():
        finish(oc_ref)


def _out_proj(o_lat, o_ctx, w, xa, mod, r, layer, n_lat_rows, n_seq, b):
    d = o_lat.shape[1]
    tm = ROW_TILE
    n_lat_tiles = n_lat_rows // tm
    if o_ctx is None:
        o_ctx = o_lat
    n_rows = n_lat_rows + (0 if o_ctx is o_lat else o_ctx.shape[0])
    return pl.pallas_call(
        functools.partial(_oproj_body, n_lat_tiles=n_lat_tiles),
        grid=(n_rows // tm,),
        in_specs=[
            pl.BlockSpec((tm, d), lambda i: (jnp.minimum(i, n_lat_tiles - 1), 0)),
            pl.BlockSpec((tm, d), lambda i: (jnp.maximum(i - n_lat_tiles, 0), 0)),
            pl.BlockSpec((d, d), lambda i: (0, 0)),
            pl.BlockSpec((tm, d), lambda i: (i, 0)),
            _mod_spec(layer, 2, r, d, n_lat_tiles, tm, n_seq, b),
        ],
        out_specs=pl.BlockSpec((tm, d), lambda i: (i, 0)),
        out_shape=jax.ShapeDtypeStruct((n_rows, d), F32),
        compiler_params=_params("parallel"),
        name="attn_out_proj",
    )(o_lat, o_ctx, w.astype(BF16), xa, mod)


def _cproj_body(x_ref, sh_ref, sc_ref, g_ref, w_ref, gb_ref, u_ref, *, d):
    h = _norm_mod(x_ref[...], g_ref[...], sh_ref[0], sc_ref[0])
    acc = jnp.dot(h.astype(BF16), w_ref[...], preferred_element_type=F32)
    gb_ref[...] = acc[:, :d].astype(BF16)
    u_ref[...] = (acc[:, d:2 * d] * acc[:, 2 * d:]).astype(BF16)


def _conv_proj(xa, mod, r, layer, g, w, n_rows, n_lat_rows, n_seq, b):
    d = xa.shape[1]
    tm = ROW_TILE
    n_lat_tiles = n_lat_rows // tm
    return pl.pallas_call(
        functools.partial(_cproj_body, d=d),
        grid=(n_rows // tm,),
        in_specs=[
            pl.BlockSpec((tm, d), lambda i: (i, 0)),
            _mod_spec(layer, 0, r, d, n_lat_tiles, tm, n_seq, b),
            _mod_spec(layer, 1, r, d, n_lat_tiles, tm, n_seq, b),
            pl.BlockSpec((1, d), lambda i: (0, 0)),
            pl.BlockSpec((d, 3 * d), lambda i: (0, 0)),
        ],
        out_specs=[pl.BlockSpec((tm, d), lambda i: (i, 0)), pl.BlockSpec((tm, d), lambda i: (i, 0))],
        out_shape=[jax.ShapeDtypeStruct((n_rows, d), BF16), jax.ShapeDtypeStruct((n_rows, d), BF16)],
        compiler_params=_params("parallel"),
        name="conv_in_proj",
    )(xa, mod, mod, g[None, :], w.astype(BF16))


def _cout_body(u_ref, up_ref, un_ref, gb_ref, cw_ref, w_ref, x_ref, g1_ref, out_ref, *, tm, n_lat_rows, n_seq, n_ctx):
    i = pl.program_id(0)
    u = u_ref[...].astype(F32)
    prev_last = up_ref[BF16_SUBLANES - 1:BF16_SUBLANES, :].astype(F32)
    next_first = un_ref[0:1, :].astype(F32)
    row = lax.broadcasted_iota(I32, (tm, 1), 0)
    grow = i * tm + row
    is_lat = grow < n_lat_rows
    pos = jnp.where(is_lat, lax.rem(grow, n_seq), lax.rem(jnp.maximum(grow - n_lat_rows, 0), n_ctx))
    seq_len = jnp.where(is_lat, n_seq, n_ctx)
    um = jnp.where(row == 0, prev_last, pltpu.roll(u, 1, 0))
    um = jnp.where(pos == 0, 0.0, um)
    un = jnp.where(row == tm - 1, next_first, pltpu.roll(u, tm - 1, 0))
    un = jnp.where(pos == seq_len - 1, 0.0, un)
    cw = cw_ref[...]
    conv = um * cw[0:1] + u * cw[1:2] + un * cw[2:3]
    z = gb_ref[...].astype(F32) * conv
    y = jnp.dot(z.astype(BF16), w_ref[...], preferred_element_type=F32)
    out_ref[...] = x_ref[...] + g1_ref[0] * y


def _conv_out(u, gb, conv_w, w, xa, mod, r, layer, n_lat_rows, n_seq, n_ctx, b):
    n_rows, d = u.shape
    tm = ROW_TILE
    halo = BF16_SUBLANES
    n_halo_blocks = n_rows // halo
    body = functools.partial(_cout_body, tm=tm, n_lat_rows=n_lat_rows, n_seq=n_seq, n_ctx=n_ctx)
    return pl.pallas_call(
        body,
        grid=(n_rows // tm,),
        in_specs=[
            pl.BlockSpec((tm, d), lambda i: (i, 0)),
            pl.BlockSpec((halo, d), lambda i: (jnp.maximum(i * (tm // halo) - 1, 0), 0)),
            pl.BlockSpec((halo, d), lambda i: (jnp.minimum((i + 1) * (tm // halo), n_halo_blocks - 1), 0)),
            pl.BlockSpec((tm, d), lambda i: (i, 0)),
            pl.BlockSpec((3, d), lambda i: (0, 0)),
            pl.BlockSpec((d, d), lambda i: (0, 0)),
            pl.BlockSpec((tm, d), lambda i: (i, 0)),
            _mod_spec(layer, 2, r, d, n_lat_rows // tm, tm, n_seq, b),
        ],
        out_specs=pl.BlockSpec((tm, d), lambda i: (i, 0)),
        out_shape=jax.ShapeDtypeStruct((n_rows, d), F32),
        compiler_params=_params("parallel"),
        name="conv_out_proj",
    )(u, u, u, gb, conv_w, w.astype(BF16), xa, mod)


def _router_body(x_ref, sh_ref, sc_ref, g_ref, rwt_ref, rb_ref, earlier_ref, h_ref, idx_ref, gate_ref, rank_ref,
                 count_ref, *, tm, d, n_exp):
    @pl.when(pl.program_id(0) == 0)
    def _():
        count_ref[...] = jnp.zeros_like(count_ref)

    h = _norm_mod(x_ref[...], g_ref[...], sh_ref[0], sc_ref[0])
    slabs = d // LANES
    for s in range(slabs):
        h_ref[pl.ds(s, tm, stride=slabs), :] = h[:, s * LANES:(s + 1) * LANES]
    logits = lax.dot_general(rwt_ref[...], h, _NT, preferred_element_type=F32,
                             precision=lax.Precision.HIGHEST) + rb_ref[...]
    eidx = lax.broadcasted_iota(I32, (n_exp, tm), 0)
    tops, sels = [], []
    for _ in range(TOP_K):
        m = jnp.max(logits, axis=0, keepdims=True)
        sel = jnp.min(jnp.where(logits == m, eidx, n_exp), axis=0, keepdims=True)
        tops.append(m)
        sels.append(sel)
        logits = jnp.where(eidx == sel, -jnp.inf, logits)
    es = [jnp.exp(m - tops[0]) for m in tops]
    den = functools.reduce(jnp.add, es)
    idx_ref[...] = jnp.concatenate(sels, axis=0)
    gate_ref[...] = jnp.concatenate([e / den for e in es], axis=0)
    onehots = [(eidx == sel).astype(F32) for sel in sels]
    prefix = jnp.dot(jnp.concatenate(onehots, axis=0).astype(BF16), earlier_ref[...], preferred_element_type=F32)
    offset = count_ref[...]
    ranks = []
    for k in range(TOP_K):
        within = prefix[k * n_exp:(k + 1) * n_exp] + offset
        ranks.append(jnp.sum(onehots[k] * within, axis=0, keepdims=True))
        offset = offset + jnp.sum(onehots[k], axis=1, keepdims=True)
    rank_ref[...] = jnp.concatenate(ranks, axis=0).astype(I32)
    count_ref[...] = offset


def _router(xa, mod, r, layer, g, router_w, router_b, n_rows, n_lat_rows, n_seq, b):
    d = xa.shape[1]
    n_exp = router_w.shape[1]
    tm = ROW_TILE
    slabs = d // LANES
    n_lat_tiles = n_lat_rows // tm
    body = functools.partial(_router_body, tm=tm, d=d, n_exp=n_exp)
    earlier = (jnp.arange(tm)[:, None] < jnp.arange(tm)[None, :]).astype(BF16)
    return pl.pallas_call(
        body,
        grid=(n_rows // tm,),
        in_specs=[
            pl.BlockSpec((tm, d), lambda i: (i, 0)),
            _mod_spec(layer, 3, r, d, n_lat_tiles, tm, n_seq, b),
            _mod_spec(layer, 4, r, d, n_lat_tiles, tm, n_seq, b),
            pl.BlockSpec((1, d), lambda i: (0, 0)),
            pl.BlockSpec((n_exp, d), lambda i: (0, 0)),
            pl.BlockSpec((n_exp, 1), lambda i: (0, 0)),
            pl.BlockSpec((tm, tm), lambda i: (0, 0)),
        ],
        out_specs=[
            pl.BlockSpec((tm * slabs, LANES), lambda i: (i, 0)),
            pl.BlockSpec((TOP_K, tm), lambda i: (0, i)),
            pl.BlockSpec((TOP_K, tm), lambda i: (0, i)),
            pl.BlockSpec((TOP_K, tm), lambda i: (0, i)),
            pl.BlockSpec((n_exp, 1), lambda i: (0, 0)),
        ],
        out_shape=[
            jax.ShapeDtypeStruct((n_rows * slabs, LANES), F32),
            jax.ShapeDtypeStruct((TOP_K, n_rows), I32),
            jax.ShapeDtypeStruct((TOP_K, n_rows), F32),
            jax.ShapeDtypeStruct((TOP_K, n_rows), I32),
            jax.ShapeDtypeStruct((n_exp, 1), F32),
        ],
        compiler_params=_params("arbitrary"),
        name="moe_router",
    )(xa, mod, mod, g[None, :], router_w.T, router_b[:, None], earlier)


def _dispatch_plan(top_idx, rank, counts, tile, slabs):
    n_tok = top_idx.shape[1]
    n_exp = counts.shape[0]
    n_assign = TOP_K * n_tok
    n_tiles = n_assign // tile
    counts = counts[:, 0].astype(I32)
    end = jnp.cumsum(counts)
    start = end - counts
    experts = jnp.arange(n_exp, dtype=I32)
    pos = rank + jnp.sum(jnp.where(top_idx[:, :, None] == experts, start, 0), axis=-1)
    slot_of_row = jnp.zeros((n_assign,), I32).at[pos.reshape(-1)].add(jnp.arange(n_assign, dtype=I32))
    row_token = ((slot_of_row % n_tok) * slabs).reshape(n_tiles, 1, tile)
    row_slot = (slot_of_row * slabs).reshape(n_tiles, 1, tile)
    first_tile = start // tile
    visits = jnp.where(counts > 0, (end - 1) // tile - first_tile + 1, 0)
    visit_end = jnp.cumsum(visits)
    visit_start = visit_end - visits
    n_visits = visit_end[-1]
    max_visits = n_tiles + n_exp - 1
    v = jnp.arange(max_visits, dtype=I32)
    valid = v < n_visits
    e_of = jnp.minimum(jnp.sum((v[:, None] >= visit_end[None, :]).astype(I32), axis=1), n_exp - 1)
    t_of = first_tile[e_of] + v - visit_start[e_of]
    lo = jnp.maximum(start[e_of], t_of * tile) - t_of * tile
    hi = jnp.minimum(end[e_of], (t_of + 1) * tile) - t_of * tile
    last = jnp.maximum(n_visits - 1, 0)
    e_of = jnp.where(valid, e_of, e_of[last])
    t_of = jnp.where(valid, t_of, t_of[last])
    lo = jnp.where(valid, lo, 0)
    hi = jnp.where(valid, hi, 0)
    return row_token, row_slot, t_of.astype(I32), e_of, lo.astype(I32), hi.astype(I32), n_visits.reshape(1).astype(I32)


def _expert_body(tile_ref, exp_ref, lo_ref, hi_ref, nv_ref,
                 tokn_ref, slotp_ref, slot_ref, tok0_ref, h_hbm, wg_ref, wl_ref, wd_ref, bg_ref, bl_ref, bd_ref,
                 y_hbm, gbuf, xs, yacc, ybuf, wd_bf, gsem, ssem, *, tile, slabs):
    v = pl.program_id(0)
    n_visits = nv_ref[0]
    t = tile_ref[v]
    new_expert = jnp.logical_or(v == 0, exp_ref[jnp.maximum(v - 1, 0)] != exp_ref[v])
    first_visit = jnp.logical_or(v == 0, tile_ref[jnp.maximum(v - 1, 0)] != t)
    last_visit = jnp.logical_or(v == n_visits - 1, tile_ref[jnp.minimum(v + 1, pl.num_programs(0) - 1)] != t)
    valid = v < n_visits
    straight = jnp.logical_and(jnp.logical_and(first_visit, last_visit), t >= 1)
    g_cur = lax.rem(t, 2)
    y_cur = lax.rem(t, 3)
    y_prev = lax.rem(t + 2, 3)
    rows = tile * slabs

    def start_gather(idx_ref, g):
        for r in range(tile):
            src = pl.multiple_of(idx_ref[0, 0, r], slabs)
            pltpu.make_async_copy(h_hbm.at[pl.ds(src, slabs), :], gbuf.at[g, pl.ds(r * slabs, slabs), :],
                                  gsem.at[g]).start()

    def wait_gather(g):
        pltpu.make_async_copy(h_hbm.at[pl.ds(0, rows), :], gbuf.at[g], gsem.at[g]).wait()

    def start_scatter(idx_ref, yb):
        for r in range(tile):
            dst = pl.multiple_of(idx_ref[0, 0, r], slabs)
            pltpu.make_async_copy(ybuf.at[yb, pl.ds(r * slabs, slabs), :], y_hbm.at[pl.ds(dst, slabs), :],
                                  ssem.at[yb]).start()

    def wait_scatter(yb):
        pltpu.make_async_copy(ybuf.at[yb], y_hbm.at[pl.ds(0, rows), :], ssem.at[yb]).wait()

    def begin_tile():
        start_gather(tokn_ref, 1 - g_cur)
        for s in range(slabs):
            xs[:, s * LANES:(s + 1) * LANES] = gbuf[g_cur, pl.ds(s, tile, stride=slabs), :].astype(BF16)

    def ffn():
        x = xs[...]
        gl = lax.dot_general(x, wg_ref[...], _NT, preferred_element_type=F32) + bg_ref[...]
        ln = lax.dot_general(x, wl_ref[...], _NT, preferred_element_type=F32) + bl_ref[...]
        x_glu = jnp.minimum(gl, SWIGLU_LIMIT)
        x_lin = jnp.clip(ln, -SWIGLU_LIMIT, SWIGLU_LIMIT)
        act = x_glu * jax.nn.sigmoid(SWIGLU_ALPHA * x_glu) * (x_lin + 1.0)
        return jnp.dot(act.astype(BF16), wd_bf[...], preferred_element_type=F32) + bd_ref[...]

    def stage(y_of_slab):
        for s in range(slabs):
            ybuf[y_cur, pl.ds(s, tile, stride=slabs), :] = y_of_slab(s)

    @pl.when(v == 0)
    def _():
        start_gather(tok0_ref, 0)

    @pl.when(jnp.logical_and(valid, new_expert))
    def _():
        wd_bf[...] = wd_ref[...].astype(BF16)

    @pl.when(jnp.logical_and(valid, first_visit))
    def _():
        wait_gather(g_cur)

        @pl.when(t >= 3)
        def _():
            wait_scatter(y_cur)

    @pl.when(jnp.logical_and(valid, straight))
    def _():
        start_scatter(slotp_ref, y_prev)
        begin_tile()
        y = ffn()
        stage(lambda s: y[:, s * LANES:(s + 1) * LANES])

    @pl.when(jnp.logical_and(valid, jnp.logical_not(straight)))
    def _():
        @pl.when(first_visit)
        def _():
            @pl.when(t >= 1)
            def _():
                start_scatter(slotp_ref, y_prev)

            begin_tile()

        y = ffn()

        @pl.when(first_visit)
        def _():
            yacc[...] = y

        @pl.when(jnp.logical_not(first_visit))
        def _():
            row = lax.broadcasted_iota(I32, (tile, 1), 0)
            mine = jnp.logical_and(row >= lo_ref[v], row < hi_ref[v])
            yacc[...] = jnp.where(mine, y, yacc[...])

        @pl.when(last_visit)
        def _():
            stage(lambda s: yacc[:, s * LANES:(s + 1) * LANES])

    @pl.when(v == n_visits - 1)
    def _():
        start_scatter(slot_ref, y_cur)
        wait_gather(1 - g_cur)
        wait_scatter(y_cur)

        @pl.when(t >= 1)
        def _():
            wait_scatter(y_prev)

        @pl.when(t >= 2)
        def _():
            wait_scatter(lax.rem(t + 1, 3))


def _experts(h_slab, plan, layer, wg, wl, wd, bg, bl, bd, n_tok):
    row_token, row_slot, t_of, e_of, lo, hi, n_visits = plan
    n_tiles, _, tile = row_token.shape
    _, n_exp, f, d = wg.shape
    slabs = d // LANES
    max_visits = t_of.shape[0]

    def expert_block(v, tl, ex, lo_, hi_, nv):
        return (layer, ex[v], 0, 0)

    body = functools.partial(_expert_body, tile=tile, slabs=slabs)
    smem_tile = functools.partial(pl.BlockSpec, (1, 1, tile), memory_space=pltpu.SMEM)
    grid_spec = pltpu.PrefetchScalarGridSpec(
        num_scalar_prefetch=5,
        grid=(max_visits,),
        in_specs=[
            smem_tile(lambda v, tl, ex, lo_, hi_, nv: (jnp.minimum(tl[v] + 1, n_tiles - 1), 0, 0)),
            smem_tile(lambda v, tl, ex, lo_, hi_, nv: (jnp.maximum(tl[v] - 1, 0), 0, 0)),
            smem_tile(lambda v, tl, ex, lo_, hi_, nv: (tl[v], 0, 0)),
            smem_tile(lambda v, tl, ex, lo_, hi_, nv: (0, 0, 0)),
            pl.BlockSpec(memory_space=pl.ANY),
            pl.BlockSpec((None, None, f, d), expert_block),
            pl.BlockSpec((None, None, f, d), expert_block),
            pl.BlockSpec((None, None, f, d), expert_block),
            pl.BlockSpec((None, None, 1, f), expert_block),
            pl.BlockSpec((None, None, 1, f), expert_block),
            pl.BlockSpec((None, None, 1, d), expert_block),
        ],
        out_specs=pl.BlockSpec(memory_space=pl.ANY),
        scratch_shapes=[
            pltpu.VMEM((2, tile * slabs, LANES), F32),
            pltpu.VMEM((tile, d), BF16),
            pltpu.VMEM((tile, d), F32),
            pltpu.VMEM((3, tile * slabs, LANES), F32),
            pltpu.VMEM((f, d), BF16),
            pltpu.SemaphoreType.DMA((2,)),
            pltpu.SemaphoreType.DMA((3,)),
        ],
    )
    return pl.pallas_call(
        body,
        grid_spec=grid_spec,
        out_shape=jax.ShapeDtypeStruct((TOP_K * n_tok * slabs, LANES), F32),
        compiler_params=_params("arbitrary"),
        name="moe_experts",
    )(t_of, e_of, lo, hi, n_visits, row_token, row_slot, row_slot, row_token, h_slab, wg, wl, wd, bg, bl, bd)


def _combine_body(x_ref, g2_ref, gate_ref, *refs, tm, slabs):
    y_refs, out_ref = refs[:TOP_K], refs[TOP_K]
    gates = gate_ref[...]
    g2 = g2_ref[0]
    for s in range(slabs):
        acc = None
        for k in range(TOP_K):
            part = gates[:, k:k + 1] * y_refs[k][pl.ds(s, tm, stride=slabs), :]
            acc = part if acc is None else acc + part
        cols = slice(s * LANES, (s + 1) * LANES)
        out_ref[:, cols] = x_ref[:, cols] + g2[:, cols] * acc


def _combine(xa, y_slab, gates_t, mod, r, layer, n_rows, n_lat_rows, n_seq, b):
    d = xa.shape[1]
    tm = ROW_TILE
    slabs = d // LANES
    n_tiles = n_rows // tm
    body = functools.partial(_combine_body, tm=tm, slabs=slabs)
    y_specs = [pl.BlockSpec((tm * slabs, LANES), functools.partial(lambda i, k: (k * n_tiles + i, 0), k=k))
               for k in range(TOP_K)]
    return pl.pallas_call(
        body,
        grid=(n_tiles,),
        in_specs=[
            pl.BlockSpec((tm, d), lambda i: (i, 0)),
            _mod_spec(layer, 5, r, d, n_lat_rows // tm, tm, n_seq, b),
            pl.BlockSpec((tm, TOP_K), lambda i: (i, 0)),
        ] + y_specs,
        out_specs=pl.BlockSpec((tm, d), lambda i: (i, 0)),
        out_shape=jax.ShapeDtypeStruct((n_rows, d), F32),
        compiler_params=_params("parallel"),
        name="moe_combine",
    )(xa, mod, gates_t, *([y_slab] * TOP_K))


def _wsplit_body(w_ref, wg_ref, wl_ref, ts):
    d, f2 = w_ref.shape
    half = LANES // 2
    for c in range(f2 // LANES):
        for j in range(d // LANES):
            blk = (c * (d // LANES) + j) * LANES
            ts[pl.ds(blk, LANES), :] = w_ref[j * LANES:(j + 1) * LANES, c * LANES:(c + 1) * LANES].T
            rows = slice(c * half, (c + 1) * half)
            cols = slice(j * LANES, (j + 1) * LANES)
            wg_ref[rows, cols] = ts[pl.ds(blk, half, stride=2), :].astype(BF16)
            wl_ref[rows, cols] = ts[pl.ds(blk + 1, half, stride=2), :].astype(BF16)


def _split_gate_up(w_gu):
    depth, n_exp, d, f2 = w_gu.shape
    f = f2 // 2
    out = jax.ShapeDtypeStruct((depth, n_exp, f, d), BF16)
    return pl.pallas_call(
        _wsplit_body,
        grid=(depth, n_exp),
        in_specs=[pl.BlockSpec((None, None, d, f2), lambda l, e: (l, e, 0, 0))],
        out_specs=[pl.BlockSpec((None, None, f, d), lambda l, e: (l, e, 0, 0))] * 2,
        out_shape=[out, out],
        scratch_shapes=[pltpu.VMEM((d * f2 // LANES, LANES), F32)],
        compiler_params=_params("parallel", "parallel"),
        name="moe_split_gate_up",
    )(w_gu)


def _moe(xa, mod, r, layer, g, router_w, router_b, expert_w, n_rows, n_lat_rows, n_seq, b):
    h_slab, top_idx, gates, rank, counts = _router(xa, mod, r, layer, g, router_w, router_b, n_rows, n_lat_rows,
                                                   n_seq, b)
    plan = _dispatch_plan(top_idx, rank, counts, MOE_TILE, xa.shape[1] // LANES)
    y_slab = _experts(h_slab, plan, layer, *expert_w, n_rows)
    return _combine(xa, y_slab, gates.T, mod, r, layer, n_rows, n_lat_rows, n_seq, b)


def _final_body(x_ref, g_ref, o_ref):
    x = x_ref[...]
    ms = jnp.mean(x * x, axis=-1, keepdims=True)
    o_ref[...] = x * lax.rsqrt(ms + EPS) * g_ref[...]


def _final_norm(xa, g, n_rows):
    d = xa.shape[1]
    tm = ROW_TILE
    return pl.pallas_call(
        _final_body,
        grid=(n_rows // tm,),
        in_specs=[pl.BlockSpec((tm, d), lambda i: (i, 0)), pl.BlockSpec((1, d), lambda i: (0, 0))],
        out_specs=pl.BlockSpec((tm, d), lambda i: (i, 0)),
        out_shape=jax.ShapeDtypeStruct((n_rows, d), F32),
        compiler_params=_params("parallel"),
        name="final_norm",
    )(xa, g[None, :])


def kernel(x, c, ctx, c_ctx, mod_w, mod_b, norm_g, attn_w_qkv, attn_w_o, attn_lambda, attn_subln_g, conv_w_in, conv_w, conv_w_out, router_w, router_b, moe_w_gu, moe_b_gu, moe_w_down, moe_b_down, final_g):
    b, n_seq, d = x.shape
    n_ctx = ctx.shape[1]
    depth = mod_w.shape[0]
    n_lat_rows = b * n_seq
    n_all_rows = n_lat_rows + b * n_ctx
    assert d % LANES == 0 and d == moe_w_down.shape[-1]
    assert n_seq % ROW_TILE == 0 and (b * n_ctx) % ROW_TILE == 0 and n_seq % GRID_W == 0
    assert ROW_TILE % n_ctx == 0 or n_ctx % ROW_TILE == 0
    assert n_lat_rows % n_ctx == 0 and n_ctx % BF16_SUBLANES == 0
    assert (TOP_K * n_lat_rows) % MOE_TILE == 0 and (TOP_K * n_all_rows) % MOE_TILE == 0

    xa = jnp.concatenate([x.reshape(n_lat_rows, d), ctx.reshape(b * n_ctx, d)], axis=0)
    mod, r = _modulation(c, c_ctx, mod_w, mod_b)
    wg, wl = _split_gate_up(moe_w_gu)
    expert_w = (wg, wl, moe_w_down, moe_b_gu[:, :, None, 0::2], moe_b_gu[:, :, None, 1::2],
                moe_b_down[:, :, None, :])
    for i in range(depth):
        update_ctx = i < depth - 1
        attn_layer = i % N_MIXERS == 0
        j = i // N_MIXERS
        n_out = n_all_rows if update_ctx else n_lat_rows
        if attn_layer:
            lam_init = 0.8 - 0.6 * math.exp(-0.3 * i)
            qkv = _qkv_proj(xa, mod, r, i, norm_g[i, 0], attn_w_qkv[j], n_all_rows, n_lat_rows, n_seq, b)
            o_lat, o_ctx = _attention(qkv, attn_lambda[j], attn_subln_g[j], lam_init, b, n_seq, n_ctx, n_lat_rows,
                                      update_ctx)
            xa = _out_proj(o_lat, o_ctx, attn_w_o[j], xa, mod, r, i, n_lat_rows, n_seq, b)
        else:
            gb, u = _conv_proj(xa, mod, r, i, norm_g[i, 0], conv_w_in[j], n_out, n_lat_rows, n_seq, b)
            xa = _conv_out(u, gb, conv_w[j], conv_w_out[j], xa, mod, r, i, n_lat_rows, n_seq, n_ctx, b)
        xa = _moe(xa, mod, r, i, norm_g[i, 1], router_w[i], router_b[i], expert_w, n_out, n_lat_rows, n_seq, b)
    return _final_norm(xa, final_g, n_lat_rows).reshape(b, n_seq, d)
```

```python
import functools
import math

import jax
import jax.numpy as jnp
from jax import lax
from jax.experimental import pallas as pl
from jax.experimental.pallas import tpu as pltpu

F32 = jnp.float32
BF16 = jnp.bfloat16
I32 = jnp.int32
U32 = jnp.uint32

GRID_W = 64
HEAD_DIM = 64
ROPE_BASE = 10000.0
N_MIXERS = 2
TOP_K = 4
SWIGLU_LIMIT = 7.0
SWIGLU_ALPHA = 1.702
N_MOD = 6
EPS = 1e-5

LANES = 128
SUBLANES = 8
BF16_SUBLANES = 16
VMEM_LIMIT_BYTES = 48 * 1024 * 1024

ROW_TILE = 512
ATTN_Q_TILE = 512
MOE_TILE = 256

_NT = (((1,), (1,)), ((), ()))


def _round_up(a, b):
    return (a + b - 1) // b * b


def _params(*sem):
    return pltpu.CompilerParams(dimension_semantics=sem, vmem_limit_bytes=VMEM_LIMIT_BYTES)


def _norm_mod(x, g, shift, scale):
    ms = jnp.mean(x * x, axis=-1, keepdims=True)
    return (x * lax.rsqrt(ms + EPS) * g) * (1.0 + scale) + shift


def _mod_body(c_ref, w_ref, b_ref, o_ref):
    c = c_ref[...]
    s = c * jax.nn.sigmoid(c)
    o_ref[0] = jnp.dot(s, w_ref[0], preferred_element_type=F32, precision=lax.Precision.HIGHEST) + b_ref[0]


def _modulation(c, c_ctx, mod_w, mod_b):
    depth, d, nd = mod_w.shape
    b = c.shape[0]
    r = _round_up(b + 1, SUBLANES)
    cv = jnp.concatenate([c, c_ctx[None, :], jnp.zeros((r - b - 1, d), F32)], axis=0)
    tn = d
    out = pl.pallas_call(
        _mod_body,
        grid=(depth, nd // tn),
        in_specs=[
            pl.BlockSpec((r, d), lambda l, j: (0, 0)),
            pl.BlockSpec((1, d, tn), lambda l, j: (l, 0, j)),
            pl.BlockSpec((1, 1, tn), lambda l, j: (l, 0, j)),
        ],
        out_specs=pl.BlockSpec((1, r, tn), lambda l, j: (l, 0, j)),
        out_shape=jax.ShapeDtypeStruct((depth, r, nd), F32),
        compiler_params=_params("parallel", "parallel"),
        name="modulation",
    )(cv, mod_w, mod_b.reshape(depth, 1, nd))
    return out.reshape(depth * r, 1, nd), r


def _mod_spec(layer, chunk, r, d, n_lat_tiles, tm, n_seq, b):
    def index(i, *_):
        row = jnp.where(i < n_lat_tiles, (i * tm) // n_seq, b)
        return (layer * r + row, 0, chunk)

    return pl.BlockSpec((1, 1, d), index)


def _rope_tables(n, tm):
    rows = n // GRID_W
    row_pos = jnp.repeat(jnp.arange(rows, dtype=F32), GRID_W)
    col_pos = jnp.tile(jnp.arange(GRID_W, dtype=F32), rows)
    axis_dims = HEAD_DIM // 2
    inv_freq = ROPE_BASE ** (-jnp.arange(0, axis_dims, 2, dtype=F32) / axis_dims)
    ang = jnp.concatenate([row_pos[:, None] * inv_freq, col_pos[:, None] * inv_freq], axis=-1)
    cos, sin = jnp.cos(ang), jnp.sin(ang)
    reps = LANES // (HEAD_DIM // 2)
    cos_t = jnp.tile(cos, (1, reps))
    sign = jnp.where((jnp.arange(LANES) % HEAD_DIM) < HEAD_DIM // 2, -1.0, 1.0).astype(F32)
    sin_t = jnp.tile(sin, (1, reps)) * sign
    cos_t = jnp.concatenate([cos_t, jnp.ones((tm, LANES), F32)], axis=0)
    sin_t = jnp.concatenate([sin_t, jnp.zeros((tm, LANES), F32)], axis=0)
    return cos_t, sin_t


def _qkv_body(x_ref, sh_ref, sc_ref, g_ref, w_ref, cos_ref, sin_ref, o_ref, *, d):
    h = _norm_mod(x_ref[...], g_ref[...], sh_ref[0], sc_ref[0])
    acc = jnp.dot(h.astype(BF16), w_ref[...], preferred_element_type=F32)
    cos = cos_ref[...]
    sin = sin_ref[...]
    lane = lax.broadcasted_iota(I32, (1, LANES), 1)
    first_half = (lane % HEAD_DIM) < HEAD_DIM // 2
    q_scale = HEAD_DIM**-0.5 * math.log2(math.e)
    for gi in range(2 * d // LANES):
        xg = acc[:, gi * LANES:(gi + 1) * LANES]
        partner = jnp.where(first_half, pltpu.roll(xg, LANES - HEAD_DIM // 2, 1), pltpu.roll(xg, HEAD_DIM // 2, 1))
        r = xg * cos + partner * sin
        if gi < d // LANES:
            r = r * q_scale
        o_ref[:, gi * LANES:(gi + 1) * LANES] = r.astype(BF16)
    o_ref[:, 2 * d:] = acc[:, 2 * d:].astype(BF16)


def _qkv_proj(xa, mod, r, layer, g, w, n_rows, n_lat_rows, n_seq, b):
    d = xa.shape[1]
    tm = ROW_TILE
    n_lat_tiles = n_lat_rows // tm
    tiles_per_seq = n_seq // tm
    cos_t, sin_t = _rope_tables(n_seq, tm)

    def rope_index(i):
        return (jnp.where(i < n_lat_tiles, i % tiles_per_seq, tiles_per_seq), 0)

    return pl.pallas_call(
        functools.partial(_qkv_body, d=d),
        grid=(n_rows // tm,),
        in_specs=[
            pl.BlockSpec((tm, d), lambda i: (i, 0)),
            _mod_spec(layer, 0, r, d, n_lat_tiles, tm, n_seq, b),
            _mod_spec(layer, 1, r, d, n_lat_tiles, tm, n_seq, b),
            pl.BlockSpec((1, d), lambda i: (0, 0)),
            pl.BlockSpec((d, 3 * d), lambda i: (0, 0)),
            pl.BlockSpec((tm, LANES), rope_index),
            pl.BlockSpec((tm, LANES), rope_index),
        ],
        out_specs=pl.BlockSpec((tm, 3 * d), lambda i: (i, 0)),
        out_shape=jax.ShapeDtypeStruct((n_rows, 3 * d), BF16),
        compiler_params=_params("parallel"),
        name="qkv_proj",
    )(xa, mod, mod, g[None, :], w.astype(BF16), cos_t, sin_t)


def _diff_attention(lam_ref, sg_ref, q_ref, k_refs, va_refs, o_ref, lam_init):
    q = q_ref[...]
    lane = lax.broadcasted_iota(I32, (1, LANES), 1)
    lp = lam_ref[...]
    lam = (jnp.exp(jnp.sum(lp[0:1] * lp[1:2], axis=-1, keepdims=True))
           - jnp.exp(jnp.sum(lp[2:3] * lp[3:4], axis=-1, keepdims=True)) + lam_init)
    scores = []
    for comp in range(2):
        in_comp = (lane < HEAD_DIM) if comp == 0 else (lane >= HEAD_DIM)
        qc = jnp.where(in_comp, q, jnp.zeros_like(q))
        scores.append([lax.dot_general(qc, k[...], _NT, preferred_element_type=F32) for k in k_refs])
    outs = []
    for ss in scores:
        m = functools.reduce(jnp.maximum, [jnp.max(s, axis=-1, keepdims=True) for s in ss])
        acc = None
        for s, va in zip(ss, va_refs):
            part = jnp.dot(jnp.exp2(s - m).astype(BF16), va[...], preferred_element_type=F32)
            acc = part if acc is None else acc + part
        outs.append(acc[:, :LANES] / acc[:, LANES:])
    o = outs[0] - lam * outs[1]
    ms = jnp.mean(o * o, axis=-1, keepdims=True)
    o_ref[...] = (o * lax.rsqrt(ms + EPS) * sg_ref[...] * (1.0 - lam_init)).astype(BF16)


def _fill_value_ones(va, v_ref):
    va[:, :LANES] = v_ref[...]
    va[:, LANES:] = jnp.ones((va.shape[0], LANES), BF16)


def _attn_lat_body(lam_ref, sg_ref, q_ref, kc_ref, kl_ref, vc_ref, vl_ref, o_ref, vac, val, *, lam_init):
    @pl.when(pl.program_id(2) == 0)
    def _():
        _fill_value_ones(vac, vc_ref)
        _fill_value_ones(val, vl_ref)

    _diff_attention(lam_ref, sg_ref, q_ref, [kc_ref, kl_ref], [vac, val], o_ref, lam_init)


def _attn_ctx_body(lam_ref, sg_ref, q_ref, kc_ref, vc_ref, o_ref, vac, *, lam_init):
    _fill_value_ones(vac, vc_ref)
    _diff_attention(lam_ref, sg_ref, q_ref, [kc_ref], [vac], o_ref, lam_init)


def _attention(qkv, lam_p, subln_g, lam_init, b, n_seq, n_ctx, n_lat_rows, update_ctx):
    d = qkv.shape[1] // 3
    heads = d // LANES
    tq = ATTN_Q_TILE
    assert n_seq % tq == 0
    q_tiles = n_seq // tq
    ctx_blk0 = n_lat_rows // n_ctx
    small = [pl.BlockSpec((4, HEAD_DIM), lambda *_: (0, 0)), pl.BlockSpec((1, LANES), lambda *_: (0, 0))]
    o_lat = pl.pallas_call(
        functools.partial(_attn_lat_body, lam_init=lam_init),
        grid=(b, heads, q_tiles),
        in_specs=small + [
            pl.BlockSpec((tq, LANES), lambda bi, h, i: (bi * q_tiles + i, h)),
            pl.BlockSpec((n_ctx, LANES), lambda bi, h, i: (ctx_blk0 + bi, heads + h)),
            pl.BlockSpec((n_seq, LANES), lambda bi, h, i: (bi, heads + h)),
            pl.BlockSpec((n_ctx, LANES), lambda bi, h, i: (ctx_blk0 + bi, 2 * heads + h)),
            pl.BlockSpec((n_seq, LANES), lambda bi, h, i: (bi, 2 * heads + h)),
        ],
        out_specs=pl.BlockSpec((tq, LANES), lambda bi, h, i: (bi * q_tiles + i, h)),
        out_shape=jax.ShapeDtypeStruct((n_lat_rows, d), BF16),
        scratch_shapes=[pltpu.VMEM((n_ctx, 2 * LANES), BF16), pltpu.VMEM((n_seq, 2 * LANES), BF16)],
        compiler_params=_params("parallel", "parallel", "arbitrary"),
        name="attn_latent",
    )(lam_p, subln_g[None, :], qkv, qkv, qkv, qkv, qkv)
    if not update_ctx:
        return o_lat, None
    o_ctx = pl.pallas_call(
        functools.partial(_attn_ctx_body, lam_init=lam_init),
        grid=(b, heads),
        in_specs=small + [
            pl.BlockSpec((n_ctx, LANES), lambda bi, h: (ctx_blk0 + bi, h)),
            pl.BlockSpec((n_ctx, LANES), lambda bi, h: (ctx_blk0 + bi, heads + h)),
            pl.BlockSpec((n_ctx, LANES), lambda bi, h: (ctx_blk0 + bi, 2 * heads + h)),
        ],
        out_specs=pl.BlockSpec((n_ctx, LANES), lambda bi, h: (bi, h)),
        out_shape=jax.ShapeDtypeStruct((b * n_ctx, d), BF16),
        scratch_shapes=[pltpu.VMEM((n_ctx, 2 * LANES), BF16)],
        compiler_params=_params("parallel", "parallel"),
        name="attn_context",
    )(lam_p, subln_g[None, :], qkv, qkv, qkv)
    return o_lat, o_ctx


def _oproj_body(ol_ref, oc_ref, w_ref, x_ref, g1_ref, out_ref, *, n_lat_tiles):
    def finish(o_ref):
        y = jnp.dot(o_ref[...], w_ref[...], preferred_element_type=F32)
        out_ref[...] = x_ref[...] + g1_ref[0] * y

    @pl.when(pl.program_id(0) < n_lat_tiles)
    def _():
        finish(ol_ref)

    @pl.when(pl.program_id(0) >= n_lat_tiles)
    def _():
        finish(oc_ref)


def _out_proj(o_lat, o_ctx, w, xa, mod, r, layer, n_lat_rows, n_seq, b):
    d = o_lat.shape[1]
    tm = ROW_TILE
    n_lat_tiles = n_lat_rows // tm
    if o_ctx is None:
        o_ctx = o_lat
    n_rows = n_lat_rows + (0 if o_ctx is o_lat else o_ctx.shape[0])
    return pl.pallas_call(
        functools.partial(_oproj_body, n_lat_tiles=n_lat_tiles),
        grid=(n_rows // tm,),
        in_specs=[
            pl.BlockSpec((tm, d), lambda i: (jnp.minimum(i, n_lat_tiles - 1), 0)),
            pl.BlockSpec((tm, d), lambda i: (jnp.maximum(i - n_lat_tiles, 0), 0)),
            pl.BlockSpec((d, d), lambda i: (0, 0)),
            pl.BlockSpec((tm, d), lambda i: (i, 0)),
            _mod_spec(layer, 2, r, d, n_lat_tiles, tm, n_seq, b),
        ],
        out_specs=pl.BlockSpec((tm, d), lambda i: (i, 0)),
        out_shape=jax.ShapeDtypeStruct((n_rows, d), F32),
        compiler_params=_params("parallel"),
        name="attn_out_proj",
    )(o_lat, o_ctx, w.astype(BF16), xa, mod)


def _cproj_body(x_ref, sh_ref, sc_ref, g_ref, w_ref, gb_ref, u_ref, *, d):
    h = _norm_mod(x_ref[...], g_ref[...], sh_ref[0], sc_ref[0])
    acc = jnp.dot(h.astype(BF16), w_ref[...], preferred_element_type=F32)
    gb_ref[...] = acc[:, :d].astype(BF16)
    u_ref[...] = (acc[:, d:2 * d] * acc[:, 2 * d:]).astype(BF16)


def _conv_proj(xa, mod, r, layer, g, w, n_rows, n_lat_rows, n_seq, b):
    d = xa.shape[1]
    tm = ROW_TILE
    n_lat_tiles = n_lat_rows // tm
    return pl.pallas_call(
        functools.partial(_cproj_body, d=d),
        grid=(n_rows // tm,),
        in_specs=[
            pl.BlockSpec((tm, d), lambda i: (i, 0)),
            _mod_spec(layer, 0, r, d, n_lat_tiles, tm, n_seq, b),
            _mod_spec(layer, 1, r, d, n_lat_tiles, tm, n_seq, b),
            pl.BlockSpec((1, d), lambda i: (0, 0)),
            pl.BlockSpec((d, 3 * d), lambda i: (0, 0)),
        ],
        out_specs=[pl.BlockSpec((tm, d), lambda i: (i, 0)), pl.BlockSpec((tm, d), lambda i: (i, 0))],
        out_shape=[jax.ShapeDtypeStruct((n_rows, d), BF16), jax.ShapeDtypeStruct((n_rows, d), BF16)],
        compiler_params=_params("parallel"),
        name="conv_in_proj",
    )(xa, mod, mod, g[None, :], w.astype(BF16))


def _cout_body(u_ref, up_ref, un_ref, gb_ref, cw_ref, w_ref, x_ref, g1_ref, out_ref, *, tm, n_lat_rows, n_seq, n_ctx):
    i = pl.program_id(0)
    u = u_ref[...].astype(F32)
    prev_last = up_ref[BF16_SUBLANES - 1:BF16_SUBLANES, :].astype(F32)
    next_first = un_ref[0:1, :].astype(F32)
    row = lax.broadcasted_iota(I32, (tm, 1), 0)
    grow = i * tm + row
    is_lat = grow < n_lat_rows
    pos = jnp.where(is_lat, lax.rem(grow, n_seq), lax.rem(jnp.maximum(grow - n_lat_rows, 0), n_ctx))
    seq_len = jnp.where(is_lat, n_seq, n_ctx)
    um = jnp.where(row == 0, prev_last, pltpu.roll(u, 1, 0))
    um = jnp.where(pos == 0, 0.0, um)
    un = jnp.where(row == tm - 1, next_first, pltpu.roll(u, tm - 1, 0))
    un = jnp.where(pos == seq_len - 1, 0.0, un)
    cw = cw_ref[...]
    conv = um * cw[0:1] + u * cw[1:2] + un * cw[2:3]
    z = gb_ref[...].astype(F32) * conv
    y = jnp.dot(z.astype(BF16), w_ref[...], preferred_element_type=F32)
    out_ref[...] = x_ref[...] + g1_ref[0] * y


def _conv_out(u, gb, conv_w, w, xa, mod, r, layer, n_lat_rows, n_seq, n_ctx, b):
    n_rows, d = u.shape
    tm = ROW_TILE
    halo = BF16_SUBLANES
    n_halo_blocks = n_rows // halo
    body = functools.partial(_cout_body, tm=tm, n_lat_rows=n_lat_rows, n_seq=n_seq, n_ctx=n_ctx)
    return pl.pallas_call(
        body,
        grid=(n_rows // tm,),
        in_specs=[
            pl.BlockSpec((tm, d), lambda i: (i, 0)),
            pl.BlockSpec((halo, d), lambda i: (jnp.maximum(i * (tm // halo) - 1, 0), 0)),
            pl.BlockSpec((halo, d), lambda i: (jnp.minimum((i + 1) * (tm // halo), n_halo_blocks - 1), 0)),
            pl.BlockSpec((tm, d), lambda i: (i, 0)),
            pl.BlockSpec((3, d), lambda i: (0, 0)),
            pl.BlockSpec((d, d), lambda i: (0, 0)),
            pl.BlockSpec((tm, d), lambda i: (i, 0)),
            _mod_spec(layer, 2, r, d, n_lat_rows // tm, tm, n_seq, b),
        ],
        out_specs=pl.BlockSpec((tm, d), lambda i: (i, 0)),
        out_shape=jax.ShapeDtypeStruct((n_rows, d), F32),
        compiler_params=_params("parallel"),
        name="conv_out_proj",
    )(u, u, u, gb, conv_w, w.astype(BF16), xa, mod)


def _router_body(x_ref, sh_ref, sc_ref, g_ref, rw_hi_ref, rw_lo_ref, rb_ref, earlier_ref, h_ref, idx_ref, gate_ref,
                 rank_ref, count_ref, *, tm, d, n_exp):
    @pl.when(pl.program_id(0) == 0)
    def _():
        count_ref[...] = jnp.zeros_like(count_ref)

    h = _norm_mod(x_ref[...], g_ref[...], sh_ref[0], sc_ref[0])
    slabs = d // LANES
    for s in range(slabs):
        h_ref[pl.ds(s, tm, stride=slabs), :] = h[:, s * LANES:(s + 1) * LANES]
    h_hi = h.astype(BF16)
    h_lo = (h - h_hi.astype(F32)).astype(BF16)
    logits = (lax.dot_general(rw_hi_ref[...], h_hi, _NT, preferred_element_type=F32)
              + lax.dot_general(rw_hi_ref[...], h_lo, _NT, preferred_element_type=F32)
              + lax.dot_general(rw_lo_ref[...], h_hi, _NT, preferred_element_type=F32)) + rb_ref[...]
    eidx = lax.broadcasted_iota(I32, (n_exp, tm), 0)
    tops, sels = [], []
    for _ in range(TOP_K):
        m = jnp.max(logits, axis=0, keepdims=True)
        sel = jnp.min(jnp.where(logits == m, eidx, n_exp), axis=0, keepdims=True)
        tops.append(m)
        sels.append(sel)
        logits = jnp.where(eidx == sel, -jnp.inf, logits)
    es = [jnp.exp(m - tops[0]) for m in tops]
    den = functools.reduce(jnp.add, es)
    idx_ref[...] = jnp.concatenate(sels, axis=0)
    gate_ref[...] = jnp.concatenate([e / den for e in es], axis=0)
    onehots = [(eidx == sel).astype(F32) for sel in sels]
    prefix = jnp.dot(jnp.concatenate(onehots, axis=0).astype(BF16), earlier_ref[...], preferred_element_type=F32)
    offset = count_ref[...]
    ranks = []
    for k in range(TOP_K):
        within = prefix[k * n_exp:(k + 1) * n_exp] + offset
        ranks.append(jnp.sum(onehots[k] * within, axis=0, keepdims=True))
        offset = offset + jnp.sum(onehots[k], axis=1, keepdims=True)
    rank_ref[...] = jnp.concatenate(ranks, axis=0).astype(I32)
    count_ref[...] = offset


def _router(xa, mod, r, layer, g, router_w, router_b, n_rows, n_lat_rows, n_seq, b):
    d = xa.shape[1]
    n_exp = router_w.shape[1]
    tm = ROW_TILE
    slabs = d // LANES
    n_lat_tiles = n_lat_rows // tm
    body = functools.partial(_router_body, tm=tm, d=d, n_exp=n_exp)
    rw_t = router_w.T
    rw_hi = rw_t.astype(BF16)
    rw_lo = (rw_t - rw_hi.astype(F32)).astype(BF16)
    earlier = (jnp.arange(tm)[:, None] < jnp.arange(tm)[None, :]).astype(BF16)
    return pl.pallas_call(
        body,
        grid=(n_rows // tm,),
        in_specs=[
            pl.BlockSpec((tm, d), lambda i: (i, 0)),
            _mod_spec(layer, 3, r, d, n_lat_tiles, tm, n_seq, b),
            _mod_spec(layer, 4, r, d, n_lat_tiles, tm, n_seq, b),
            pl.BlockSpec((1, d), lambda i: (0, 0)),
            pl.BlockSpec((n_exp, d), lambda i: (0, 0)),
            pl.BlockSpec((n_exp, d), lambda i: (0, 0)),
            pl.BlockSpec((n_exp, 1), lambda i: (0, 0)),
            pl.BlockSpec((tm, tm), lambda i: (0, 0)),
        ],
        out_specs=[
            pl.BlockSpec((tm * slabs, LANES), lambda i: (i, 0)),
            pl.BlockSpec((TOP_K, tm), lambda i: (0, i)),
            pl.BlockSpec((TOP_K, tm), lambda i: (0, i)),
            pl.BlockSpec((TOP_K, tm), lambda i: (0, i)),
            pl.BlockSpec((n_exp, 1), lambda i: (0, 0)),
        ],
        out_shape=[
            jax.ShapeDtypeStruct((n_rows * slabs, LANES), F32),
            jax.ShapeDtypeStruct((TOP_K, n_rows), I32),
            jax.ShapeDtypeStruct((TOP_K, n_rows), F32),
            jax.ShapeDtypeStruct((TOP_K, n_rows), I32),
            jax.ShapeDtypeStruct((n_exp, 1), F32),
        ],
        compiler_params=_params("arbitrary"),
        name="moe_router",
    )(xa, mod, mod, g[None, :], rw_hi, rw_lo, router_b[:, None], earlier)


def _dispatch_plan(top_idx, rank, counts, tile, in_slabs, out_slabs):
    n_tok = top_idx.shape[1]
    n_exp = counts.shape[0]
    n_assign = TOP_K * n_tok
    n_tiles = n_assign // tile
    counts = counts[:, 0].astype(I32)
    end = jnp.cumsum(counts)
    start = end - counts
    experts = jnp.arange(n_exp, dtype=I32)
    pos = rank + jnp.sum(jnp.where(top_idx[:, :, None] == experts, start, 0), axis=-1)
    slot_of_row = jnp.zeros((n_assign,), I32).at[pos.reshape(-1)].add(jnp.arange(n_assign, dtype=I32))
    row_token = ((slot_of_row % n_tok) * in_slabs).reshape(n_tiles, 1, tile)
    row_slot = (slot_of_row * out_slabs).reshape(n_tiles, 1, tile)
    first_tile = start // tile
    visits = jnp.where(counts > 0, (end - 1) // tile - first_tile + 1, 0)
    visit_end = jnp.cumsum(visits)
    visit_start = visit_end - visits
    n_visits = visit_end[-1]
    max_visits = n_tiles + n_exp - 1
    v = jnp.arange(max_visits, dtype=I32)
    valid = v < n_visits
    e_of = jnp.minimum(jnp.sum((v[:, None] >= visit_end[None, :]).astype(I32), axis=1), n_exp - 1)
    t_of = first_tile[e_of] + v - visit_start[e_of]
    lo = jnp.maximum(start[e_of], t_of * tile) - t_of * tile
    hi = jnp.minimum(end[e_of], (t_of + 1) * tile) - t_of * tile
    last = jnp.maximum(n_visits - 1, 0)
    e_of = jnp.where(valid, e_of, e_of[last])
    t_of = jnp.where(valid, t_of, t_of[last])
    lo = jnp.where(valid, lo, 0)
    hi = jnp.where(valid, hi, 0)
    return row_token, row_slot, t_of.astype(I32), e_of, lo.astype(I32), hi.astype(I32), n_visits.reshape(1).astype(I32)


def _expert_body(tile_ref, exp_ref, lo_ref, hi_ref, nv_ref,
                 tokn_ref, slotp_ref, slot_ref, tok0_ref, h_hbm, wg_ref, wl_ref, wd_ref, bg_ref, bl_ref, bd_ref,
                 y_hbm, gbuf, xs, yacc, ybuf, wd_bf, gsem, ssem, *, tile, slabs):
    v = pl.program_id(0)
    n_visits = nv_ref[0]
    t = tile_ref[v]
    new_expert = jnp.logical_or(v == 0, exp_ref[jnp.maximum(v - 1, 0)] != exp_ref[v])
    first_visit = jnp.logical_or(v == 0, tile_ref[jnp.maximum(v - 1, 0)] != t)
    last_visit = jnp.logical_or(v == n_visits - 1, tile_ref[jnp.minimum(v + 1, pl.num_programs(0) - 1)] != t)
    valid = v < n_visits
    straight = jnp.logical_and(jnp.logical_and(first_visit, last_visit), t >= 1)
    g_cur = lax.rem(t, 2)
    y_cur = lax.rem(t, 3)
    y_prev = lax.rem(t + 2, 3)
    rows = tile * slabs
    out_slabs = slabs // 2

    def start_gather(idx_ref, g):
        for r in range(tile):
            src = pl.multiple_of(idx_ref[0, 0, r], slabs)
            pltpu.make_async_copy(h_hbm.at[pl.ds(src, slabs), :], gbuf.at[g, pl.ds(r * slabs, slabs), :],
                                  gsem.at[g]).start()

    def wait_gather(g):
        pltpu.make_async_copy(h_hbm.at[pl.ds(0, rows), :], gbuf.at[g], gsem.at[g]).wait()

    def start_scatter(idx_ref, yb):
        for r in range(tile):
            dst = pl.multiple_of(idx_ref[0, 0, r], out_slabs)
            pltpu.make_async_copy(ybuf.at[yb, pl.ds(r * out_slabs, out_slabs), :],
                                  y_hbm.at[pl.ds(dst, out_slabs), :], ssem.at[yb]).start()

    def wait_scatter(yb):
        pltpu.make_async_copy(ybuf.at[yb], y_hbm.at[pl.ds(0, tile * out_slabs), :], ssem.at[yb]).wait()

    def begin_tile():
        start_gather(tokn_ref, 1 - g_cur)
        for s in range(slabs):
            xs[:, s * LANES:(s + 1) * LANES] = gbuf[g_cur, pl.ds(s, tile, stride=slabs), :].astype(BF16)

    def ffn():
        x = xs[...]
        gl = lax.dot_general(x, wg_ref[...], _NT, preferred_element_type=F32) + bg_ref[...]
        ln = lax.dot_general(x, wl_ref[...], _NT, preferred_element_type=F32) + bl_ref[...]
        x_glu = jnp.minimum(gl, SWIGLU_LIMIT)
        x_lin = jnp.clip(ln, -SWIGLU_LIMIT, SWIGLU_LIMIT)
        act = (0.5 * x_glu) * (1.0 + jnp.tanh((0.5 * SWIGLU_ALPHA) * x_glu)) * (x_lin + 1.0)
        return jnp.dot(act.astype(BF16), wd_bf[...], preferred_element_type=F32) + bd_ref[...]

    def stage(y_of_slab):
        for s in range(out_slabs):
            lo = lax.bitcast_convert_type(y_of_slab(s).astype(jnp.bfloat16).astype(F32), U32)
            hi = lax.bitcast_convert_type(y_of_slab(s + out_slabs).astype(jnp.bfloat16).astype(F32), U32)
            ybuf[y_cur, pl.ds(s, tile, stride=out_slabs), :] = hi | (lo >> 16)

    @pl.when(v == 0)
    def _():
        start_gather(tok0_ref, 0)

    @pl.when(jnp.logical_and(valid, new_expert))
    def _():
        wd_bf[...] = wd_ref[...].astype(BF16)

    @pl.when(jnp.logical_and(valid, first_visit))
    def _():
        wait_gather(g_cur)

        @pl.when(t >= 3)
        def _():
            wait_scatter(y_cur)

    @pl.when(jnp.logical_and(valid, straight))
    def _():
        start_scatter(slotp_ref, y_prev)
        begin_tile()
        y = ffn()
        stage(lambda s: y[:, s * LANES:(s + 1) * LANES])

    @pl.when(jnp.logical_and(valid, jnp.logical_not(straight)))
    def _():
        @pl.when(first_visit)
        def _():
            @pl.when(t >= 1)
            def _():
                start_scatter(slotp_ref, y_prev)

            begin_tile()

        y = ffn()

        @pl.when(first_visit)
        def _():
            yacc[...] = y

        @pl.when(jnp.logical_not(first_visit))
        def _():
            row = lax.broadcasted_iota(I32, (tile, 1), 0)
            mine = jnp.logical_and(row >= lo_ref[v], row < hi_ref[v])
            yacc[...] = jnp.where(mine, y, yacc[...])

        @pl.when(last_visit)
        def _():
            stage(lambda s: yacc[:, s * LANES:(s + 1) * LANES])

    @pl.when(v == n_visits - 1)
    def _():
        start_scatter(slot_ref, y_cur)
        wait_gather(1 - g_cur)
        wait_scatter(y_cur)

        @pl.when(t >= 1)
        def _():
            wait_scatter(y_prev)

        @pl.when(t >= 2)
        def _():
            wait_scatter(lax.rem(t + 1, 3))


def _experts(h_slab, plan, layer, wg, wl, wd, bg, bl, bd, n_tok):
    row_token, row_slot, t_of, e_of, lo, hi, n_visits = plan
    n_tiles, _, tile = row_token.shape
    _, n_exp, f, d = wg.shape
    slabs = d // LANES
    max_visits = t_of.shape[0]

    def expert_block(v, tl, ex, lo_, hi_, nv):
        return (layer, ex[v], 0, 0)

    body = functools.partial(_expert_body, tile=tile, slabs=slabs)
    smem_tile = functools.partial(pl.BlockSpec, (1, 1, tile), memory_space=pltpu.SMEM)
    grid_spec = pltpu.PrefetchScalarGridSpec(
        num_scalar_prefetch=5,
        grid=(max_visits,),
        in_specs=[
            smem_tile(lambda v, tl, ex, lo_, hi_, nv: (jnp.minimum(tl[v] + 1, n_tiles - 1), 0, 0)),
            smem_tile(lambda v, tl, ex, lo_, hi_, nv: (jnp.maximum(tl[v] - 1, 0), 0, 0)),
            smem_tile(lambda v, tl, ex, lo_, hi_, nv: (tl[v], 0, 0)),
            smem_tile(lambda v, tl, ex, lo_, hi_, nv: (0, 0, 0)),
            pl.BlockSpec(memory_space=pl.ANY),
            pl.BlockSpec((None, None, f, d), expert_block),
            pl.BlockSpec((None, None, f, d), expert_block),
            pl.BlockSpec((None, None, f, d), expert_block),
            pl.BlockSpec((None, None, 1, f), expert_block),
            pl.BlockSpec((None, None, 1, f), expert_block),
            pl.BlockSpec((None, None, 1, d), expert_block),
        ],
        out_specs=pl.BlockSpec(memory_space=pl.ANY),
        scratch_shapes=[
            pltpu.VMEM((2, tile * slabs, LANES), F32),
            pltpu.VMEM((tile, d), BF16),
            pltpu.VMEM((tile, d), F32),
            pltpu.VMEM((3, tile * slabs // 2, LANES), U32),
            pltpu.VMEM((f, d), BF16),
            pltpu.SemaphoreType.DMA((2,)),
            pltpu.SemaphoreType.DMA((3,)),
        ],
    )
    return pl.pallas_call(
        body,
        grid_spec=grid_spec,
        out_shape=jax.ShapeDtypeStruct((TOP_K * n_tok * slabs // 2, LANES), U32),
        compiler_params=_params("arbitrary"),
        name="moe_experts",
    )(t_of, e_of, lo, hi, n_visits, row_token, row_slot, row_slot, row_token, h_slab, wg, wl, wd, bg, bl, bd)


def _combine_body(x_ref, g2_ref, gate_ref, *refs, tm, slabs, final):
    y_refs = refs[:TOP_K]
    out_ref = refs[-1]
    gates = gate_ref[...]
    g2 = g2_ref[0]
    half = slabs // 2
    for s in range(half):
        acc_lo = acc_hi = None
        for k in range(TOP_K):
            w = y_refs[k][pl.ds(s, tm, stride=half), :]
            lo = gates[:, k:k + 1] * lax.bitcast_convert_type(w << 16, F32)
            hi = gates[:, k:k + 1] * lax.bitcast_convert_type(w & jnp.uint32(0xFFFF0000), F32)
            acc_lo = lo if acc_lo is None else acc_lo + lo
            acc_hi = hi if acc_hi is None else acc_hi + hi
        for acc, slab in ((acc_lo, s), (acc_hi, s + half)):
            cols = slice(slab * LANES, (slab + 1) * LANES)
            out_ref[:, cols] = x_ref[:, cols] + g2[:, cols] * acc
    if final:
        x = out_ref[...]
        ms = jnp.mean(x * x, axis=-1, keepdims=True)
        out_ref[...] = x * lax.rsqrt(ms + EPS) * refs[TOP_K][...]


def _combine(xa, y_slab, gates_t, mod, r, layer, n_rows, n_lat_rows, n_seq, b, final_g=None):
    d = xa.shape[1]
    tm = ROW_TILE
    slabs = d // LANES
    n_tiles = n_rows // tm
    final = final_g is not None
    body = functools.partial(_combine_body, tm=tm, slabs=slabs, final=final)
    y_specs = [pl.BlockSpec((tm * slabs // 2, LANES), functools.partial(lambda i, k: (k * n_tiles + i, 0), k=k))
               for k in range(TOP_K)]
    extra_specs = [pl.BlockSpec((1, d), lambda i: (0, 0))] if final else []
    extra_args = [final_g[None, :]] if final else []
    return pl.pallas_call(
        body,
        grid=(n_tiles,),
        in_specs=[
            pl.BlockSpec((tm, d), lambda i: (i, 0)),
            _mod_spec(layer, 5, r, d, n_lat_rows // tm, tm, n_seq, b),
            pl.BlockSpec((tm, TOP_K), lambda i: (i, 0)),
        ] + y_specs + extra_specs,
        out_specs=pl.BlockSpec((tm, d), lambda i: (i, 0)),
        out_shape=jax.ShapeDtypeStruct((n_rows, d), F32),
        compiler_params=_params("parallel"),
        name="moe_combine",
    )(xa, mod, gates_t, *([y_slab] * TOP_K), *extra_args)


def _wsplit_body(w_ref, wg_ref, wl_ref, ts):
    d, f2 = w_ref.shape
    half = LANES // 2
    for c in range(f2 // LANES):
        for j in range(d // LANES):
            blk = (c * (d // LANES) + j) * LANES
            ts[pl.ds(blk, LANES), :] = w_ref[j * LANES:(j + 1) * LANES, c * LANES:(c + 1) * LANES].T
            rows = slice(c * half, (c + 1) * half)
            cols = slice(j * LANES, (j + 1) * LANES)
            wg_ref[rows, cols] = ts[pl.ds(blk, half, stride=2), :].astype(BF16)
            wl_ref[rows, cols] = ts[pl.ds(blk + 1, half, stride=2), :].astype(BF16)


def _split_gate_up(w_gu):
    depth, n_exp, d, f2 = w_gu.shape
    f = f2 // 2
    out = jax.ShapeDtypeStruct((depth, n_exp, f, d), BF16)
    return pl.pallas_call(
        _wsplit_body,
        grid=(depth, n_exp),
        in_specs=[pl.BlockSpec((None, None, d, f2), lambda l, e: (l, e, 0, 0))],
        out_specs=[pl.BlockSpec((None, None, f, d), lambda l, e: (l, e, 0, 0))] * 2,
        out_shape=[out, out],
        scratch_shapes=[pltpu.VMEM((d * f2 // LANES, LANES), F32)],
        compiler_params=_params("parallel", "parallel"),
        name="moe_split_gate_up",
    )(w_gu)


def _moe(xa, mod, r, layer, g, router_w, router_b, expert_w, n_rows, n_lat_rows, n_seq, b, final_g=None):
    h_slab, top_idx, gates, rank, counts = _router(xa, mod, r, layer, g, router_w, router_b, n_rows, n_lat_rows,
                                                   n_seq, b)
    slabs = xa.shape[1] // LANES
    plan = _dispatch_plan(top_idx, rank, counts, MOE_TILE, slabs, slabs // 2)
    y_slab = _experts(h_slab, plan, layer, *expert_w, n_rows)
    return _combine(xa, y_slab, gates.T, mod, r, layer, n_rows, n_lat_rows, n_seq, b, final_g)


def kernel(x, c, ctx, c_ctx, mod_w, mod_b, norm_g, attn_w_qkv, attn_w_o, attn_lambda, attn_subln_g, conv_w_in, conv_w, conv_w_out, router_w, router_b, moe_w_gu, moe_b_gu, moe_w_down, moe_b_down, final_g):
    b, n_seq, d = x.shape
    n_ctx = ctx.shape[1]
    depth = mod_w.shape[0]
    n_lat_rows = b * n_seq
    n_all_rows = n_lat_rows + b * n_ctx
    assert d % LANES == 0 and d == moe_w_down.shape[-1]
    assert n_seq % ROW_TILE == 0 and (b * n_ctx) % ROW_TILE == 0 and n_seq % GRID_W == 0
    assert ROW_TILE % n_ctx == 0 or n_ctx % ROW_TILE == 0
    assert n_lat_rows % n_ctx == 0 and n_ctx % BF16_SUBLANES == 0
    assert (TOP_K * n_lat_rows) % MOE_TILE == 0 and (TOP_K * n_all_rows) % MOE_TILE == 0

    xa = jnp.concatenate([x.reshape(n_lat_rows, d), ctx.reshape(b * n_ctx, d)], axis=0)
    mod, r = _modulation(c, c_ctx, mod_w, mod_b)
    wg, wl = _split_gate_up(moe_w_gu)
    expert_w = (wg, wl, moe_w_down, moe_b_gu[:, :, None, 0::2], moe_b_gu[:, :, None, 1::2],
                moe_b_down[:, :, None, :])
    for i in range(depth):
        update_ctx = i < depth - 1
        attn_layer = i % N_MIXERS == 0
        j = i // N_MIXERS
        n_out = n_all_rows if update_ctx else n_lat_rows
        if attn_layer:
            lam_init = 0.8 - 0.6 * math.exp(-0.3 * i)
            qkv = _qkv_proj(xa, mod, r, i, norm_g[i, 0], attn_w_qkv[j], n_all_rows, n_lat_rows, n_seq, b)
            o_lat, o_ctx = _attention(qkv, attn_lambda[j], attn_subln_g[j], lam_init, b, n_seq, n_ctx, n_lat_rows,
                                      update_ctx)
            xa = _out_proj(o_lat, o_ctx, attn_w_o[j], xa, mod, r, i, n_lat_rows, n_seq, b)
        else:
            gb, u = _conv_proj(xa, mod, r, i, norm_g[i, 0], conv_w_in[j], n_out, n_lat_rows, n_seq, b)
            xa = _conv_out(u, gb, conv_w[j], conv_w_out[j], xa, mod, r, i, n_lat_rows, n_seq, n_ctx, b)
        xa = _moe(xa, mod, r, i, norm_g[i, 1], router_w[i], router_b[i], expert_w, n_out, n_lat_rows, n_seq, b,
                  final_g if i == depth - 1 else None)
    return xa.reshape(b, n_seq, d)
```

```python
import functools
import math

import jax
import jax.numpy as jnp
from jax import lax
from jax.experimental import pallas as pl
from jax.experimental.pallas import tpu as pltpu

F32 = jnp.float32
BF16 = jnp.bfloat16
I32 = jnp.int32
U32 = jnp.uint32

GRID_W = 64
HEAD_DIM = 64
ROPE_BASE = 10000.0
N_MIXERS = 2
TOP_K = 4
SWIGLU_LIMIT = 7.0
SWIGLU_ALPHA = 1.702
N_MOD = 6
EPS = 1e-5

LANES = 128
SUBLANES = 8
BF16_SUBLANES = 16
VMEM_LIMIT_BYTES = 48 * 1024 * 1024

ROW_TILE = 512
FUSED_ROW_TILE = 256
ATTN_Q_TILE = 512
MOE_TILE = 256

_NT = (((1,), (1,)), ((), ()))


def _round_up(a, b):
    return (a + b - 1) // b * b


def _params(*sem):
    return pltpu.CompilerParams(dimension_semantics=sem, vmem_limit_bytes=VMEM_LIMIT_BYTES)


def _pack_bf16_pair(lo, hi):
    lo_bits = lax.bitcast_convert_type(lo.astype(jnp.bfloat16).astype(F32), U32)
    hi_bits = lax.bitcast_convert_type(hi.astype(jnp.bfloat16).astype(F32), U32)
    return hi_bits | (lo_bits >> 16)


def _unpack_bf16_pair(w):
    return lax.bitcast_convert_type(w << 16, F32), lax.bitcast_convert_type(w & jnp.uint32(0xFFFF0000), F32)


def _norm_mod(x, g, shift, scale):
    ms = jnp.mean(x * x, axis=-1, keepdims=True)
    return (x * lax.rsqrt(ms + EPS) * g) * (1.0 + scale) + shift


def _mod_body(c_ref, w_ref, b_ref, o_ref):
    c = c_ref[...]
    s = c * jax.nn.sigmoid(c)
    o_ref[0] = jnp.dot(s, w_ref[0], preferred_element_type=F32, precision=lax.Precision.HIGHEST) + b_ref[0]


def _modulation(c, c_ctx, mod_w, mod_b):
    depth, d, nd = mod_w.shape
    b = c.shape[0]
    r = _round_up(b + 1, SUBLANES)
    cv = jnp.concatenate([c, c_ctx[None, :], jnp.zeros((r - b - 1, d), F32)], axis=0)
    tn = d
    out = pl.pallas_call(
        _mod_body,
        grid=(depth, nd // tn),
        in_specs=[
            pl.BlockSpec((r, d), lambda l, j: (0, 0)),
            pl.BlockSpec((1, d, tn), lambda l, j: (l, 0, j)),
            pl.BlockSpec((1, 1, tn), lambda l, j: (l, 0, j)),
        ],
        out_specs=pl.BlockSpec((1, r, tn), lambda l, j: (l, 0, j)),
        out_shape=jax.ShapeDtypeStruct((depth, r, nd), F32),
        compiler_params=_params("parallel", "parallel"),
        name="modulation",
    )(cv, mod_w, mod_b.reshape(depth, 1, nd))
    return out.reshape(depth * r, 1, nd), r


def _mod_spec(layer, chunk, r, d, n_lat_tiles, tm, n_seq, b):
    def index(i, *_):
        row = jnp.where(i < n_lat_tiles, (i * tm) // n_seq, b)
        return (layer * r + row, 0, chunk)

    return pl.BlockSpec((1, 1, d), index)


def _rope_tables(n, tm):
    rows = n // GRID_W
    row_pos = jnp.repeat(jnp.arange(rows, dtype=F32), GRID_W)
    col_pos = jnp.tile(jnp.arange(GRID_W, dtype=F32), rows)
    axis_dims = HEAD_DIM // 2
    inv_freq = ROPE_BASE ** (-jnp.arange(0, axis_dims, 2, dtype=F32) / axis_dims)
    ang = jnp.concatenate([row_pos[:, None] * inv_freq, col_pos[:, None] * inv_freq], axis=-1)
    cos, sin = jnp.cos(ang), jnp.sin(ang)
    reps = LANES // (HEAD_DIM // 2)
    cos_t = jnp.tile(cos, (1, reps))
    sign = jnp.where((jnp.arange(LANES) % HEAD_DIM) < HEAD_DIM // 2, -1.0, 1.0).astype(F32)
    sin_t = jnp.tile(sin, (1, reps)) * sign
    cos_t = jnp.concatenate([cos_t, jnp.ones((tm, LANES), F32)], axis=0)
    sin_t = jnp.concatenate([sin_t, jnp.zeros((tm, LANES), F32)], axis=0)
    return cos_t, sin_t


def _qkv_rows(x, sh_ref, sc_ref, g_ref, w_ref, cos_ref, sin_ref, o_ref, d):
    h = _norm_mod(x, g_ref[...], sh_ref[0], sc_ref[0])
    acc = jnp.dot(h.astype(BF16), w_ref[...], preferred_element_type=F32)
    cos = cos_ref[...]
    sin = sin_ref[...]
    lane = lax.broadcasted_iota(I32, (1, LANES), 1)
    first_half = (lane % HEAD_DIM) < HEAD_DIM // 2
    q_scale = HEAD_DIM**-0.5 * math.log2(math.e)
    for gi in range(2 * d // LANES):
        xg = acc[:, gi * LANES:(gi + 1) * LANES]
        partner = jnp.where(first_half, pltpu.roll(xg, LANES - HEAD_DIM // 2, 1), pltpu.roll(xg, HEAD_DIM // 2, 1))
        r = xg * cos + partner * sin
        if gi < d // LANES:
            r = r * q_scale
        o_ref[:, gi * LANES:(gi + 1) * LANES] = r.astype(BF16)
    o_ref[:, 2 * d:] = acc[:, 2 * d:].astype(BF16)


def _qkv_body(x_ref, sh_ref, sc_ref, g_ref, w_ref, cos_ref, sin_ref, o_ref, *, d):
    _qkv_rows(x_ref[...], sh_ref, sc_ref, g_ref, w_ref, cos_ref, sin_ref, o_ref, d)


def _qkv_proj(xa, mod, r, layer, g, w, n_rows, n_lat_rows, n_seq, b):
    d = xa.shape[1]
    tm = ROW_TILE
    n_lat_tiles = n_lat_rows // tm
    tiles_per_seq = n_seq // tm
    cos_t, sin_t = _rope_tables(n_seq, tm)

    def rope_index(i):
        return (jnp.where(i < n_lat_tiles, i % tiles_per_seq, tiles_per_seq), 0)

    return pl.pallas_call(
        functools.partial(_qkv_body, d=d),
        grid=(n_rows // tm,),
        in_specs=[
            pl.BlockSpec((tm, d), lambda i: (i, 0)),
            _mod_spec(layer, 0, r, d, n_lat_tiles, tm, n_seq, b),
            _mod_spec(layer, 1, r, d, n_lat_tiles, tm, n_seq, b),
            pl.BlockSpec((1, d), lambda i: (0, 0)),
            pl.BlockSpec((d, 3 * d), lambda i: (0, 0)),
            pl.BlockSpec((tm, LANES), rope_index),
            pl.BlockSpec((tm, LANES), rope_index),
        ],
        out_specs=pl.BlockSpec((tm, 3 * d), lambda i: (i, 0)),
        out_shape=jax.ShapeDtypeStruct((n_rows, 3 * d), BF16),
        compiler_params=_params("parallel"),
        name="qkv_proj",
    )(xa, mod, mod, g[None, :], w.astype(BF16), cos_t, sin_t)


def _diff_attention(lam_ref, sg_ref, q_ref, k_refs, va_refs, o_ref, lam_init):
    q = q_ref[...]
    lane = lax.broadcasted_iota(I32, (1, LANES), 1)
    lp = lam_ref[...]
    lam = (jnp.exp(jnp.sum(lp[0:1] * lp[1:2], axis=-1, keepdims=True))
           - jnp.exp(jnp.sum(lp[2:3] * lp[3:4], axis=-1, keepdims=True)) + lam_init)
    scores = []
    for comp in range(2):
        in_comp = (lane < HEAD_DIM) if comp == 0 else (lane >= HEAD_DIM)
        qc = jnp.where(in_comp, q, jnp.zeros_like(q))
        scores.append([lax.dot_general(qc, k[...], _NT, preferred_element_type=F32) for k in k_refs])
    outs = []
    for ss in scores:
        m = functools.reduce(jnp.maximum, [jnp.max(s, axis=-1, keepdims=True) for s in ss])
        acc = None
        for s, va in zip(ss, va_refs):
            part = jnp.dot(jnp.exp2(s - m).astype(BF16), va[...], preferred_element_type=F32)
            acc = part if acc is None else acc + part
        outs.append(acc[:, :LANES] / acc[:, LANES:])
    o = outs[0] - lam * outs[1]
    ms = jnp.mean(o * o, axis=-1, keepdims=True)
    o_ref[...] = (o * lax.rsqrt(ms + EPS) * sg_ref[...] * (1.0 - lam_init)).astype(BF16)


def _fill_value_ones(va, v_ref):
    va[:, :LANES] = v_ref[...]
    va[:, LANES:] = jnp.ones((va.shape[0], LANES), BF16)


def _attn_lat_body(lam_ref, sg_ref, q_ref, kc_ref, kl_ref, vc_ref, vl_ref, o_ref, vac, val, *, lam_init):
    @pl.when(pl.program_id(2) == 0)
    def _():
        _fill_value_ones(vac, vc_ref)
        _fill_value_ones(val, vl_ref)

    _diff_attention(lam_ref, sg_ref, q_ref, [kc_ref, kl_ref], [vac, val], o_ref, lam_init)


def _attn_ctx_body(lam_ref, sg_ref, q_ref, kc_ref, vc_ref, o_ref, vac, *, lam_init):
    _fill_value_ones(vac, vc_ref)
    _diff_attention(lam_ref, sg_ref, q_ref, [kc_ref], [vac], o_ref, lam_init)


def _attention(qkv, lam_p, subln_g, lam_init, b, n_seq, n_ctx, n_lat_rows, update_ctx):
    d = qkv.shape[1] // 3
    heads = d // LANES
    tq = ATTN_Q_TILE
    assert n_seq % tq == 0
    q_tiles = n_seq // tq
    ctx_blk0 = n_lat_rows // n_ctx
    small = [pl.BlockSpec((4, HEAD_DIM), lambda *_: (0, 0)), pl.BlockSpec((1, LANES), lambda *_: (0, 0))]
    o_lat = pl.pallas_call(
        functools.partial(_attn_lat_body, lam_init=lam_init),
        grid=(b, heads, q_tiles),
        in_specs=small + [
            pl.BlockSpec((tq, LANES), lambda bi, h, i: (bi * q_tiles + i, h)),
            pl.BlockSpec((n_ctx, LANES), lambda bi, h, i: (ctx_blk0 + bi, heads + h)),
            pl.BlockSpec((n_seq, LANES), lambda bi, h, i: (bi, heads + h)),
            pl.BlockSpec((n_ctx, LANES), lambda bi, h, i: (ctx_blk0 + bi, 2 * heads + h)),
            pl.BlockSpec((n_seq, LANES), lambda bi, h, i: (bi, 2 * heads + h)),
        ],
        out_specs=pl.BlockSpec((tq, LANES), lambda bi, h, i: (bi * q_tiles + i, h)),
        out_shape=jax.ShapeDtypeStruct((n_lat_rows, d), BF16),
        scratch_shapes=[pltpu.VMEM((n_ctx, 2 * LANES), BF16), pltpu.VMEM((n_seq, 2 * LANES), BF16)],
        compiler_params=_params("parallel", "parallel", "arbitrary"),
        name="attn_latent",
    )(lam_p, subln_g[None, :], qkv, qkv, qkv, qkv, qkv)
    if not update_ctx:
        return o_lat, None
    o_ctx = pl.pallas_call(
        functools.partial(_attn_ctx_body, lam_init=lam_init),
        grid=(b, heads),
        in_specs=small + [
            pl.BlockSpec((n_ctx, LANES), lambda bi, h: (ctx_blk0 + bi, h)),
            pl.BlockSpec((n_ctx, LANES), lambda bi, h: (ctx_blk0 + bi, heads + h)),
            pl.BlockSpec((n_ctx, LANES), lambda bi, h: (ctx_blk0 + bi, 2 * heads + h)),
        ],
        out_specs=pl.BlockSpec((n_ctx, LANES), lambda bi, h: (bi, h)),
        out_shape=jax.ShapeDtypeStruct((b * n_ctx, d), BF16),
        scratch_shapes=[pltpu.VMEM((n_ctx, 2 * LANES), BF16)],
        compiler_params=_params("parallel", "parallel"),
        name="attn_context",
    )(lam_p, subln_g[None, :], qkv, qkv, qkv)
    return o_lat, o_ctx


def _oproj_body(ol_ref, oc_ref, w_ref, x_ref, g1_ref, out_ref, *, n_lat_tiles):
    def finish(o_ref):
        y = jnp.dot(o_ref[...], w_ref[...], preferred_element_type=F32)
        out_ref[...] = x_ref[...] + g1_ref[0] * y

    @pl.when(pl.program_id(0) < n_lat_tiles)
    def _():
        finish(ol_ref)

    @pl.when(pl.program_id(0) >= n_lat_tiles)
    def _():
        finish(oc_ref)


def _out_proj(o_lat, o_ctx, w, xa, mod, r, layer, n_lat_rows, n_seq, b):
    d = o_lat.shape[1]
    tm = ROW_TILE
    n_lat_tiles = n_lat_rows // tm
    if o_ctx is None:
        o_ctx = o_lat
    n_rows = n_lat_rows + (0 if o_ctx is o_lat else o_ctx.shape[0])
    return pl.pallas_call(
        functools.partial(_oproj_body, n_lat_tiles=n_lat_tiles),
        grid=(n_rows // tm,),
        in_specs=[
            pl.BlockSpec((tm, d), lambda i: (jnp.minimum(i, n_lat_tiles - 1), 0)),
            pl.BlockSpec((tm, d), lambda i: (jnp.maximum(i - n_lat_tiles, 0), 0)),
            pl.BlockSpec((d, d), lambda i: (0, 0)),
            pl.BlockSpec((tm, d), lambda i: (i, 0)),
            _mod_spec(layer, 2, r, d, n_lat_tiles, tm, n_seq, b),
        ],
        out_specs=pl.BlockSpec((tm, d), lambda i: (i, 0)),
        out_shape=jax.ShapeDtypeStruct((n_rows, d), F32),
        compiler_params=_params("parallel"),
        name="attn_out_proj",
    )(o_lat, o_ctx, w.astype(BF16), xa, mod)


def _cproj_rows(x, sh_ref, sc_ref, g_ref, w_ref, gb_ref, u_ref, d):
    h = _norm_mod(x, g_ref[...], sh_ref[0], sc_ref[0])
    acc = jnp.dot(h.astype(BF16), w_ref[...], preferred_element_type=F32)
    gb_ref[...] = acc[:, :d].astype(BF16)
    u_ref[...] = (acc[:, d:2 * d] * acc[:, 2 * d:]).astype(BF16)


def _cproj_body(x_ref, sh_ref, sc_ref, g_ref, w_ref, gb_ref, u_ref, *, d):
    _cproj_rows(x_ref[...], sh_ref, sc_ref, g_ref, w_ref, gb_ref, u_ref, d)


def _conv_proj(xa, mod, r, layer, g, w, n_rows, n_lat_rows, n_seq, b):
    d = xa.shape[1]
    tm = ROW_TILE
    n_lat_tiles = n_lat_rows // tm
    return pl.pallas_call(
        functools.partial(_cproj_body, d=d),
        grid=(n_rows // tm,),
        in_specs=[
            pl.BlockSpec((tm, d), lambda i: (i, 0)),
            _mod_spec(layer, 0, r, d, n_lat_tiles, tm, n_seq, b),
            _mod_spec(layer, 1, r, d, n_lat_tiles, tm, n_seq, b),
            pl.BlockSpec((1, d), lambda i: (0, 0)),
            pl.BlockSpec((d, 3 * d), lambda i: (0, 0)),
        ],
        out_specs=[pl.BlockSpec((tm, d), lambda i: (i, 0)), pl.BlockSpec((tm, d), lambda i: (i, 0))],
        out_shape=[jax.ShapeDtypeStruct((n_rows, d), BF16), jax.ShapeDtypeStruct((n_rows, d), BF16)],
        compiler_params=_params("parallel"),
        name="conv_in_proj",
    )(xa, mod, mod, g[None, :], w.astype(BF16))


def _cout_body(u_ref, up_ref, un_ref, gb_ref, cw_ref, w_ref, x_ref, g1_ref, out_ref, *, tm, n_lat_rows, n_seq, n_ctx):
    i = pl.program_id(0)
    u = u_ref[...].astype(F32)
    prev_last = up_ref[BF16_SUBLANES - 1:BF16_SUBLANES, :].astype(F32)
    next_first = un_ref[0:1, :].astype(F32)
    row = lax.broadcasted_iota(I32, (tm, 1), 0)
    grow = i * tm + row
    is_lat = grow < n_lat_rows
    pos = jnp.where(is_lat, lax.rem(grow, n_seq), lax.rem(jnp.maximum(grow - n_lat_rows, 0), n_ctx))
    seq_len = jnp.where(is_lat, n_seq, n_ctx)
    um = jnp.where(row == 0, prev_last, pltpu.roll(u, 1, 0))
    um = jnp.where(pos == 0, 0.0, um)
    un = jnp.where(row == tm - 1, next_first, pltpu.roll(u, tm - 1, 0))
    un = jnp.where(pos == seq_len - 1, 0.0, un)
    cw = cw_ref[...]
    conv = um * cw[0:1] + u * cw[1:2] + un * cw[2:3]
    z = gb_ref[...].astype(F32) * conv
    y = jnp.dot(z.astype(BF16), w_ref[...], preferred_element_type=F32)
    out_ref[...] = x_ref[...] + g1_ref[0] * y


def _conv_out(u, gb, conv_w, w, xa, mod, r, layer, n_lat_rows, n_seq, n_ctx, b):
    n_rows, d = u.shape
    tm = ROW_TILE
    halo = BF16_SUBLANES
    n_halo_blocks = n_rows // halo
    body = functools.partial(_cout_body, tm=tm, n_lat_rows=n_lat_rows, n_seq=n_seq, n_ctx=n_ctx)
    return pl.pallas_call(
        body,
        grid=(n_rows // tm,),
        in_specs=[
            pl.BlockSpec((tm, d), lambda i: (i, 0)),
            pl.BlockSpec((halo, d), lambda i: (jnp.maximum(i * (tm // halo) - 1, 0), 0)),
            pl.BlockSpec((halo, d), lambda i: (jnp.minimum((i + 1) * (tm // halo), n_halo_blocks - 1), 0)),
            pl.BlockSpec((tm, d), lambda i: (i, 0)),
            pl.BlockSpec((3, d), lambda i: (0, 0)),
            pl.BlockSpec((d, d), lambda i: (0, 0)),
            pl.BlockSpec((tm, d), lambda i: (i, 0)),
            _mod_spec(layer, 2, r, d, n_lat_rows // tm, tm, n_seq, b),
        ],
        out_specs=pl.BlockSpec((tm, d), lambda i: (i, 0)),
        out_shape=jax.ShapeDtypeStruct((n_rows, d), F32),
        compiler_params=_params("parallel"),
        name="conv_out_proj",
    )(u, u, u, gb, conv_w, w.astype(BF16), xa, mod)


def _router_body(x_ref, sh_ref, sc_ref, g_ref, rw_hi_ref, rw_lo_ref, rb_ref, earlier_ref, h_ref, idx_ref, gate_ref,
                 rank_ref, count_ref, *, tm, d, n_exp):
    @pl.when(pl.program_id(0) == 0)
    def _():
        count_ref[...] = jnp.zeros_like(count_ref)

    h = _norm_mod(x_ref[...], g_ref[...], sh_ref[0], sc_ref[0])
    half = d // LANES // 2
    for s in range(half):
        h_ref[pl.ds(s, tm, stride=half), :] = _pack_bf16_pair(h[:, s * LANES:(s + 1) * LANES],
                                                              h[:, (s + half) * LANES:(s + half + 1) * LANES])
    h_hi = h.astype(BF16)
    h_lo = (h - h_hi.astype(F32)).astype(BF16)
    logits = (lax.dot_general(rw_hi_ref[...], h_hi, _NT, preferred_element_type=F32)
              + lax.dot_general(rw_hi_ref[...], h_lo, _NT, preferred_element_type=F32)
              + lax.dot_general(rw_lo_ref[...], h_hi, _NT, preferred_element_type=F32)) + rb_ref[...]
    eidx = lax.broadcasted_iota(I32, (n_exp, tm), 0)
    tops, sels = [], []
    for _ in range(TOP_K):
        m = jnp.max(logits, axis=0, keepdims=True)
        sel = jnp.min(jnp.where(logits == m, eidx, n_exp), axis=0, keepdims=True)
        tops.append(m)
        sels.append(sel)
        logits = jnp.where(eidx == sel, -jnp.inf, logits)
    es = [jnp.exp(m - tops[0]) for m in tops]
    den = functools.reduce(jnp.add, es)
    idx_ref[...] = jnp.concatenate(sels, axis=0)
    gate_ref[...] = jnp.concatenate([e / den for e in es], axis=0)
    onehots = [(eidx == sel).astype(F32) for sel in sels]
    prefix = jnp.dot(jnp.concatenate(onehots, axis=0).astype(BF16), earlier_ref[...], preferred_element_type=F32)
    offset = count_ref[...]
    ranks = []
    for k in range(TOP_K):
        within = prefix[k * n_exp:(k + 1) * n_exp] + offset
        ranks.append(jnp.sum(onehots[k] * within, axis=0, keepdims=True))
        offset = offset + jnp.sum(onehots[k], axis=1, keepdims=True)
    rank_ref[...] = jnp.concatenate(ranks, axis=0).astype(I32)
    count_ref[...] = offset


def _router(xa, mod, r, layer, g, router_w, router_b, n_rows, n_lat_rows, n_seq, b):
    d = xa.shape[1]
    n_exp = router_w.shape[1]
    tm = ROW_TILE
    slabs = d // LANES
    n_lat_tiles = n_lat_rows // tm
    body = functools.partial(_router_body, tm=tm, d=d, n_exp=n_exp)
    rw_t = router_w.T
    rw_hi = rw_t.astype(BF16)
    rw_lo = (rw_t - rw_hi.astype(F32)).astype(BF16)
    earlier = (jnp.arange(tm)[:, None] < jnp.arange(tm)[None, :]).astype(BF16)
    return pl.pallas_call(
        body,
        grid=(n_rows // tm,),
        in_specs=[
            pl.BlockSpec((tm, d), lambda i: (i, 0)),
            _mod_spec(layer, 3, r, d, n_lat_tiles, tm, n_seq, b),
            _mod_spec(layer, 4, r, d, n_lat_tiles, tm, n_seq, b),
            pl.BlockSpec((1, d), lambda i: (0, 0)),
            pl.BlockSpec((n_exp, d), lambda i: (0, 0)),
            pl.BlockSpec((n_exp, d), lambda i: (0, 0)),
            pl.BlockSpec((n_exp, 1), lambda i: (0, 0)),
            pl.BlockSpec((tm, tm), lambda i: (0, 0)),
        ],
        out_specs=[
            pl.BlockSpec((tm * slabs // 2, LANES), lambda i: (i, 0)),
            pl.BlockSpec((TOP_K, tm), lambda i: (0, i)),
            pl.BlockSpec((TOP_K, tm), lambda i: (0, i)),
            pl.BlockSpec((TOP_K, tm), lambda i: (0, i)),
            pl.BlockSpec((n_exp, 1), lambda i: (0, 0)),
        ],
        out_shape=[
            jax.ShapeDtypeStruct((n_rows * slabs // 2, LANES), U32),
            jax.ShapeDtypeStruct((TOP_K, n_rows), I32),
            jax.ShapeDtypeStruct((TOP_K, n_rows), F32),
            jax.ShapeDtypeStruct((TOP_K, n_rows), I32),
            jax.ShapeDtypeStruct((n_exp, 1), F32),
        ],
        compiler_params=_params("arbitrary"),
        name="moe_router",
    )(xa, mod, mod, g[None, :], rw_hi, rw_lo, router_b[:, None], earlier)


def _dispatch_plan(top_idx, rank, counts, tile, in_slabs, out_slabs):
    n_tok = top_idx.shape[1]
    n_exp = counts.shape[0]
    n_assign = TOP_K * n_tok
    n_tiles = n_assign // tile
    counts = counts[:, 0].astype(I32)
    end = jnp.cumsum(counts)
    start = end - counts
    experts = jnp.arange(n_exp, dtype=I32)
    pos = rank + jnp.sum(jnp.where(top_idx[:, :, None] == experts, start, 0), axis=-1)
    slot_of_row = jnp.zeros((n_assign,), I32).at[pos.reshape(-1)].add(jnp.arange(n_assign, dtype=I32))
    row_token = ((slot_of_row % n_tok) * in_slabs).reshape(n_tiles, 1, tile)
    row_slot = (slot_of_row * out_slabs).reshape(n_tiles, 1, tile)
    first_tile = start // tile
    visits = jnp.where(counts > 0, (end - 1) // tile - first_tile + 1, 0)
    visit_end = jnp.cumsum(visits)
    visit_start = visit_end - visits
    n_visits = visit_end[-1]
    max_visits = n_tiles + n_exp - 1
    v = jnp.arange(max_visits, dtype=I32)
    valid = v < n_visits
    e_of = jnp.minimum(jnp.sum((v[:, None] >= visit_end[None, :]).astype(I32), axis=1), n_exp - 1)
    t_of = first_tile[e_of] + v - visit_start[e_of]
    lo = jnp.maximum(start[e_of], t_of * tile) - t_of * tile
    hi = jnp.minimum(end[e_of], (t_of + 1) * tile) - t_of * tile
    last = jnp.maximum(n_visits - 1, 0)
    e_of = jnp.where(valid, e_of, e_of[last])
    t_of = jnp.where(valid, t_of, t_of[last])
    lo = jnp.where(valid, lo, 0)
    hi = jnp.where(valid, hi, 0)
    return row_token, row_slot, t_of.astype(I32), e_of, lo.astype(I32), hi.astype(I32), n_visits.reshape(1).astype(I32)


def _expert_body(tile_ref, exp_ref, lo_ref, hi_ref, nv_ref,
                 tokn_ref, slotp_ref, slot_ref, tok0_ref, h_hbm, wg_ref, wl_ref, wd_ref, bg_ref, bl1_ref, bd_ref,
                 y_hbm, gbuf, xs, yacc, ybuf, wd_bf, gsem, ssem, *, tile, slabs):
    v = pl.program_id(0)
    n_visits = nv_ref[0]
    t = tile_ref[v]
    new_expert = jnp.logical_or(v == 0, exp_ref[jnp.maximum(v - 1, 0)] != exp_ref[v])
    first_visit = jnp.logical_or(v == 0, tile_ref[jnp.maximum(v - 1, 0)] != t)
    last_visit = jnp.logical_or(v == n_visits - 1, tile_ref[jnp.minimum(v + 1, pl.num_programs(0) - 1)] != t)
    valid = v < n_visits
    straight = jnp.logical_and(jnp.logical_and(first_visit, last_visit), t >= 1)
    g_cur = lax.rem(t, 2)
    y_cur = lax.rem(t, 3)
    y_prev = lax.rem(t + 2, 3)
    half = slabs // 2
    rows = tile * half

    def start_gather(idx_ref, g):
        for r in range(tile):
            src = pl.multiple_of(idx_ref[0, 0, r], half)
            pltpu.make_async_copy(h_hbm.at[pl.ds(src, half), :], gbuf.at[g, pl.ds(r * half, half), :],
                                  gsem.at[g]).start()

    def wait_gather(g):
        pltpu.make_async_copy(h_hbm.at[pl.ds(0, rows), :], gbuf.at[g], gsem.at[g]).wait()

    def start_scatter(idx_ref, yb):
        for r in range(tile):
            dst = pl.multiple_of(idx_ref[0, 0, r], half)
            pltpu.make_async_copy(ybuf.at[yb, pl.ds(r * half, half), :], y_hbm.at[pl.ds(dst, half), :],
                                  ssem.at[yb]).start()

    def wait_scatter(yb):
        pltpu.make_async_copy(ybuf.at[yb], y_hbm.at[pl.ds(0, rows), :], ssem.at[yb]).wait()

    def begin_tile():
        start_gather(tokn_ref, 1 - g_cur)
        for s in range(half):
            lo, hi = _unpack_bf16_pair(gbuf[g_cur, pl.ds(s, tile, stride=half), :])
            xs[:, s * LANES:(s + 1) * LANES] = lo.astype(BF16)
            xs[:, (s + half) * LANES:(s + half + 1) * LANES] = hi.astype(BF16)

    def ffn():
        x = xs[...]
        gl = lax.dot_general(x, wg_ref[...], _NT, preferred_element_type=F32) + bg_ref[...]
        ln1 = lax.dot_general(x, wl_ref[...], _NT, preferred_element_type=F32) + bl1_ref[...]
        x_glu = jnp.minimum(gl, SWIGLU_LIMIT)
        x_lin1 = jnp.clip(ln1, 1.0 - SWIGLU_LIMIT, 1.0 + SWIGLU_LIMIT)
        act = x_glu * (1.0 + jnp.tanh((0.5 * SWIGLU_ALPHA) * x_glu)) * x_lin1
        return jnp.dot(act.astype(BF16), wd_bf[...], preferred_element_type=F32) + bd_ref[...]

    def stage(y_of_slab):
        for s in range(half):
            ybuf[y_cur, pl.ds(s, tile, stride=half), :] = _pack_bf16_pair(y_of_slab(s), y_of_slab(s + half))

    @pl.when(v == 0)
    def _():
        start_gather(tok0_ref, 0)

    @pl.when(jnp.logical_and(valid, new_expert))
    def _():
        wd_bf[...] = (0.5 * wd_ref[...]).astype(BF16)

    @pl.when(jnp.logical_and(valid, first_visit))
    def _():
        wait_gather(g_cur)

        @pl.when(t >= 3)
        def _():
            wait_scatter(y_cur)

    @pl.when(jnp.logical_and(valid, straight))
    def _():
        start_scatter(slotp_ref, y_prev)
        begin_tile()
        y = ffn()
        stage(lambda s: y[:, s * LANES:(s + 1) * LANES])

    @pl.when(jnp.logical_and(valid, jnp.logical_not(straight)))
    def _():
        @pl.when(first_visit)
        def _():
            @pl.when(t >= 1)
            def _():
                start_scatter(slotp_ref, y_prev)

            begin_tile()

        y = ffn()

        @pl.when(first_visit)
        def _():
            yacc[...] = y

        @pl.when(jnp.logical_not(first_visit))
        def _():
            row = lax.broadcasted_iota(I32, (tile, 1), 0)
            mine = jnp.logical_and(row >= lo_ref[v], row < hi_ref[v])
            yacc[...] = jnp.where(mine, y, yacc[...])

        @pl.when(last_visit)
        def _():
            stage(lambda s: yacc[:, s * LANES:(s + 1) * LANES])

    @pl.when(v == n_visits - 1)
    def _():
        start_scatter(slot_ref, y_cur)
        wait_gather(1 - g_cur)
        wait_scatter(y_cur)

        @pl.when(t >= 1)
        def _():
            wait_scatter(y_prev)

        @pl.when(t >= 2)
        def _():
            wait_scatter(lax.rem(t + 1, 3))


def _experts(h_slab, plan, layer, wg, wl, wd, bg, bl, bd, n_tok):
    row_token, row_slot, t_of, e_of, lo, hi, n_visits = plan
    n_tiles, _, tile = row_token.shape
    _, n_exp, f, d = wg.shape
    slabs = d // LANES
    max_visits = t_of.shape[0]

    def expert_block(v, tl, ex, lo_, hi_, nv):
        return (layer, ex[v], 0, 0)

    body = functools.partial(_expert_body, tile=tile, slabs=slabs)
    smem_tile = functools.partial(pl.BlockSpec, (1, 1, tile), memory_space=pltpu.SMEM)
    grid_spec = pltpu.PrefetchScalarGridSpec(
        num_scalar_prefetch=5,
        grid=(max_visits,),
        in_specs=[
            smem_tile(lambda v, tl, ex, lo_, hi_, nv: (jnp.minimum(tl[v] + 1, n_tiles - 1), 0, 0)),
            smem_tile(lambda v, tl, ex, lo_, hi_, nv: (jnp.maximum(tl[v] - 1, 0), 0, 0)),
            smem_tile(lambda v, tl, ex, lo_, hi_, nv: (tl[v], 0, 0)),
            smem_tile(lambda v, tl, ex, lo_, hi_, nv: (0, 0, 0)),
            pl.BlockSpec(memory_space=pl.ANY),
            pl.BlockSpec((None, None, f, d), expert_block),
            pl.BlockSpec((None, None, f, d), expert_block),
            pl.BlockSpec((None, None, f, d), expert_block),
            pl.BlockSpec((None, None, 1, f), expert_block),
            pl.BlockSpec((None, None, 1, f), expert_block),
            pl.BlockSpec((None, None, 1, d), expert_block),
        ],
        out_specs=pl.BlockSpec(memory_space=pl.ANY),
        scratch_shapes=[
            pltpu.VMEM((2, tile * slabs // 2, LANES), U32),
            pltpu.VMEM((tile, d), BF16),
            pltpu.VMEM((tile, d), F32),
            pltpu.VMEM((3, tile * slabs // 2, LANES), U32),
            pltpu.VMEM((f, d), BF16),
            pltpu.SemaphoreType.DMA((2,)),
            pltpu.SemaphoreType.DMA((3,)),
        ],
    )
    return pl.pallas_call(
        body,
        grid_spec=grid_spec,
        out_shape=jax.ShapeDtypeStruct((TOP_K * n_tok * slabs // 2, LANES), U32),
        compiler_params=_params("arbitrary"),
        name="moe_experts",
    )(t_of, e_of, lo, hi, n_visits, row_token, row_slot, row_slot, row_token, h_slab, wg, wl, wd, bg, bl, bd)


def _combine_rows(x_ref, g2_ref, gate_ref, y_refs, out_ref, tm, slabs):
    gates = gate_ref[...]
    g2 = g2_ref[0]
    half = slabs // 2
    for s in range(half):
        acc_lo = acc_hi = None
        for k in range(TOP_K):
            lo, hi = _unpack_bf16_pair(y_refs[k][pl.ds(s, tm, stride=half), :])
            acc_lo = gates[:, k:k + 1] * lo if acc_lo is None else acc_lo + gates[:, k:k + 1] * lo
            acc_hi = gates[:, k:k + 1] * hi if acc_hi is None else acc_hi + gates[:, k:k + 1] * hi
        for acc, slab in ((acc_lo, s), (acc_hi, s + half)):
            cols = slice(slab * LANES, (slab + 1) * LANES)
            out_ref[:, cols] = x_ref[:, cols] + g2[:, cols] * acc


def _combine_final_body(x_ref, g2_ref, gate_ref, *refs, tm, slabs):
    y_refs, fg_ref, out_ref = refs[:TOP_K], refs[TOP_K], refs[TOP_K + 1]
    _combine_rows(x_ref, g2_ref, gate_ref, y_refs, out_ref, tm, slabs)
    x = out_ref[...]
    ms = jnp.mean(x * x, axis=-1, keepdims=True)
    out_ref[...] = x * lax.rsqrt(ms + EPS) * fg_ref[...]


def _combine_qkv_body(x_ref, g2_ref, gate_ref, *refs, tm, slabs, d):
    y_refs = refs[:TOP_K]
    sh_ref, sc_ref, g_ref, w_ref, cos_ref, sin_ref, xo_ref, o_ref = refs[TOP_K:]
    _combine_rows(x_ref, g2_ref, gate_ref, y_refs, xo_ref, tm, slabs)
    _qkv_rows(xo_ref[...], sh_ref, sc_ref, g_ref, w_ref, cos_ref, sin_ref, o_ref, d)


def _combine_cproj_body(x_ref, g2_ref, gate_ref, *refs, tm, slabs, d):
    y_refs = refs[:TOP_K]
    sh_ref, sc_ref, g_ref, w_ref, xo_ref, gb_ref, u_ref = refs[TOP_K:]
    _combine_rows(x_ref, g2_ref, gate_ref, y_refs, xo_ref, tm, slabs)
    _cproj_rows(xo_ref[...], sh_ref, sc_ref, g_ref, w_ref, gb_ref, u_ref, d)


def _combine(xa, y_slab, gates_t, mod, r, layer, n_rows, n_moe_rows, n_lat_rows, n_seq, b, final_g=None, nxt=None):
    d = xa.shape[1]
    slabs = d // LANES
    tm = ROW_TILE if nxt is None else FUSED_ROW_TILE
    n_tiles = n_rows // tm
    n_lat_tiles = n_lat_rows // tm
    y_tiles_per_k = n_moe_rows // tm
    y_specs = [pl.BlockSpec((tm * slabs // 2, LANES), functools.partial(lambda i, k: (k * y_tiles_per_k + i, 0), k=k))
               for k in range(TOP_K)]
    row_spec = pl.BlockSpec((tm, d), lambda i: (i, 0))
    in_specs = [row_spec, _mod_spec(layer, 5, r, d, n_lat_tiles, tm, n_seq, b),
                pl.BlockSpec((tm, TOP_K), lambda i: (i, 0))] + y_specs
    args = [xa, mod, gates_t] + [y_slab] * TOP_K
    x_shape = jax.ShapeDtypeStruct((n_rows, d), F32)
    if nxt is None:
        return pl.pallas_call(
            functools.partial(_combine_final_body, tm=tm, slabs=slabs),
            grid=(n_tiles,),
            in_specs=in_specs + [pl.BlockSpec((1, d), lambda i: (0, 0))],
            out_specs=row_spec,
            out_shape=x_shape,
            compiler_params=_params("parallel"),
            name="moe_combine_final",
        )(*args, final_g[None, :])
    kind, g, w = nxt
    in_specs += [_mod_spec(layer + 1, 0, r, d, n_lat_tiles, tm, n_seq, b),
                 _mod_spec(layer + 1, 1, r, d, n_lat_tiles, tm, n_seq, b),
                 pl.BlockSpec((1, d), lambda i: (0, 0)),
                 pl.BlockSpec((d, 3 * d), lambda i: (0, 0))]
    args += [mod, mod, g[None, :], w.astype(BF16)]
    if kind == "attn":
        tiles_per_seq = n_seq // tm
        cos_t, sin_t = _rope_tables(n_seq, tm)

        def rope_index(i):
            return (jnp.where(i < n_lat_tiles, i % tiles_per_seq, tiles_per_seq), 0)

        return pl.pallas_call(
            functools.partial(_combine_qkv_body, tm=tm, slabs=slabs, d=d),
            grid=(n_tiles,),
            in_specs=in_specs + [pl.BlockSpec((tm, LANES), rope_index)] * 2,
            out_specs=[row_spec, pl.BlockSpec((tm, 3 * d), lambda i: (i, 0))],
            out_shape=[x_shape, jax.ShapeDtypeStruct((n_rows, 3 * d), BF16)],
            compiler_params=_params("parallel"),
            name="moe_combine_qkv_proj",
        )(*args, cos_t, sin_t)
    xa_new, gb, u = pl.pallas_call(
        functools.partial(_combine_cproj_body, tm=tm, slabs=slabs, d=d),
        grid=(n_tiles,),
        in_specs=in_specs,
        out_specs=[row_spec, row_spec, row_spec],
        out_shape=[x_shape, jax.ShapeDtypeStruct((n_rows, d), BF16), jax.ShapeDtypeStruct((n_rows, d), BF16)],
        compiler_params=_params("parallel"),
        name="moe_combine_conv_in_proj",
    )(*args)
    return xa_new, (gb, u)


def _wsplit_body(w_ref, wg_ref, wl_ref, ts):
    d, f2 = w_ref.shape
    half = LANES // 2
    for c in range(f2 // LANES):
        for j in range(d // LANES):
            blk = (c * (d // LANES) + j) * LANES
            ts[pl.ds(blk, LANES), :] = w_ref[j * LANES:(j + 1) * LANES, c * LANES:(c + 1) * LANES].T
            rows = slice(c * half, (c + 1) * half)
            cols = slice(j * LANES, (j + 1) * LANES)
            wg_ref[rows, cols] = ts[pl.ds(blk, half, stride=2), :].astype(BF16)
            wl_ref[rows, cols] = ts[pl.ds(blk + 1, half, stride=2), :].astype(BF16)


def _split_gate_up(w_gu):
    depth, n_exp, d, f2 = w_gu.shape
    f = f2 // 2
    out = jax.ShapeDtypeStruct((depth, n_exp, f, d), BF16)
    return pl.pallas_call(
        _wsplit_body,
        grid=(depth, n_exp),
        in_specs=[pl.BlockSpec((None, None, d, f2), lambda l, e: (l, e, 0, 0))],
        out_specs=[pl.BlockSpec((None, None, f, d), lambda l, e: (l, e, 0, 0))] * 2,
        out_shape=[out, out],
        scratch_shapes=[pltpu.VMEM((d * f2 // LANES, LANES), F32)],
        compiler_params=_params("parallel", "parallel"),
        name="moe_split_gate_up",
    )(w_gu)


def _moe_experts(xa, mod, r, layer, g, router_w, router_b, expert_w, n_rows, n_lat_rows, n_seq, b):
    h_slab, top_idx, gates, rank, counts = _router(xa, mod, r, layer, g, router_w, router_b, n_rows, n_lat_rows,
                                                   n_seq, b)
    slabs = xa.shape[1] // LANES
    plan = _dispatch_plan(top_idx, rank, counts, MOE_TILE, slabs // 2, slabs // 2)
    return _experts(h_slab, plan, layer, *expert_w, n_rows), gates.T


def kernel(x, c, ctx, c_ctx, mod_w, mod_b, norm_g, attn_w_qkv, attn_w_o, attn_lambda, attn_subln_g, conv_w_in, conv_w, conv_w_out, router_w, router_b, moe_w_gu, moe_b_gu, moe_w_down, moe_b_down, final_g):
    b, n_seq, d = x.shape
    n_ctx = ctx.shape[1]
    depth = mod_w.shape[0]
    n_lat_rows = b * n_seq
    n_all_rows = n_lat_rows + b * n_ctx
    assert d % LANES == 0 and d == moe_w_down.shape[-1]
    assert n_seq % ROW_TILE == 0 and (b * n_ctx) % ROW_TILE == 0 and n_seq % GRID_W == 0
    assert ROW_TILE % n_ctx == 0 or n_ctx % ROW_TILE == 0
    assert n_lat_rows % n_ctx == 0 and n_ctx % BF16_SUBLANES == 0
    assert (TOP_K * n_lat_rows) % MOE_TILE == 0 and (TOP_K * n_all_rows) % MOE_TILE == 0

    xa = jnp.concatenate([x.reshape(n_lat_rows, d), ctx.reshape(b * n_ctx, d)], axis=0)
    mod, r = _modulation(c, c_ctx, mod_w, mod_b)
    wg, wl = _split_gate_up(moe_w_gu)
    expert_w = (wg, wl, moe_w_down, moe_b_gu[:, :, None, 0::2], moe_b_gu[:, :, None, 1::2] + 1.0,
                moe_b_down[:, :, None, :])
    proj = None
    for i in range(depth):
        update_ctx = i < depth - 1
        attn_layer = i % N_MIXERS == 0
        j = i // N_MIXERS
        n_out = n_all_rows if update_ctx else n_lat_rows
        if attn_layer:
            lam_init = 0.8 - 0.6 * math.exp(-0.3 * i)
            qkv = proj if proj is not None else _qkv_proj(xa, mod, r, i, norm_g[i, 0], attn_w_qkv[j], n_all_rows,
                                                           n_lat_rows, n_seq, b)
            o_lat, o_ctx = _attention(qkv, attn_lambda[j], attn_subln_g[j], lam_init, b, n_seq, n_ctx, n_lat_rows,
                                      update_ctx)
            xa = _out_proj(o_lat, o_ctx, attn_w_o[j], xa, mod, r, i, n_lat_rows, n_seq, b)
        else:
            gb, u = proj if proj is not None else _conv_proj(xa, mod, r, i, norm_g[i, 0], conv_w_in[j], n_out,
                                                             n_lat_rows, n_seq, b)
            xa = _conv_out(u, gb, conv_w[j], conv_w_out[j], xa, mod, r, i, n_lat_rows, n_seq, n_ctx, b)
        y_slab, gates_t = _moe_experts(xa, mod, r, i, norm_g[i, 1], router_w[i], router_b[i], expert_w, n_out,
                                       n_lat_rows, n_seq, b)
        if i == depth - 1:
            xa = _combine(xa, y_slab, gates_t, mod, r, i, n_out, n_out, n_lat_rows, n_seq, b, final_g=final_g)
        else:
            nxt_attn = (i + 1) % N_MIXERS == 0
            jn = (i + 1) // N_MIXERS
            n_next = n_all_rows if (nxt_attn or i + 1 < depth - 1) else n_lat_rows
            nxt = ("attn", norm_g[i + 1, 0], attn_w_qkv[jn]) if nxt_attn else ("conv", norm_g[i + 1, 0], conv_w_in[jn])
            xa, proj = _combine(xa, y_slab, gates_t, mod, r, i, n_next, n_out, n_lat_rows, n_seq, b, nxt=nxt)
    return xa.reshape(b, n_seq, d)
```

```python
import functools
import math

import jax
import jax.numpy as jnp
from jax import lax
from jax.experimental import pallas as pl
from jax.experimental.pallas import tpu as pltpu

F32 = jnp.float32
BF16 = jnp.bfloat16
I32 = jnp.int32
U32 = jnp.uint32

GRID_W = 64
HEAD_DIM = 64
ROPE_BASE = 10000.0
N_MIXERS = 2
TOP_K = 4
SWIGLU_LIMIT = 7.0
SWIGLU_ALPHA = 1.702
N_MOD = 6
EPS = 1e-5

LANES = 128
SUBLANES = 8
BF16_SUBLANES = 16
VMEM_LIMIT_BYTES = 48 * 1024 * 1024

ROW_TILE = 512
FUSED_ROW_TILE = 256
ATTN_Q_TILE = 512
MOE_TILE = 256

_NT = (((1,), (1,)), ((), ()))


def _round_up(a, b):
    return (a + b - 1) // b * b


def _params(*sem):
    return pltpu.CompilerParams(dimension_semantics=sem, vmem_limit_bytes=VMEM_LIMIT_BYTES)


def _pack_bf16_pair(lo, hi):
    lo_bits = lax.bitcast_convert_type(lo.astype(jnp.bfloat16).astype(F32), U32)
    hi_bits = lax.bitcast_convert_type(hi.astype(jnp.bfloat16).astype(F32), U32)
    return hi_bits | (lo_bits >> 16)


def _unpack_bf16_pair(w):
    return lax.bitcast_convert_type(w << 16, F32), lax.bitcast_convert_type(w & jnp.uint32(0xFFFF0000), F32)


def _norm_mod(x, g, shift, scale):
    ms = jnp.mean(x * x, axis=-1, keepdims=True)
    return (x * lax.rsqrt(ms + EPS) * g) * (1.0 + scale) + shift


def _mod_body(c_ref, w_ref, b_ref, o_ref):
    c = c_ref[...]
    s = c * jax.nn.sigmoid(c)
    o_ref[0] = jnp.dot(s, w_ref[0], preferred_element_type=F32, precision=lax.Precision.HIGHEST) + b_ref[0]


def _modulation(c, c_ctx, mod_w, mod_b):
    depth, d, nd = mod_w.shape
    b = c.shape[0]
    r = _round_up(b + 1, SUBLANES)
    cv = jnp.concatenate([c, c_ctx[None, :], jnp.zeros((r - b - 1, d), F32)], axis=0)
    tn = d
    out = pl.pallas_call(
        _mod_body,
        grid=(depth, nd // tn),
        in_specs=[
            pl.BlockSpec((r, d), lambda l, j: (0, 0)),
            pl.BlockSpec((1, d, tn), lambda l, j: (l, 0, j)),
            pl.BlockSpec((1, 1, tn), lambda l, j: (l, 0, j)),
        ],
        out_specs=pl.BlockSpec((1, r, tn), lambda l, j: (l, 0, j)),
        out_shape=jax.ShapeDtypeStruct((depth, r, nd), F32),
        compiler_params=_params("parallel", "parallel"),
        name="modulation",
    )(cv, mod_w, mod_b.reshape(depth, 1, nd))
    return out.reshape(depth * r, 1, nd), r


def _mod_spec(layer, chunk, r, d, n_lat_tiles, tm, n_seq, b):
    def index(i, *_):
        row = jnp.where(i < n_lat_tiles, (i * tm) // n_seq, b)
        return (layer * r + row, 0, chunk)

    return pl.BlockSpec((1, 1, d), index)


def _rope_tables(n, tm):
    rows = n // GRID_W
    row_pos = jnp.repeat(jnp.arange(rows, dtype=F32), GRID_W)
    col_pos = jnp.tile(jnp.arange(GRID_W, dtype=F32), rows)
    axis_dims = HEAD_DIM // 2
    inv_freq = ROPE_BASE ** (-jnp.arange(0, axis_dims, 2, dtype=F32) / axis_dims)
    ang = jnp.concatenate([row_pos[:, None] * inv_freq, col_pos[:, None] * inv_freq], axis=-1)
    cos, sin = jnp.cos(ang), jnp.sin(ang)
    reps = LANES // (HEAD_DIM // 2)
    cos_t = jnp.tile(cos, (1, reps))
    sign = jnp.where((jnp.arange(LANES) % HEAD_DIM) < HEAD_DIM // 2, -1.0, 1.0).astype(F32)
    sin_t = jnp.tile(sin, (1, reps)) * sign
    cos_t = jnp.concatenate([cos_t, jnp.ones((tm, LANES), F32)], axis=0)
    sin_t = jnp.concatenate([sin_t, jnp.zeros((tm, LANES), F32)], axis=0)
    return cos_t, sin_t


def _qkv_rows(x, sh_ref, sc_ref, g_ref, w_ref, cos_ref, sin_ref, o_ref, d):
    h = _norm_mod(x, g_ref[...], sh_ref[0], sc_ref[0])
    acc = jnp.dot(h.astype(BF16), w_ref[...], preferred_element_type=F32)
    cos = cos_ref[...]
    sin = sin_ref[...]
    lane = lax.broadcasted_iota(I32, (1, LANES), 1)
    first_half = (lane % HEAD_DIM) < HEAD_DIM // 2
    q_scale = HEAD_DIM**-0.5 * math.log2(math.e)
    for gi in range(2 * d // LANES):
        xg = acc[:, gi * LANES:(gi + 1) * LANES]
        partner = jnp.where(first_half, pltpu.roll(xg, LANES - HEAD_DIM // 2, 1), pltpu.roll(xg, HEAD_DIM // 2, 1))
        r = xg * cos + partner * sin
        if gi < d // LANES:
            r = r * q_scale
        o_ref[:, gi * LANES:(gi + 1) * LANES] = r.astype(BF16)
    o_ref[:, 2 * d:] = acc[:, 2 * d:].astype(BF16)


def _qkv_body(x_ref, sh_ref, sc_ref, g_ref, w_ref, cos_ref, sin_ref, o_ref, *, d):
    _qkv_rows(x_ref[...], sh_ref, sc_ref, g_ref, w_ref, cos_ref, sin_ref, o_ref, d)


def _qkv_proj(xa, mod, r, layer, g, w, n_rows, n_lat_rows, n_seq, b):
    d = xa.shape[1]
    tm = ROW_TILE
    n_lat_tiles = n_lat_rows // tm
    tiles_per_seq = n_seq // tm
    cos_t, sin_t = _rope_tables(n_seq, tm)

    def rope_index(i):
        return (jnp.where(i < n_lat_tiles, i % tiles_per_seq, tiles_per_seq), 0)

    return pl.pallas_call(
        functools.partial(_qkv_body, d=d),
        grid=(n_rows // tm,),
        in_specs=[
            pl.BlockSpec((tm, d), lambda i: (i, 0)),
            _mod_spec(layer, 0, r, d, n_lat_tiles, tm, n_seq, b),
            _mod_spec(layer, 1, r, d, n_lat_tiles, tm, n_seq, b),
            pl.BlockSpec((1, d), lambda i: (0, 0)),
            pl.BlockSpec((d, 3 * d), lambda i: (0, 0)),
            pl.BlockSpec((tm, LANES), rope_index),
            pl.BlockSpec((tm, LANES), rope_index),
        ],
        out_specs=pl.BlockSpec((tm, 3 * d), lambda i: (i, 0)),
        out_shape=jax.ShapeDtypeStruct((n_rows, 3 * d), BF16),
        compiler_params=_params("parallel"),
        name="qkv_proj",
    )(xa, mod, mod, g[None, :], w.astype(BF16), cos_t, sin_t)


def _diff_attention(lam_ref, sg_ref, q_ref, k_refs, va_refs, o_ref, lam_init):
    q = q_ref[...]
    lane = lax.broadcasted_iota(I32, (1, LANES), 1)
    lp = lam_ref[...]
    lam = (jnp.exp(jnp.sum(lp[0:1] * lp[1:2], axis=-1, keepdims=True))
           - jnp.exp(jnp.sum(lp[2:3] * lp[3:4], axis=-1, keepdims=True)) + lam_init)
    scores = []
    for comp in range(2):
        in_comp = (lane < HEAD_DIM) if comp == 0 else (lane >= HEAD_DIM)
        qc = jnp.where(in_comp, q, jnp.zeros_like(q))
        scores.append([lax.dot_general(qc, k[...], _NT, preferred_element_type=F32) for k in k_refs])
    outs = []
    for ss in scores:
        m = functools.reduce(jnp.maximum, [jnp.max(s, axis=-1, keepdims=True) for s in ss])
        acc = None
        for s, va in zip(ss, va_refs):
            part = jnp.dot(jnp.exp2(s - m).astype(BF16), va[...], preferred_element_type=F32)
            acc = part if acc is None else acc + part
        outs.append(acc[:, :LANES] / acc[:, LANES:])
    o = outs[0] - lam * outs[1]
    ms = jnp.mean(o * o, axis=-1, keepdims=True)
    o_ref[...] = (o * lax.rsqrt(ms + EPS) * sg_ref[...] * (1.0 - lam_init)).astype(BF16)


def _fill_value_ones(va, v_ref):
    va[:, :LANES] = v_ref[...]
    va[:, LANES:] = jnp.ones((va.shape[0], LANES), BF16)


def _attn_lat_body(lam_ref, sg_ref, q_ref, kc_ref, kl_ref, vc_ref, vl_ref, o_ref, vac, val, *, lam_init):
    @pl.when(pl.program_id(2) == 0)
    def _():
        _fill_value_ones(vac, vc_ref)
        _fill_value_ones(val, vl_ref)

    _diff_attention(lam_ref, sg_ref, q_ref, [kc_ref, kl_ref], [vac, val], o_ref, lam_init)


def _attn_ctx_body(lam_ref, sg_ref, q_ref, kc_ref, vc_ref, o_ref, vac, *, lam_init):
    _fill_value_ones(vac, vc_ref)
    _diff_attention(lam_ref, sg_ref, q_ref, [kc_ref], [vac], o_ref, lam_init)


def _attention(qkv, lam_p, subln_g, lam_init, b, n_seq, n_ctx, n_lat_rows, update_ctx):
    d = qkv.shape[1] // 3
    heads = d // LANES
    tq = ATTN_Q_TILE
    assert n_seq % tq == 0
    q_tiles = n_seq // tq
    ctx_blk0 = n_lat_rows // n_ctx
    small = [pl.BlockSpec((4, HEAD_DIM), lambda *_: (0, 0)), pl.BlockSpec((1, LANES), lambda *_: (0, 0))]
    o_lat = pl.pallas_call(
        functools.partial(_attn_lat_body, lam_init=lam_init),
        grid=(b, heads, q_tiles),
        in_specs=small + [
            pl.BlockSpec((tq, LANES), lambda bi, h, i: (bi * q_tiles + i, h)),
            pl.BlockSpec((n_ctx, LANES), lambda bi, h, i: (ctx_blk0 + bi, heads + h)),
            pl.BlockSpec((n_seq, LANES), lambda bi, h, i: (bi, heads + h)),
            pl.BlockSpec((n_ctx, LANES), lambda bi, h, i: (ctx_blk0 + bi, 2 * heads + h)),
            pl.BlockSpec((n_seq, LANES), lambda bi, h, i: (bi, 2 * heads + h)),
        ],
        out_specs=pl.BlockSpec((tq, LANES), lambda bi, h, i: (bi * q_tiles + i, h)),
        out_shape=jax.ShapeDtypeStruct((n_lat_rows, d), BF16),
        scratch_shapes=[pltpu.VMEM((n_ctx, 2 * LANES), BF16), pltpu.VMEM((n_seq, 2 * LANES), BF16)],
        compiler_params=_params("parallel", "parallel", "arbitrary"),
        name="attn_latent",
    )(lam_p, subln_g[None, :], qkv, qkv, qkv, qkv, qkv)
    if not update_ctx:
        return o_lat, None
    o_ctx = pl.pallas_call(
        functools.partial(_attn_ctx_body, lam_init=lam_init),
        grid=(b, heads),
        in_specs=small + [
            pl.BlockSpec((n_ctx, LANES), lambda bi, h: (ctx_blk0 + bi, h)),
            pl.BlockSpec((n_ctx, LANES), lambda bi, h: (ctx_blk0 + bi, heads + h)),
            pl.BlockSpec((n_ctx, LANES), lambda bi, h: (ctx_blk0 + bi, 2 * heads + h)),
        ],
        out_specs=pl.BlockSpec((n_ctx, LANES), lambda bi, h: (bi, h)),
        out_shape=jax.ShapeDtypeStruct((b * n_ctx, d), BF16),
        scratch_shapes=[pltpu.VMEM((n_ctx, 2 * LANES), BF16)],
        compiler_params=_params("parallel", "parallel"),
        name="attn_context",
    )(lam_p, subln_g[None, :], qkv, qkv, qkv)
    return o_lat, o_ctx


def _oproj_body(ol_ref, oc_ref, w_ref, x_ref, g1_ref, *refs, n_lat_tiles, tm, d, n_exp):
    router_in, out_ref, router_out = refs[:N_ROUTER_IN], refs[N_ROUTER_IN], refs[N_ROUTER_IN + 1:]

    def finish(o_ref):
        y = jnp.dot(o_ref[...], w_ref[...], preferred_element_type=F32)
        out_ref[...] = x_ref[...] + g1_ref[0] * y

    @pl.when(pl.program_id(0) < n_lat_tiles)
    def _():
        finish(ol_ref)

    @pl.when(pl.program_id(0) >= n_lat_tiles)
    def _():
        finish(oc_ref)

    _router_rows(out_ref[...], *router_in, *router_out, tm=tm, d=d, n_exp=n_exp)


def _out_proj(o_lat, o_ctx, w, xa, mod, r, layer, g_moe, router_w, router_b, n_lat_rows, n_seq, b):
    d = o_lat.shape[1]
    tm = ROW_TILE
    n_lat_tiles = n_lat_rows // tm
    if o_ctx is None:
        o_ctx = o_lat
    n_rows = n_lat_rows + (0 if o_ctx is o_lat else o_ctx.shape[0])
    r_in, r_args, r_out, r_shape, n_exp = _router_io(d, mod, r, layer, g_moe, router_w, router_b, n_rows, n_lat_rows,
                                                     n_seq, b, tm)
    outs = pl.pallas_call(
        functools.partial(_oproj_body, n_lat_tiles=n_lat_tiles, tm=tm, d=d, n_exp=n_exp),
        grid=(n_rows // tm,),
        in_specs=[
            pl.BlockSpec((tm, d), lambda i: (jnp.minimum(i, n_lat_tiles - 1), 0)),
            pl.BlockSpec((tm, d), lambda i: (jnp.maximum(i - n_lat_tiles, 0), 0)),
            pl.BlockSpec((d, d), lambda i: (0, 0)),
            pl.BlockSpec((tm, d), lambda i: (i, 0)),
            _mod_spec(layer, 2, r, d, n_lat_tiles, tm, n_seq, b),
        ] + r_in,
        out_specs=[pl.BlockSpec((tm, d), lambda i: (i, 0))] + r_out,
        out_shape=[jax.ShapeDtypeStruct((n_rows, d), F32)] + r_shape,
        compiler_params=_params("arbitrary"),
        name="attn_out_proj_router",
    )(o_lat, o_ctx, w.astype(BF16), xa, mod, *r_args)
    return outs[0], outs[1:]


def _cproj_rows(x, sh_ref, sc_ref, g_ref, w_ref, gb_ref, u_ref, d):
    h = _norm_mod(x, g_ref[...], sh_ref[0], sc_ref[0])
    acc = jnp.dot(h.astype(BF16), w_ref[...], preferred_element_type=F32)
    gb_ref[...] = acc[:, :d].astype(BF16)
    u_ref[...] = (acc[:, d:2 * d] * acc[:, 2 * d:]).astype(BF16)


def _cproj_body(x_ref, sh_ref, sc_ref, g_ref, w_ref, gb_ref, u_ref, *, d):
    _cproj_rows(x_ref[...], sh_ref, sc_ref, g_ref, w_ref, gb_ref, u_ref, d)


def _conv_proj(xa, mod, r, layer, g, w, n_rows, n_lat_rows, n_seq, b):
    d = xa.shape[1]
    tm = ROW_TILE
    n_lat_tiles = n_lat_rows // tm
    return pl.pallas_call(
        functools.partial(_cproj_body, d=d),
        grid=(n_rows // tm,),
        in_specs=[
            pl.BlockSpec((tm, d), lambda i: (i, 0)),
            _mod_spec(layer, 0, r, d, n_lat_tiles, tm, n_seq, b),
            _mod_spec(layer, 1, r, d, n_lat_tiles, tm, n_seq, b),
            pl.BlockSpec((1, d), lambda i: (0, 0)),
            pl.BlockSpec((d, 3 * d), lambda i: (0, 0)),
        ],
        out_specs=[pl.BlockSpec((tm, d), lambda i: (i, 0)), pl.BlockSpec((tm, d), lambda i: (i, 0))],
        out_shape=[jax.ShapeDtypeStruct((n_rows, d), BF16), jax.ShapeDtypeStruct((n_rows, d), BF16)],
        compiler_params=_params("parallel"),
        name="conv_in_proj",
    )(xa, mod, mod, g[None, :], w.astype(BF16))


def _cout_body(u_ref, up_ref, un_ref, gb_ref, cw_ref, w_ref, x_ref, g1_ref, *refs, tm, n_lat_rows, n_seq, n_ctx, d,
               n_exp):
    router_in, out_ref, router_out = refs[:N_ROUTER_IN], refs[N_ROUTER_IN], refs[N_ROUTER_IN + 1:]
    i = pl.program_id(0)
    u = u_ref[...].astype(F32)
    prev_last = up_ref[BF16_SUBLANES - 1:BF16_SUBLANES, :].astype(F32)
    next_first = un_ref[0:1, :].astype(F32)
    row = lax.broadcasted_iota(I32, (tm, 1), 0)
    grow = i * tm + row
    is_lat = grow < n_lat_rows
    pos = jnp.where(is_lat, lax.rem(grow, n_seq), lax.rem(jnp.maximum(grow - n_lat_rows, 0), n_ctx))
    seq_len = jnp.where(is_lat, n_seq, n_ctx)
    um = jnp.where(row == 0, prev_last, pltpu.roll(u, 1, 0))
    um = jnp.where(pos == 0, 0.0, um)
    un = jnp.where(row == tm - 1, next_first, pltpu.roll(u, tm - 1, 0))
    un = jnp.where(pos == seq_len - 1, 0.0, un)
    cw = cw_ref[...]
    conv = um * cw[0:1] + u * cw[1:2] + un * cw[2:3]
    z = gb_ref[...].astype(F32) * conv
    y = jnp.dot(z.astype(BF16), w_ref[...], preferred_element_type=F32)
    out_ref[...] = x_ref[...] + g1_ref[0] * y
    _router_rows(out_ref[...], *router_in, *router_out, tm=tm, d=d, n_exp=n_exp)


def _conv_out(u, gb, conv_w, w, xa, mod, r, layer, g_moe, router_w, router_b, n_lat_rows, n_seq, n_ctx, b):
    n_rows, d = u.shape
    tm = ROW_TILE
    halo = BF16_SUBLANES
    n_halo_blocks = n_rows // halo
    r_in, r_args, r_out, r_shape, n_exp = _router_io(d, mod, r, layer, g_moe, router_w, router_b, n_rows, n_lat_rows,
                                                     n_seq, b, tm)
    body = functools.partial(_cout_body, tm=tm, n_lat_rows=n_lat_rows, n_seq=n_seq, n_ctx=n_ctx, d=d, n_exp=n_exp)
    outs = pl.pallas_call(
        body,
        grid=(n_rows // tm,),
        in_specs=[
            pl.BlockSpec((tm, d), lambda i: (i, 0)),
            pl.BlockSpec((halo, d), lambda i: (jnp.maximum(i * (tm // halo) - 1, 0), 0)),
            pl.BlockSpec((halo, d), lambda i: (jnp.minimum((i + 1) * (tm // halo), n_halo_blocks - 1), 0)),
            pl.BlockSpec((tm, d), lambda i: (i, 0)),
            pl.BlockSpec((3, d), lambda i: (0, 0)),
            pl.BlockSpec((d, d), lambda i: (0, 0)),
            pl.BlockSpec((tm, d), lambda i: (i, 0)),
            _mod_spec(layer, 2, r, d, n_lat_rows // tm, tm, n_seq, b),
        ] + r_in,
        out_specs=[pl.BlockSpec((tm, d), lambda i: (i, 0))] + r_out,
        out_shape=[jax.ShapeDtypeStruct((n_rows, d), F32)] + r_shape,
        compiler_params=_params("arbitrary"),
        name="conv_out_proj_router",
    )(u, u, u, gb, conv_w, w.astype(BF16), xa, mod, *r_args)
    return outs[0], outs[1:]


def _router_rows(x, sh_ref, sc_ref, g_ref, rw_hi_ref, rw_lo_ref, rb_ref, earlier_ref, h_ref, idx_ref, gate_ref,
                 rank_ref, count_ref, *, tm, d, n_exp):
    @pl.when(pl.program_id(0) == 0)
    def _():
        count_ref[...] = jnp.zeros_like(count_ref)

    h = _norm_mod(x, g_ref[...], sh_ref[0], sc_ref[0])
    half = d // LANES // 2
    for s in range(half):
        h_ref[pl.ds(s, tm, stride=half), :] = _pack_bf16_pair(h[:, s * LANES:(s + 1) * LANES],
                                                              h[:, (s + half) * LANES:(s + half + 1) * LANES])
    h_hi = h.astype(BF16)
    h_lo = (h - h_hi.astype(F32)).astype(BF16)
    logits = (lax.dot_general(rw_hi_ref[...], h_hi, _NT, preferred_element_type=F32)
              + lax.dot_general(rw_hi_ref[...], h_lo, _NT, preferred_element_type=F32)
              + lax.dot_general(rw_lo_ref[...], h_hi, _NT, preferred_element_type=F32)) + rb_ref[...]
    eidx = lax.broadcasted_iota(I32, (n_exp, tm), 0)
    tops, sels = [], []
    for _ in range(TOP_K):
        m = jnp.max(logits, axis=0, keepdims=True)
        sel = jnp.min(jnp.where(logits == m, eidx, n_exp), axis=0, keepdims=True)
        tops.append(m)
        sels.append(sel)
        logits = jnp.where(eidx == sel, -jnp.inf, logits)
    es = [jnp.exp(m - tops[0]) for m in tops]
    den = functools.reduce(jnp.add, es)
    idx_ref[...] = jnp.concatenate(sels, axis=0)
    gate_ref[...] = jnp.concatenate([e / den for e in es], axis=0)
    onehots = [(eidx == sel).astype(F32) for sel in sels]
    prefix = jnp.dot(jnp.concatenate(onehots, axis=0).astype(BF16), earlier_ref[...], preferred_element_type=F32)
    offset = count_ref[...]
    ranks = []
    for k in range(TOP_K):
        within = prefix[k * n_exp:(k + 1) * n_exp] + offset
        ranks.append(jnp.sum(onehots[k] * within, axis=0, keepdims=True))
        offset = offset + jnp.sum(onehots[k], axis=1, keepdims=True)
    rank_ref[...] = jnp.concatenate(ranks, axis=0).astype(I32)
    count_ref[...] = offset


N_ROUTER_IN = 7
N_ROUTER_OUT = 5


def _router_io(d, mod, r, layer, g, router_w, router_b, n_rows, n_lat_rows, n_seq, b, tm):
    n_exp = router_w.shape[1]
    slabs = d // LANES
    n_lat_tiles = n_lat_rows // tm
    rw_t = router_w.T
    rw_hi = rw_t.astype(BF16)
    rw_lo = (rw_t - rw_hi.astype(F32)).astype(BF16)
    earlier = (jnp.arange(tm)[:, None] < jnp.arange(tm)[None, :]).astype(BF16)
    in_specs = [
        _mod_spec(layer, 3, r, d, n_lat_tiles, tm, n_seq, b),
        _mod_spec(layer, 4, r, d, n_lat_tiles, tm, n_seq, b),
        pl.BlockSpec((1, d), lambda i: (0, 0)),
        pl.BlockSpec((n_exp, d), lambda i: (0, 0)),
        pl.BlockSpec((n_exp, d), lambda i: (0, 0)),
        pl.BlockSpec((n_exp, 1), lambda i: (0, 0)),
        pl.BlockSpec((tm, tm), lambda i: (0, 0)),
    ]
    args = [mod, mod, g[None, :], rw_hi, rw_lo, router_b[:, None], earlier]
    out_specs = [
        pl.BlockSpec((tm * slabs // 2, LANES), lambda i: (i, 0)),
        pl.BlockSpec((TOP_K, tm), lambda i: (0, i)),
        pl.BlockSpec((TOP_K, tm), lambda i: (0, i)),
        pl.BlockSpec((TOP_K, tm), lambda i: (0, i)),
        pl.BlockSpec((n_exp, 1), lambda i: (0, 0)),
    ]
    out_shape = [
        jax.ShapeDtypeStruct((n_rows * slabs // 2, LANES), U32),
        jax.ShapeDtypeStruct((TOP_K, n_rows), I32),
        jax.ShapeDtypeStruct((TOP_K, n_rows), F32),
        jax.ShapeDtypeStruct((TOP_K, n_rows), I32),
        jax.ShapeDtypeStruct((n_exp, 1), F32),
    ]
    assert len(in_specs) == N_ROUTER_IN and len(out_specs) == N_ROUTER_OUT
    return in_specs, args, out_specs, out_shape, n_exp


PLAN_TOKEN_TILE = 2048


def _plan_body(cols_ref, idx_ref, rank_ref, pos_ref, vis_ref, *, n_exp, tile, tm):
    cols = cols_ref[...]
    start, end, first_tile, visit_start, visit_end = (cols[:, c:c + 1] for c in range(5))
    idx = idx_ref[...]
    experts = lax.broadcasted_iota(I32, (n_exp, tm), 0)
    pos_ref[...] = rank_ref[...] + jnp.concatenate(
        [jnp.sum(jnp.where(experts == idx[k:k + 1], start, 0), axis=0, keepdims=True) for k in range(TOP_K)], axis=0)

    @pl.when(pl.program_id(0) == 0)
    def _():
        width = vis_ref.shape[1]
        v = lax.broadcasted_iota(I32, (1, width), 1)
        n_visits = jnp.max(visit_end, axis=0, keepdims=True)
        e_of = jnp.minimum(jnp.sum((v >= visit_end).astype(I32), axis=0, keepdims=True), n_exp - 1)
        mine = lax.broadcasted_iota(I32, (n_exp, width), 0) == e_of

        def pick(col):
            return jnp.sum(jnp.where(mine, col, 0), axis=0, keepdims=True)

        t_of = pick(first_tile) + v - pick(visit_start)
        lo = jnp.maximum(pick(start), t_of * tile) - t_of * tile
        hi = jnp.minimum(pick(end), (t_of + 1) * tile) - t_of * tile
        valid = v < n_visits
        at_last = v == jnp.maximum(n_visits - 1, 0)
        t_last = jnp.sum(jnp.where(at_last, t_of, 0), axis=1, keepdims=True)
        e_last = jnp.sum(jnp.where(at_last, e_of, 0), axis=1, keepdims=True)
        vis_ref[0:1, :] = jnp.where(valid, t_of, t_last)
        vis_ref[1:2, :] = jnp.where(valid, e_of, e_last)
        vis_ref[2:3, :] = jnp.where(valid, lo, 0)
        vis_ref[3:4, :] = jnp.where(valid, hi, 0)
        vis_ref[4:SUBLANES, :] = jnp.broadcast_to(n_visits, (SUBLANES - 4, width))


def _dispatch_plan(top_idx, rank, counts, tile, in_slabs, out_slabs):
    n_tok = top_idx.shape[1]
    n_exp = counts.shape[0]
    n_assign = TOP_K * n_tok
    n_tiles = n_assign // tile
    max_visits = n_tiles + n_exp - 1
    counts = counts[:, 0].astype(I32)
    end = jnp.cumsum(counts)
    start = end - counts
    first_tile = start // tile
    visits = jnp.where(counts > 0, (end - 1) // tile - first_tile + 1, 0)
    visit_end = jnp.cumsum(visits)
    cols = jnp.stack([start, end, first_tile, visit_end - visits, visit_end] + [end] * (SUBLANES - 5), axis=1)
    tm = math.gcd(n_tok, PLAN_TOKEN_TILE)
    width = _round_up(max_visits, LANES)
    pos, vis = pl.pallas_call(
        functools.partial(_plan_body, n_exp=n_exp, tile=tile, tm=tm),
        grid=(n_tok // tm,),
        in_specs=[
            pl.BlockSpec((n_exp, SUBLANES), lambda i: (0, 0)),
            pl.BlockSpec((TOP_K, tm), lambda i: (0, i)),
            pl.BlockSpec((TOP_K, tm), lambda i: (0, i)),
        ],
        out_specs=[pl.BlockSpec((TOP_K, tm), lambda i: (0, i)), pl.BlockSpec((SUBLANES, width), lambda i: (0, 0))],
        out_shape=[jax.ShapeDtypeStruct((TOP_K, n_tok), I32), jax.ShapeDtypeStruct((SUBLANES, width), I32)],
        compiler_params=_params("arbitrary"),
        name="moe_dispatch_plan",
    )(cols, top_idx, rank)
    slot_of_row = jnp.zeros((n_assign,), I32).at[pos.reshape(-1)].add(jnp.arange(n_assign, dtype=I32))
    row_token = ((slot_of_row % n_tok) * in_slabs).reshape(n_tiles, 1, tile)
    row_slot = (slot_of_row * out_slabs).reshape(n_tiles, 1, tile)
    return (row_token, row_slot, vis[0, :max_visits], vis[1, :max_visits], vis[2, :max_visits], vis[3, :max_visits],
            vis[4, :1])


def _expert_body(tile_ref, exp_ref, lo_ref, hi_ref, nv_ref,
                 idx_ref, idx0_ref, h_hbm, wg_ref, wl_ref, wd_ref, bg_ref, bl1_ref, bd_ref,
                 y_hbm, gbuf, xs, yacc, ybuf, wd_bf, gsem, ssem, *, tile, slabs):
    v = pl.program_id(0)
    n_visits = nv_ref[0]
    t = tile_ref[v]
    new_expert = jnp.logical_or(v == 0, exp_ref[jnp.maximum(v - 1, 0)] != exp_ref[v])
    first_visit = jnp.logical_or(v == 0, tile_ref[jnp.maximum(v - 1, 0)] != t)
    last_visit = jnp.logical_or(v == n_visits - 1, tile_ref[jnp.minimum(v + 1, pl.num_programs(0) - 1)] != t)
    valid = v < n_visits
    straight = jnp.logical_and(jnp.logical_and(first_visit, last_visit), t >= 1)
    g_cur = lax.rem(t, 2)
    y_cur = lax.rem(t, 3)
    y_prev = lax.rem(t + 2, 3)
    half = slabs // 2
    rows = tile * half

    own_tok, next_tok, prev_slot, own_slot = range(4)

    def start_gather(ref, which, g):
        for r in range(tile):
            src = pl.multiple_of(ref[0, 0, which * tile + r], half)
            pltpu.make_async_copy(h_hbm.at[pl.ds(src, half), :], gbuf.at[g, pl.ds(r * half, half), :],
                                  gsem.at[g]).start()

    def wait_gather(g):
        pltpu.make_async_copy(h_hbm.at[pl.ds(0, rows), :], gbuf.at[g], gsem.at[g]).wait()

    def start_scatter(which, yb):
        for r in range(tile):
            dst = pl.multiple_of(idx_ref[0, 0, which * tile + r], half)
            pltpu.make_async_copy(ybuf.at[yb, pl.ds(r * half, half), :], y_hbm.at[pl.ds(dst, half), :],
                                  ssem.at[yb]).start()

    def wait_scatter(yb):
        pltpu.make_async_copy(ybuf.at[yb], y_hbm.at[pl.ds(0, rows), :], ssem.at[yb]).wait()

    def begin_tile():
        start_gather(idx_ref, next_tok, 1 - g_cur)
        for s in range(half):
            lo, hi = _unpack_bf16_pair(gbuf[g_cur, pl.ds(s, tile, stride=half), :])
            xs[:, s * LANES:(s + 1) * LANES] = lo.astype(BF16)
            xs[:, (s + half) * LANES:(s + half + 1) * LANES] = hi.astype(BF16)

    def ffn():
        x = xs[...]
        gl = lax.dot_general(x, wg_ref[...], _NT, preferred_element_type=F32) + bg_ref[...]
        ln1 = lax.dot_general(x, wl_ref[...], _NT, preferred_element_type=F32) + bl1_ref[...]
        x_glu = jnp.minimum(gl, SWIGLU_LIMIT)
        x_lin1 = jnp.clip(ln1, 1.0 - SWIGLU_LIMIT, 1.0 + SWIGLU_LIMIT)
        act = x_glu * (1.0 + jnp.tanh((0.5 * SWIGLU_ALPHA) * x_glu)) * x_lin1
        return jnp.dot(act.astype(BF16), wd_bf[...], preferred_element_type=F32) + bd_ref[...]

    def stage(y_of_slab):
        for s in range(half):
            ybuf[y_cur, pl.ds(s, tile, stride=half), :] = _pack_bf16_pair(y_of_slab(s), y_of_slab(s + half))

    @pl.when(v == 0)
    def _():
        start_gather(idx0_ref, own_tok, 0)

    @pl.when(jnp.logical_and(valid, new_expert))
    def _():
        wd_bf[...] = (0.5 * wd_ref[...]).astype(BF16)

    @pl.when(jnp.logical_and(valid, first_visit))
    def _():
        wait_gather(g_cur)

        @pl.when(t >= 3)
        def _():
            wait_scatter(y_cur)

    @pl.when(jnp.logical_and(valid, straight))
    def _():
        start_scatter(prev_slot, y_prev)
        begin_tile()
        y = ffn()
        stage(lambda s: y[:, s * LANES:(s + 1) * LANES])

    @pl.when(jnp.logical_and(valid, jnp.logical_not(straight)))
    def _():
        @pl.when(first_visit)
        def _():
            @pl.when(t >= 1)
            def _():
                start_scatter(prev_slot, y_prev)

            begin_tile()

        y = ffn()

        @pl.when(first_visit)
        def _():
            yacc[...] = y

        @pl.when(jnp.logical_not(first_visit))
        def _():
            row = lax.broadcasted_iota(I32, (tile, 1), 0)
            mine = jnp.logical_and(row >= lo_ref[v], row < hi_ref[v])
            yacc[...] = jnp.where(mine, y, yacc[...])

        @pl.when(last_visit)
        def _():
            stage(lambda s: yacc[:, s * LANES:(s + 1) * LANES])

    @pl.when(v == n_visits - 1)
    def _():
        start_scatter(own_slot, y_cur)
        wait_gather(1 - g_cur)
        wait_scatter(y_cur)

        @pl.when(t >= 1)
        def _():
            wait_scatter(y_prev)

        @pl.when(t >= 2)
        def _():
            wait_scatter(lax.rem(t + 1, 3))


def _experts(h_slab, plan, layer, wg, wl, wd, bg, bl, bd, n_tok):
    row_token, row_slot, t_of, e_of, lo, hi, n_visits = plan
    n_tiles, _, tile = row_token.shape
    _, n_exp, f, d = wg.shape
    slabs = d // LANES
    max_visits = t_of.shape[0]

    def expert_block(v, tl, ex, lo_, hi_, nv):
        return (layer, ex[v], 0, 0)

    body = functools.partial(_expert_body, tile=tile, slabs=slabs)
    nxt = jnp.minimum(jnp.arange(n_tiles) + 1, n_tiles - 1)
    prv = jnp.maximum(jnp.arange(n_tiles) - 1, 0)
    tile_idx = jnp.concatenate([row_token, row_token[nxt], row_slot[prv], row_slot], axis=2)
    smem_tile = functools.partial(pl.BlockSpec, (1, 1, 4 * tile), memory_space=pltpu.SMEM)
    grid_spec = pltpu.PrefetchScalarGridSpec(
        num_scalar_prefetch=5,
        grid=(max_visits,),
        in_specs=[
            smem_tile(lambda v, tl, ex, lo_, hi_, nv: (tl[v], 0, 0)),
            smem_tile(lambda v, tl, ex, lo_, hi_, nv: (0, 0, 0)),
            pl.BlockSpec(memory_space=pl.ANY),
            pl.BlockSpec((None, None, f, d), expert_block),
            pl.BlockSpec((None, None, f, d), expert_block),
            pl.BlockSpec((None, None, f, d), expert_block),
            pl.BlockSpec((None, None, 1, f), expert_block),
            pl.BlockSpec((None, None, 1, f), expert_block),
            pl.BlockSpec((None, None, 1, d), expert_block),
        ],
        out_specs=pl.BlockSpec(memory_space=pl.ANY),
        scratch_shapes=[
            pltpu.VMEM((2, tile * slabs // 2, LANES), U32),
            pltpu.VMEM((tile, d), BF16),
            pltpu.VMEM((tile, d), F32),
            pltpu.VMEM((3, tile * slabs // 2, LANES), U32),
            pltpu.VMEM((f, d), BF16),
            pltpu.SemaphoreType.DMA((2,)),
            pltpu.SemaphoreType.DMA((3,)),
        ],
    )
    return pl.pallas_call(
        body,
        grid_spec=grid_spec,
        out_shape=jax.ShapeDtypeStruct((TOP_K * n_tok * slabs // 2, LANES), U32),
        compiler_params=_params("arbitrary"),
        name="moe_experts",
    )(t_of, e_of, lo, hi, n_visits, tile_idx, tile_idx, h_slab, wg, wl, wd, bg, bl, bd)


def _combine_rows(x_ref, g2_ref, gate_ref, y_refs, out_ref, tm, slabs):
    gates = gate_ref[...]
    g2 = g2_ref[0]
    half = slabs // 2
    for s in range(half):
        acc_lo = acc_hi = None
        for k in range(TOP_K):
            lo, hi = _unpack_bf16_pair(y_refs[k][pl.ds(s, tm, stride=half), :])
            acc_lo = gates[:, k:k + 1] * lo if acc_lo is None else acc_lo + gates[:, k:k + 1] * lo
            acc_hi = gates[:, k:k + 1] * hi if acc_hi is None else acc_hi + gates[:, k:k + 1] * hi
        for acc, slab in ((acc_lo, s), (acc_hi, s + half)):
            cols = slice(slab * LANES, (slab + 1) * LANES)
            out_ref[:, cols] = x_ref[:, cols] + g2[:, cols] * acc


def _combine_final_body(x_ref, g2_ref, gate_ref, *refs, tm, slabs):
    y_refs, fg_ref, out_ref = refs[:TOP_K], refs[TOP_K], refs[TOP_K + 1]
    _combine_rows(x_ref, g2_ref, gate_ref, y_refs, out_ref, tm, slabs)
    x = out_ref[...]
    ms = jnp.mean(x * x, axis=-1, keepdims=True)
    out_ref[...] = x * lax.rsqrt(ms + EPS) * fg_ref[...]


def _combine_qkv_body(x_ref, g2_ref, gate_ref, *refs, tm, slabs, d):
    y_refs = refs[:TOP_K]
    sh_ref, sc_ref, g_ref, w_ref, cos_ref, sin_ref, xo_ref, o_ref = refs[TOP_K:]
    _combine_rows(x_ref, g2_ref, gate_ref, y_refs, xo_ref, tm, slabs)
    _qkv_rows(xo_ref[...], sh_ref, sc_ref, g_ref, w_ref, cos_ref, sin_ref, o_ref, d)


def _combine_cproj_body(x_ref, g2_ref, gate_ref, *refs, tm, slabs, d):
    y_refs = refs[:TOP_K]
    sh_ref, sc_ref, g_ref, w_ref, xo_ref, gb_ref, u_ref = refs[TOP_K:]
    _combine_rows(x_ref, g2_ref, gate_ref, y_refs, xo_ref, tm, slabs)
    _cproj_rows(xo_ref[...], sh_ref, sc_ref, g_ref, w_ref, gb_ref, u_ref, d)


def _combine(xa, y_slab, gates_t, mod, r, layer, n_rows, n_moe_rows, n_lat_rows, n_seq, b, final_g=None, nxt=None):
    d = xa.shape[1]
    slabs = d // LANES
    tm = ROW_TILE if nxt is None else FUSED_ROW_TILE
    n_tiles = n_rows // tm
    n_lat_tiles = n_lat_rows // tm
    y_tiles_per_k = n_moe_rows // tm
    y_specs = [pl.BlockSpec((tm * slabs // 2, LANES), functools.partial(lambda i, k: (k * y_tiles_per_k + i, 0), k=k))
               for k in range(TOP_K)]
    row_spec = pl.BlockSpec((tm, d), lambda i: (i, 0))
    in_specs = [row_spec, _mod_spec(layer, 5, r, d, n_lat_tiles, tm, n_seq, b),
                pl.BlockSpec((tm, TOP_K), lambda i: (i, 0))] + y_specs
    args = [xa, mod, gates_t] + [y_slab] * TOP_K
    x_shape = jax.ShapeDtypeStruct((n_rows, d), F32)
    if nxt is None:
        return pl.pallas_call(
            functools.partial(_combine_final_body, tm=tm, slabs=slabs),
            grid=(n_tiles,),
            in_specs=in_specs + [pl.BlockSpec((1, d), lambda i: (0, 0))],
            out_specs=row_spec,
            out_shape=x_shape,
            compiler_params=_params("parallel"),
            name="moe_combine_final",
        )(*args, final_g[None, :])
    kind, g, w = nxt
    in_specs += [_mod_spec(layer + 1, 0, r, d, n_lat_tiles, tm, n_seq, b),
                 _mod_spec(layer + 1, 1, r, d, n_lat_tiles, tm, n_seq, b),
                 pl.BlockSpec((1, d), lambda i: (0, 0)),
                 pl.BlockSpec((d, 3 * d), lambda i: (0, 0))]
    args += [mod, mod, g[None, :], w.astype(BF16)]
    if kind == "attn":
        tiles_per_seq = n_seq // tm
        cos_t, sin_t = _rope_tables(n_seq, tm)

        def rope_index(i):
            return (jnp.where(i < n_lat_tiles, i % tiles_per_seq, tiles_per_seq), 0)

        return pl.pallas_call(
            functools.partial(_combine_qkv_body, tm=tm, slabs=slabs, d=d),
            grid=(n_tiles,),
            in_specs=in_specs + [pl.BlockSpec((tm, LANES), rope_index)] * 2,
            out_specs=[row_spec, pl.BlockSpec((tm, 3 * d), lambda i: (i, 0))],
            out_shape=[x_shape, jax.ShapeDtypeStruct((n_rows, 3 * d), BF16)],
            compiler_params=_params("parallel"),
            name="moe_combine_qkv_proj",
        )(*args, cos_t, sin_t)
    xa_new, gb, u = pl.pallas_call(
        functools.partial(_combine_cproj_body, tm=tm, slabs=slabs, d=d),
        grid=(n_tiles,),
        in_specs=in_specs,
        out_specs=[row_spec, row_spec, row_spec],
        out_shape=[x_shape, jax.ShapeDtypeStruct((n_rows, d), BF16), jax.ShapeDtypeStruct((n_rows, d), BF16)],
        compiler_params=_params("parallel"),
        name="moe_combine_conv_in_proj",
    )(*args)
    return xa_new, (gb, u)


def _wsplit_body(w_ref, wg_ref, wl_ref, ts):
    d, f2 = w_ref.shape
    half = LANES // 2
    for c in range(f2 // LANES):
        for j in range(d // LANES):
            blk = (c * (d // LANES) + j) * LANES
            ts[pl.ds(blk, LANES), :] = w_ref[j * LANES:(j + 1) * LANES, c * LANES:(c + 1) * LANES].T
            rows = slice(c * half, (c + 1) * half)
            cols = slice(j * LANES, (j + 1) * LANES)
            wg_ref[rows, cols] = ts[pl.ds(blk, half, stride=2), :].astype(BF16)
            wl_ref[rows, cols] = ts[pl.ds(blk + 1, half, stride=2), :].astype(BF16)


def _split_gate_up(w_gu):
    depth, n_exp, d, f2 = w_gu.shape
    f = f2 // 2
    out = jax.ShapeDtypeStruct((depth, n_exp, f, d), BF16)
    return pl.pallas_call(
        _wsplit_body,
        grid=(depth, n_exp),
        in_specs=[pl.BlockSpec((None, None, d, f2), lambda l, e: (l, e, 0, 0))],
        out_specs=[pl.BlockSpec((None, None, f, d), lambda l, e: (l, e, 0, 0))] * 2,
        out_shape=[out, out],
        scratch_shapes=[pltpu.VMEM((d * f2 // LANES, LANES), F32)],
        compiler_params=_params("parallel", "parallel"),
        name="moe_split_gate_up",
    )(w_gu)


def _moe_experts(routed, layer, expert_w, n_rows, d):
    h_slab, top_idx, gates, rank, counts = routed
    slabs = d // LANES
    plan = _dispatch_plan(top_idx, rank, counts, MOE_TILE, slabs // 2, slabs // 2)
    return _experts(h_slab, plan, layer, *expert_w, n_rows), gates.T


def kernel(x, c, ctx, c_ctx, mod_w, mod_b, norm_g, attn_w_qkv, attn_w_o, attn_lambda, attn_subln_g, conv_w_in, conv_w, conv_w_out, router_w, router_b, moe_w_gu, moe_b_gu, moe_w_down, moe_b_down, final_g):
    b, n_seq, d = x.shape
    n_ctx = ctx.shape[1]
    depth = mod_w.shape[0]
    n_lat_rows = b * n_seq
    n_all_rows = n_lat_rows + b * n_ctx
    assert d % LANES == 0 and d == moe_w_down.shape[-1]
    assert n_seq % ROW_TILE == 0 and (b * n_ctx) % ROW_TILE == 0 and n_seq % GRID_W == 0
    assert ROW_TILE % n_ctx == 0 or n_ctx % ROW_TILE == 0
    assert n_lat_rows % n_ctx == 0 and n_ctx % BF16_SUBLANES == 0
    assert (TOP_K * n_lat_rows) % MOE_TILE == 0 and (TOP_K * n_all_rows) % MOE_TILE == 0

    xa = jnp.concatenate([x.reshape(n_lat_rows, d), ctx.reshape(b * n_ctx, d)], axis=0)
    mod, r = _modulation(c, c_ctx, mod_w, mod_b)
    wg, wl = _split_gate_up(moe_w_gu)
    expert_w = (wg, wl, moe_w_down, moe_b_gu[:, :, None, 0::2], moe_b_gu[:, :, None, 1::2] + 1.0,
                moe_b_down[:, :, None, :])
    proj = None
    for i in range(depth):
        update_ctx = i < depth - 1
        attn_layer = i % N_MIXERS == 0
        j = i // N_MIXERS
        n_out = n_all_rows if update_ctx else n_lat_rows
        if attn_layer:
            lam_init = 0.8 - 0.6 * math.exp(-0.3 * i)
            qkv = proj if proj is not None else _qkv_proj(xa, mod, r, i, norm_g[i, 0], attn_w_qkv[j], n_all_rows,
                                                           n_lat_rows, n_seq, b)
            o_lat, o_ctx = _attention(qkv, attn_lambda[j], attn_subln_g[j], lam_init, b, n_seq, n_ctx, n_lat_rows,
                                      update_ctx)
            xa, routed = _out_proj(o_lat, o_ctx, attn_w_o[j], xa, mod, r, i, norm_g[i, 1], router_w[i], router_b[i],
                                   n_lat_rows, n_seq, b)
        else:
            gb, u = proj if proj is not None else _conv_proj(xa, mod, r, i, norm_g[i, 0], conv_w_in[j], n_out,
                                                             n_lat_rows, n_seq, b)
            xa, routed = _conv_out(u, gb, conv_w[j], conv_w_out[j], xa, mod, r, i, norm_g[i, 1], router_w[i],
                                   router_b[i], n_lat_rows, n_seq, n_ctx, b)
        y_slab, gates_t = _moe_experts(routed, i, expert_w, n_out, d)
        if i == depth - 1:
            xa = _combine(xa, y_slab, gates_t, mod, r, i, n_out, n_out, n_lat_rows, n_seq, b, final_g=final_g)
        else:
            nxt_attn = (i + 1) % N_MIXERS == 0
            jn = (i + 1) // N_MIXERS
            n_next = n_all_rows if (nxt_attn or i + 1 < depth - 1) else n_lat_rows
            nxt = ("attn", norm_g[i + 1, 0], attn_w_qkv[jn]) if nxt_attn else ("conv", norm_g[i + 1, 0], conv_w_in[jn])
            xa, proj = _combine(xa, y_slab, gates_t, mod, r, i, n_next, n_out, n_lat_rows, n_seq, b, nxt=nxt)
    return xa.reshape(b, n_seq, d)
```

```python
import functools
import math

import jax
import jax.numpy as jnp
from jax import lax
from jax.experimental import pallas as pl
from jax.experimental.pallas import tpu as pltpu

F32 = jnp.float32
BF16 = jnp.bfloat16
I32 = jnp.int32
U32 = jnp.uint32

GRID_W = 64
HEAD_DIM = 64
ROPE_BASE = 10000.0
N_MIXERS = 2
TOP_K = 4
SWIGLU_LIMIT = 7.0
SWIGLU_ALPHA = 1.702
N_MOD = 6
EPS = 1e-5

LANES = 128
SUBLANES = 8
BF16_SUBLANES = 16
VMEM_LIMIT_BYTES = 48 * 1024 * 1024

ROW_TILE = 512
FUSED_ROW_TILE = 512
ATTN_Q_TILE = 1024
MOE_TILE = 256

_NT = (((1,), (1,)), ((), ()))


def _round_up(a, b):
    return (a + b - 1) // b * b


def _params(*sem):
    return pltpu.CompilerParams(dimension_semantics=sem, vmem_limit_bytes=VMEM_LIMIT_BYTES)


def _pack_bf16_pair(lo, hi):
    lo_bits = lax.bitcast_convert_type(lo.astype(jnp.bfloat16).astype(F32), U32)
    hi_bits = lax.bitcast_convert_type(hi.astype(jnp.bfloat16).astype(F32), U32)
    return hi_bits | (lo_bits >> 16)


def _unpack_bf16_pair(w):
    return lax.bitcast_convert_type(w << 16, F32), lax.bitcast_convert_type(w & jnp.uint32(0xFFFF0000), F32)


def _norm_mod(x, g, shift, scale):
    ms = jnp.mean(x * x, axis=-1, keepdims=True)
    return (x * lax.rsqrt(ms + EPS) * g) * (1.0 + scale) + shift


def _mod_body(c_ref, w_ref, b_ref, o_ref):
    c = c_ref[...]
    s = c * jax.nn.sigmoid(c)
    o_ref[0] = jnp.dot(s, w_ref[0], preferred_element_type=F32, precision=lax.Precision.HIGHEST) + b_ref[0]


def _modulation(c, c_ctx, mod_w, mod_b):
    depth, d, nd = mod_w.shape
    b = c.shape[0]
    r = _round_up(b + 1, SUBLANES)
    cv = jnp.concatenate([c, c_ctx[None, :], jnp.zeros((r - b - 1, d), F32)], axis=0)
    tn = d
    out = pl.pallas_call(
        _mod_body,
        grid=(depth, nd // tn),
        in_specs=[
            pl.BlockSpec((r, d), lambda l, j: (0, 0)),
            pl.BlockSpec((1, d, tn), lambda l, j: (l, 0, j)),
            pl.BlockSpec((1, 1, tn), lambda l, j: (l, 0, j)),
        ],
        out_specs=pl.BlockSpec((1, r, tn), lambda l, j: (l, 0, j)),
        out_shape=jax.ShapeDtypeStruct((depth, r, nd), F32),
        compiler_params=_params("parallel", "parallel"),
        name="modulation",
    )(cv, mod_w, mod_b.reshape(depth, 1, nd))
    return out.reshape(depth * r, 1, nd), r


def _mod_spec(layer, chunk, r, d, n_lat_tiles, tm, n_seq, b):
    def index(i, *_):
        row = jnp.where(i < n_lat_tiles, (i * tm) // n_seq, b)
        return (layer * r + row, 0, chunk)

    return pl.BlockSpec((1, 1, d), index)


def _rope_tables(n, tm):
    rows = n // GRID_W
    row_pos = jnp.repeat(jnp.arange(rows, dtype=F32), GRID_W)
    col_pos = jnp.tile(jnp.arange(GRID_W, dtype=F32), rows)
    axis_dims = HEAD_DIM // 2
    inv_freq = ROPE_BASE ** (-jnp.arange(0, axis_dims, 2, dtype=F32) / axis_dims)
    ang = jnp.concatenate([row_pos[:, None] * inv_freq, col_pos[:, None] * inv_freq], axis=-1)
    cos, sin = jnp.cos(ang), jnp.sin(ang)
    reps = LANES // (HEAD_DIM // 2)
    cos_t = jnp.tile(cos, (1, reps))
    sign = jnp.where((jnp.arange(LANES) % HEAD_DIM) < HEAD_DIM // 2, -1.0, 1.0).astype(F32)
    sin_t = jnp.tile(sin, (1, reps)) * sign
    cos_t = jnp.concatenate([cos_t, jnp.ones((tm, LANES), F32)], axis=0)
    sin_t = jnp.concatenate([sin_t, jnp.zeros((tm, LANES), F32)], axis=0)
    return cos_t, sin_t


def _qkv_rows(x, sh_ref, sc_ref, g_ref, w_ref, cos_ref, sin_ref, o_ref, d):
    h = _norm_mod(x, g_ref[...], sh_ref[0], sc_ref[0])
    acc = jnp.dot(h.astype(BF16), w_ref[...], preferred_element_type=F32)
    cos = cos_ref[...]
    sin = sin_ref[...]
    lane = lax.broadcasted_iota(I32, (1, LANES), 1)
    first_half = (lane % HEAD_DIM) < HEAD_DIM // 2
    q_scale = HEAD_DIM**-0.5 * math.log2(math.e)
    for gi in range(2 * d // LANES):
        xg = acc[:, gi * LANES:(gi + 1) * LANES]
        partner = jnp.where(first_half, pltpu.roll(xg, LANES - HEAD_DIM // 2, 1), pltpu.roll(xg, HEAD_DIM // 2, 1))
        r = xg * cos + partner * sin
        if gi < d // LANES:
            r = r * q_scale
        o_ref[:, gi * LANES:(gi + 1) * LANES] = r.astype(BF16)
    o_ref[:, 2 * d:] = acc[:, 2 * d:].astype(BF16)


def _qkv_body(x_ref, sh_ref, sc_ref, g_ref, w_ref, cos_ref, sin_ref, o_ref, *, d):
    _qkv_rows(x_ref[...], sh_ref, sc_ref, g_ref, w_ref, cos_ref, sin_ref, o_ref, d)


def _qkv_proj(xa, mod, r, layer, g, w, n_rows, n_lat_rows, n_seq, b):
    d = xa.shape[1]
    tm = ROW_TILE
    n_lat_tiles = n_lat_rows // tm
    tiles_per_seq = n_seq // tm
    cos_t, sin_t = _rope_tables(n_seq, tm)

    def rope_index(i):
        return (jnp.where(i < n_lat_tiles, i % tiles_per_seq, tiles_per_seq), 0)

    return pl.pallas_call(
        functools.partial(_qkv_body, d=d),
        grid=(n_rows // tm,),
        in_specs=[
            pl.BlockSpec((tm, d), lambda i: (i, 0)),
            _mod_spec(layer, 0, r, d, n_lat_tiles, tm, n_seq, b),
            _mod_spec(layer, 1, r, d, n_lat_tiles, tm, n_seq, b),
            pl.BlockSpec((1, d), lambda i: (0, 0)),
            pl.BlockSpec((d, 3 * d), lambda i: (0, 0)),
            pl.BlockSpec((tm, LANES), rope_index),
            pl.BlockSpec((tm, LANES), rope_index),
        ],
        out_specs=pl.BlockSpec((tm, 3 * d), lambda i: (i, 0)),
        out_shape=jax.ShapeDtypeStruct((n_rows, 3 * d), BF16),
        compiler_params=_params("parallel"),
        name="qkv_proj",
    )(xa, mod, mod, g[None, :], w.astype(BF16), cos_t, sin_t)


def _diff_attention(lam_ref, sg_ref, q_ref, k_refs, va_refs, o_ref, lam_init):
    q = q_ref[...]
    lane = lax.broadcasted_iota(I32, (1, LANES), 1)
    lp = lam_ref[...]
    lam = (jnp.exp(jnp.sum(lp[0:1] * lp[1:2], axis=-1, keepdims=True))
           - jnp.exp(jnp.sum(lp[2:3] * lp[3:4], axis=-1, keepdims=True)) + lam_init)
    scores = []
    for comp in range(2):
        in_comp = (lane < HEAD_DIM) if comp == 0 else (lane >= HEAD_DIM)
        qc = jnp.where(in_comp, q, jnp.zeros_like(q))
        scores.append([lax.dot_general(qc, k[...], _NT, preferred_element_type=F32) for k in k_refs])
    outs = []
    for ss in scores:
        m = functools.reduce(jnp.maximum, [jnp.max(s, axis=-1, keepdims=True) for s in ss])
        acc = None
        for s, va in zip(ss, va_refs):
            part = jnp.dot(jnp.exp2(s - m).astype(BF16), va[...], preferred_element_type=F32)
            acc = part if acc is None else acc + part
        outs.append(acc[:, :LANES] / acc[:, LANES:])
    o = outs[0] - lam * outs[1]
    ms = jnp.mean(o * o, axis=-1, keepdims=True)
    o_ref[...] = (o * lax.rsqrt(ms + EPS) * sg_ref[...] * (1.0 - lam_init)).astype(BF16)


def _fill_value_ones(va, v_ref):
    va[:, :LANES] = v_ref[...]
    va[:, LANES:] = jnp.ones((va.shape[0], LANES), BF16)


def _attn_lat_body(lam_ref, sg_ref, q_ref, kc_ref, kl_ref, vc_ref, vl_ref, o_ref, vac, val, *, lam_init):
    @pl.when(pl.program_id(2) == 0)
    def _():
        _fill_value_ones(vac, vc_ref)
        _fill_value_ones(val, vl_ref)

    _diff_attention(lam_ref, sg_ref, q_ref, [kc_ref, kl_ref], [vac, val], o_ref, lam_init)


def _attn_ctx_body(lam_ref, sg_ref, q_ref, kc_ref, vc_ref, o_ref, vac, *, lam_init):
    _fill_value_ones(vac, vc_ref)
    _diff_attention(lam_ref, sg_ref, q_ref, [kc_ref], [vac], o_ref, lam_init)


def _attention(qkv, lam_p, subln_g, lam_init, b, n_seq, n_ctx, n_lat_rows, update_ctx):
    d = qkv.shape[1] // 3
    heads = d // LANES
    tq = ATTN_Q_TILE
    assert n_seq % tq == 0
    q_tiles = n_seq // tq
    ctx_blk0 = n_lat_rows // n_ctx
    small = [pl.BlockSpec((4, HEAD_DIM), lambda *_: (0, 0)), pl.BlockSpec((1, LANES), lambda *_: (0, 0))]
    o_lat = pl.pallas_call(
        functools.partial(_attn_lat_body, lam_init=lam_init),
        grid=(b, heads, q_tiles),
        in_specs=small + [
            pl.BlockSpec((tq, LANES), lambda bi, h, i: (bi * q_tiles + i, h)),
            pl.BlockSpec((n_ctx, LANES), lambda bi, h, i: (ctx_blk0 + bi, heads + h)),
            pl.BlockSpec((n_seq, LANES), lambda bi, h, i: (bi, heads + h)),
            pl.BlockSpec((n_ctx, LANES), lambda bi, h, i: (ctx_blk0 + bi, 2 * heads + h)),
            pl.BlockSpec((n_seq, LANES), lambda bi, h, i: (bi, 2 * heads + h)),
        ],
        out_specs=pl.BlockSpec((tq, LANES), lambda bi, h, i: (bi * q_tiles + i, h)),
        out_shape=jax.ShapeDtypeStruct((n_lat_rows, d), BF16),
        scratch_shapes=[pltpu.VMEM((n_ctx, 2 * LANES), BF16), pltpu.VMEM((n_seq, 2 * LANES), BF16)],
        compiler_params=_params("parallel", "parallel", "arbitrary"),
        name="attn_latent",
    )(lam_p, subln_g[None, :], qkv, qkv, qkv, qkv, qkv)
    if not update_ctx:
        return o_lat, None
    o_ctx = pl.pallas_call(
        functools.partial(_attn_ctx_body, lam_init=lam_init),
        grid=(b, heads),
        in_specs=small + [
            pl.BlockSpec((n_ctx, LANES), lambda bi, h: (ctx_blk0 + bi, h)),
            pl.BlockSpec((n_ctx, LANES), lambda bi, h: (ctx_blk0 + bi, heads + h)),
            pl.BlockSpec((n_ctx, LANES), lambda bi, h: (ctx_blk0 + bi, 2 * heads + h)),
        ],
        out_specs=pl.BlockSpec((n_ctx, LANES), lambda bi, h: (bi, h)),
        out_shape=jax.ShapeDtypeStruct((b * n_ctx, d), BF16),
        scratch_shapes=[pltpu.VMEM((n_ctx, 2 * LANES), BF16)],
        compiler_params=_params("parallel", "parallel"),
        name="attn_context",
    )(lam_p, subln_g[None, :], qkv, qkv, qkv)
    return o_lat, o_ctx


def _oproj_body(ol_ref, oc_ref, w_ref, x_ref, g1_ref, *refs, n_lat_tiles, tm, d, n_exp):
    router_in, out_ref, router_out = refs[:N_ROUTER_IN], refs[N_ROUTER_IN], refs[N_ROUTER_IN + 1:]

    def finish(o_ref):
        y = jnp.dot(o_ref[...], w_ref[...], preferred_element_type=F32)
        out_ref[...] = x_ref[...] + g1_ref[0] * y

    @pl.when(pl.program_id(0) < n_lat_tiles)
    def _():
        finish(ol_ref)

    @pl.when(pl.program_id(0) >= n_lat_tiles)
    def _():
        finish(oc_ref)

    _router_rows(out_ref[...], *router_in, *router_out, tm=tm, d=d, n_exp=n_exp)


def _out_proj(o_lat, o_ctx, w, xa, mod, r, layer, g_moe, router_w, router_b, n_lat_rows, n_seq, b):
    d = o_lat.shape[1]
    tm = ROW_TILE
    n_lat_tiles = n_lat_rows // tm
    if o_ctx is None:
        o_ctx = o_lat
    n_rows = n_lat_rows + (0 if o_ctx is o_lat else o_ctx.shape[0])
    r_in, r_args, r_out, r_shape, n_exp = _router_io(d, mod, r, layer, g_moe, router_w, router_b, n_rows, n_lat_rows,
                                                     n_seq, b, tm)
    outs = pl.pallas_call(
        functools.partial(_oproj_body, n_lat_tiles=n_lat_tiles, tm=tm, d=d, n_exp=n_exp),
        grid=(n_rows // tm,),
        in_specs=[
            pl.BlockSpec((tm, d), lambda i: (jnp.minimum(i, n_lat_tiles - 1), 0)),
            pl.BlockSpec((tm, d), lambda i: (jnp.maximum(i - n_lat_tiles, 0), 0)),
            pl.BlockSpec((d, d), lambda i: (0, 0)),
            pl.BlockSpec((tm, d), lambda i: (i, 0)),
            _mod_spec(layer, 2, r, d, n_lat_tiles, tm, n_seq, b),
        ] + r_in,
        out_specs=[pl.BlockSpec((tm, d), lambda i: (i, 0))] + r_out,
        out_shape=[jax.ShapeDtypeStruct((n_rows, d), F32)] + r_shape,
        compiler_params=_params("arbitrary"),
        name="attn_out_proj_router",
    )(o_lat, o_ctx, w.astype(BF16), xa, mod, *r_args)
    return outs[0], outs[1:]


def _cproj_rows(x, sh_ref, sc_ref, g_ref, w_ref, gb_ref, u_ref, d):
    h = _norm_mod(x, g_ref[...], sh_ref[0], sc_ref[0])
    acc = jnp.dot(h.astype(BF16), w_ref[...], preferred_element_type=F32)
    gb_ref[...] = acc[:, :d].astype(BF16)
    u_ref[...] = (acc[:, d:2 * d] * acc[:, 2 * d:]).astype(BF16)


def _cproj_body(x_ref, sh_ref, sc_ref, g_ref, w_ref, gb_ref, u_ref, *, d):
    _cproj_rows(x_ref[...], sh_ref, sc_ref, g_ref, w_ref, gb_ref, u_ref, d)


def _conv_proj(xa, mod, r, layer, g, w, n_rows, n_lat_rows, n_seq, b):
    d = xa.shape[1]
    tm = ROW_TILE
    n_lat_tiles = n_lat_rows // tm
    return pl.pallas_call(
        functools.partial(_cproj_body, d=d),
        grid=(n_rows // tm,),
        in_specs=[
            pl.BlockSpec((tm, d), lambda i: (i, 0)),
            _mod_spec(layer, 0, r, d, n_lat_tiles, tm, n_seq, b),
            _mod_spec(layer, 1, r, d, n_lat_tiles, tm, n_seq, b),
            pl.BlockSpec((1, d), lambda i: (0, 0)),
            pl.BlockSpec((d, 3 * d), lambda i: (0, 0)),
        ],
        out_specs=[pl.BlockSpec((tm, d), lambda i: (i, 0)), pl.BlockSpec((tm, d), lambda i: (i, 0))],
        out_shape=[jax.ShapeDtypeStruct((n_rows, d), BF16), jax.ShapeDtypeStruct((n_rows, d), BF16)],
        compiler_params=_params("parallel"),
        name="conv_in_proj",
    )(xa, mod, mod, g[None, :], w.astype(BF16))


def _cout_body(u_ref, up_ref, un_ref, gb_ref, cw_ref, w_ref, x_ref, g1_ref, *refs, tm, n_lat_rows, n_seq, n_ctx, d,
               n_exp):
    router_in, out_ref, router_out = refs[:N_ROUTER_IN], refs[N_ROUTER_IN], refs[N_ROUTER_IN + 1:]
    i = pl.program_id(0)
    u = u_ref[...].astype(F32)
    prev_last = up_ref[BF16_SUBLANES - 1:BF16_SUBLANES, :].astype(F32)
    next_first = un_ref[0:1, :].astype(F32)
    row = lax.broadcasted_iota(I32, (tm, 1), 0)
    grow = i * tm + row
    is_lat = grow < n_lat_rows
    pos = jnp.where(is_lat, lax.rem(grow, n_seq), lax.rem(jnp.maximum(grow - n_lat_rows, 0), n_ctx))
    seq_len = jnp.where(is_lat, n_seq, n_ctx)
    um = jnp.where(row == 0, prev_last, pltpu.roll(u, 1, 0))
    um = jnp.where(pos == 0, 0.0, um)
    un = jnp.where(row == tm - 1, next_first, pltpu.roll(u, tm - 1, 0))
    un = jnp.where(pos == seq_len - 1, 0.0, un)
    cw = cw_ref[...]
    conv = um * cw[0:1] + u * cw[1:2] + un * cw[2:3]
    z = gb_ref[...].astype(F32) * conv
    y = jnp.dot(z.astype(BF16), w_ref[...], preferred_element_type=F32)
    out_ref[...] = x_ref[...] + g1_ref[0] * y
    _router_rows(out_ref[...], *router_in, *router_out, tm=tm, d=d, n_exp=n_exp)


def _conv_out(u, gb, conv_w, w, xa, mod, r, layer, g_moe, router_w, router_b, n_lat_rows, n_seq, n_ctx, b):
    n_rows, d = u.shape
    tm = ROW_TILE
    halo = BF16_SUBLANES
    n_halo_blocks = n_rows // halo
    r_in, r_args, r_out, r_shape, n_exp = _router_io(d, mod, r, layer, g_moe, router_w, router_b, n_rows, n_lat_rows,
                                                     n_seq, b, tm)
    body = functools.partial(_cout_body, tm=tm, n_lat_rows=n_lat_rows, n_seq=n_seq, n_ctx=n_ctx, d=d, n_exp=n_exp)
    outs = pl.pallas_call(
        body,
        grid=(n_rows // tm,),
        in_specs=[
            pl.BlockSpec((tm, d), lambda i: (i, 0)),
            pl.BlockSpec((halo, d), lambda i: (jnp.maximum(i * (tm // halo) - 1, 0), 0)),
            pl.BlockSpec((halo, d), lambda i: (jnp.minimum((i + 1) * (tm // halo), n_halo_blocks - 1), 0)),
            pl.BlockSpec((tm, d), lambda i: (i, 0)),
            pl.BlockSpec((3, d), lambda i: (0, 0)),
            pl.BlockSpec((d, d), lambda i: (0, 0)),
            pl.BlockSpec((tm, d), lambda i: (i, 0)),
            _mod_spec(layer, 2, r, d, n_lat_rows // tm, tm, n_seq, b),
        ] + r_in,
        out_specs=[pl.BlockSpec((tm, d), lambda i: (i, 0))] + r_out,
        out_shape=[jax.ShapeDtypeStruct((n_rows, d), F32)] + r_shape,
        compiler_params=_params("arbitrary"),
        name="conv_out_proj_router",
    )(u, u, u, gb, conv_w, w.astype(BF16), xa, mod, *r_args)
    return outs[0], outs[1:]


def _router_rows(x, sh_ref, sc_ref, g_ref, rw_hi_ref, rw_lo_ref, rb_ref, earlier_ref, h_ref, idx_ref, gate_ref,
                 rank_ref, count_ref, *, tm, d, n_exp):
    @pl.when(pl.program_id(0) == 0)
    def _():
        count_ref[...] = jnp.zeros_like(count_ref)

    h = _norm_mod(x, g_ref[...], sh_ref[0], sc_ref[0])
    half = d // LANES // 2
    for s in range(half):
        h_ref[pl.ds(s, tm, stride=half), :] = _pack_bf16_pair(h[:, s * LANES:(s + 1) * LANES],
                                                              h[:, (s + half) * LANES:(s + half + 1) * LANES])
    h_hi = h.astype(BF16)
    h_lo = (h - h_hi.astype(F32)).astype(BF16)
    logits = (lax.dot_general(rw_hi_ref[...], h_hi, _NT, preferred_element_type=F32)
              + lax.dot_general(rw_hi_ref[...], h_lo, _NT, preferred_element_type=F32)
              + lax.dot_general(rw_lo_ref[...], h_hi, _NT, preferred_element_type=F32)) + rb_ref[...]
    eidx = lax.broadcasted_iota(I32, (n_exp, tm), 0)
    tops, sels = [], []
    for _ in range(TOP_K):
        m = jnp.max(logits, axis=0, keepdims=True)
        sel = jnp.min(jnp.where(logits == m, eidx, n_exp), axis=0, keepdims=True)
        tops.append(m)
        sels.append(sel)
        logits = jnp.where(eidx == sel, -jnp.inf, logits)
    es = [jnp.exp(m - tops[0]) for m in tops]
    den = functools.reduce(jnp.add, es)
    idx_ref[...] = jnp.concatenate(sels, axis=0)
    gate_ref[...] = jnp.concatenate([e / den for e in es], axis=0)
    onehots = [(eidx == sel).astype(F32) for sel in sels]
    prefix = jnp.dot(jnp.concatenate(onehots, axis=0).astype(BF16), earlier_ref[...], preferred_element_type=F32)
    offset = count_ref[...]
    ranks = []
    for k in range(TOP_K):
        within = prefix[k * n_exp:(k + 1) * n_exp] + offset
        ranks.append(jnp.sum(onehots[k] * within, axis=0, keepdims=True))
        offset = offset + jnp.sum(onehots[k], axis=1, keepdims=True)
    rank_ref[...] = jnp.concatenate(ranks, axis=0).astype(I32)
    count_ref[...] = offset


N_ROUTER_IN = 7
N_ROUTER_OUT = 5


def _router_io(d, mod, r, layer, g, router_w, router_b, n_rows, n_lat_rows, n_seq, b, tm):
    n_exp = router_w.shape[1]
    slabs = d // LANES
    n_lat_tiles = n_lat_rows // tm
    rw_t = router_w.T
    rw_hi = rw_t.astype(BF16)
    rw_lo = (rw_t - rw_hi.astype(F32)).astype(BF16)
    earlier = (jnp.arange(tm)[:, None] < jnp.arange(tm)[None, :]).astype(BF16)
    in_specs = [
        _mod_spec(layer, 3, r, d, n_lat_tiles, tm, n_seq, b),
        _mod_spec(layer, 4, r, d, n_lat_tiles, tm, n_seq, b),
        pl.BlockSpec((1, d), lambda i: (0, 0)),
        pl.BlockSpec((n_exp, d), lambda i: (0, 0)),
        pl.BlockSpec((n_exp, d), lambda i: (0, 0)),
        pl.BlockSpec((n_exp, 1), lambda i: (0, 0)),
        pl.BlockSpec((tm, tm), lambda i: (0, 0)),
    ]
    args = [mod, mod, g[None, :], rw_hi, rw_lo, router_b[:, None], earlier]
    out_specs = [
        pl.BlockSpec((tm * slabs // 2, LANES), lambda i: (i, 0)),
        pl.BlockSpec((TOP_K, tm), lambda i: (0, i)),
        pl.BlockSpec((TOP_K, tm), lambda i: (0, i)),
        pl.BlockSpec((TOP_K, tm), lambda i: (0, i)),
        pl.BlockSpec((n_exp, 1), lambda i: (0, 0)),
    ]
    out_shape = [
        jax.ShapeDtypeStruct((n_rows * slabs // 2, LANES), U32),
        jax.ShapeDtypeStruct((TOP_K, n_rows), I32),
        jax.ShapeDtypeStruct((TOP_K, n_rows), F32),
        jax.ShapeDtypeStruct((TOP_K, n_rows), I32),
        jax.ShapeDtypeStruct((n_exp, 1), F32),
    ]
    assert len(in_specs) == N_ROUTER_IN and len(out_specs) == N_ROUTER_OUT
    return in_specs, args, out_specs, out_shape, n_exp


PLAN_TOKEN_TILE = 2048


def _plan_body(cols_ref, idx_ref, rank_ref, pos_ref, vis_ref, *, n_exp, tile, tm):
    cols = cols_ref[...]
    start, end, first_tile, visit_start, visit_end = (cols[:, c:c + 1] for c in range(5))
    idx = idx_ref[...]
    experts = lax.broadcasted_iota(I32, (n_exp, tm), 0)
    pos_ref[...] = rank_ref[...] + jnp.concatenate(
        [jnp.sum(jnp.where(experts == idx[k:k + 1], start, 0), axis=0, keepdims=True) for k in range(TOP_K)], axis=0)

    @pl.when(pl.program_id(0) == 0)
    def _():
        width = vis_ref.shape[1]
        v = lax.broadcasted_iota(I32, (1, width), 1)
        n_visits = jnp.max(visit_end, axis=0, keepdims=True)
        e_of = jnp.minimum(jnp.sum((v >= visit_end).astype(I32), axis=0, keepdims=True), n_exp - 1)
        mine = lax.broadcasted_iota(I32, (n_exp, width), 0) == e_of

        def pick(col):
            return jnp.sum(jnp.where(mine, col, 0), axis=0, keepdims=True)

        t_of = pick(first_tile) + v - pick(visit_start)
        lo = jnp.maximum(pick(start), t_of * tile) - t_of * tile
        hi = jnp.minimum(pick(end), (t_of + 1) * tile) - t_of * tile
        valid = v < n_visits
        at_last = v == jnp.maximum(n_visits - 1, 0)
        t_last = jnp.sum(jnp.where(at_last, t_of, 0), axis=1, keepdims=True)
        e_last = jnp.sum(jnp.where(at_last, e_of, 0), axis=1, keepdims=True)
        vis_ref[0:1, :] = jnp.where(valid, t_of, t_last)
        vis_ref[1:2, :] = jnp.where(valid, e_of, e_last)
        vis_ref[2:3, :] = jnp.where(valid, lo, 0)
        vis_ref[3:4, :] = jnp.where(valid, hi, 0)
        vis_ref[4:SUBLANES, :] = jnp.broadcast_to(n_visits, (SUBLANES - 4, width))


def _dispatch_plan(top_idx, rank, counts, tile, in_slabs, out_slabs):
    n_tok = top_idx.shape[1]
    n_exp = counts.shape[0]
    n_assign = TOP_K * n_tok
    n_tiles = n_assign // tile
    max_visits = n_tiles + n_exp - 1
    counts = counts[:, 0].astype(I32)
    end = jnp.cumsum(counts)
    start = end - counts
    first_tile = start // tile
    visits = jnp.where(counts > 0, (end - 1) // tile - first_tile + 1, 0)
    visit_end = jnp.cumsum(visits)
    cols = jnp.stack([start, end, first_tile, visit_end - visits, visit_end] + [end] * (SUBLANES - 5), axis=1)
    tm = math.gcd(n_tok, PLAN_TOKEN_TILE)
    width = _round_up(max_visits, LANES)
    pos, vis = pl.pallas_call(
        functools.partial(_plan_body, n_exp=n_exp, tile=tile, tm=tm),
        grid=(n_tok // tm,),
        in_specs=[
            pl.BlockSpec((n_exp, SUBLANES), lambda i: (0, 0)),
            pl.BlockSpec((TOP_K, tm), lambda i: (0, i)),
            pl.BlockSpec((TOP_K, tm), lambda i: (0, i)),
        ],
        out_specs=[pl.BlockSpec((TOP_K, tm), lambda i: (0, i)), pl.BlockSpec((SUBLANES, width), lambda i: (0, 0))],
        out_shape=[jax.ShapeDtypeStruct((TOP_K, n_tok), I32), jax.ShapeDtypeStruct((SUBLANES, width), I32)],
        compiler_params=_params("arbitrary"),
        name="moe_dispatch_plan",
    )(cols, top_idx, rank)
    slot_of_row = jnp.zeros((n_assign,), I32).at[pos.reshape(-1)].add(jnp.arange(n_assign, dtype=I32))
    row_token = ((slot_of_row % n_tok) * in_slabs).reshape(n_tiles, 1, tile)
    row_slot = (slot_of_row * out_slabs).reshape(n_tiles, 1, tile)
    return (row_token, row_slot, vis[0, :max_visits], vis[1, :max_visits], vis[2, :max_visits], vis[3, :max_visits],
            vis[4, :1])


def _expert_body(tile_ref, exp_ref, lo_ref, hi_ref, nv_ref,
                 idx_ref, idx0_ref, h_hbm, wg_ref, wl_ref, wd_ref, bg_ref, bl1_ref, bd_ref,
                 y_hbm, gbuf, xs, yacc, ybuf, wd_bf, gsem, ssem, *, tile, slabs):
    v = pl.program_id(0)
    n_visits = nv_ref[0]
    t = tile_ref[v]
    new_expert = jnp.logical_or(v == 0, exp_ref[jnp.maximum(v - 1, 0)] != exp_ref[v])
    first_visit = jnp.logical_or(v == 0, tile_ref[jnp.maximum(v - 1, 0)] != t)
    last_visit = jnp.logical_or(v == n_visits - 1, tile_ref[jnp.minimum(v + 1, pl.num_programs(0) - 1)] != t)
    valid = v < n_visits
    straight = jnp.logical_and(jnp.logical_and(first_visit, last_visit), t >= 1)
    g_cur = lax.rem(t, 2)
    y_cur = lax.rem(t, 3)
    y_prev = lax.rem(t + 2, 3)
    half = slabs // 2
    rows = tile * half

    own_tok, next_tok, prev_slot, own_slot = range(4)

    def start_gather(ref, which, g):
        for r in range(tile):
            src = pl.multiple_of(ref[0, 0, which * tile + r], half)
            pltpu.make_async_copy(h_hbm.at[pl.ds(src, half), :], gbuf.at[g, pl.ds(r * half, half), :],
                                  gsem.at[g]).start()

    def wait_gather(g):
        pltpu.make_async_copy(h_hbm.at[pl.ds(0, rows), :], gbuf.at[g], gsem.at[g]).wait()

    def start_scatter(which, yb):
        for r in range(tile):
            dst = pl.multiple_of(idx_ref[0, 0, which * tile + r], half)
            pltpu.make_async_copy(ybuf.at[yb, pl.ds(r * half, half), :], y_hbm.at[pl.ds(dst, half), :],
                                  ssem.at[yb]).start()

    def wait_scatter(yb):
        pltpu.make_async_copy(ybuf.at[yb], y_hbm.at[pl.ds(0, rows), :], ssem.at[yb]).wait()

    def begin_tile():
        start_gather(idx_ref, next_tok, 1 - g_cur)
        for s in range(half):
            lo, hi = _unpack_bf16_pair(gbuf[g_cur, pl.ds(s, tile, stride=half), :])
            xs[:, s * LANES:(s + 1) * LANES] = lo.astype(BF16)
            xs[:, (s + half) * LANES:(s + half + 1) * LANES] = hi.astype(BF16)

    def ffn():
        x = xs[...]
        gl = lax.dot_general(x, wg_ref[...], _NT, preferred_element_type=F32) + bg_ref[...]
        ln1 = lax.dot_general(x, wl_ref[...], _NT, preferred_element_type=F32) + bl1_ref[...]
        x_glu = jnp.minimum(gl, SWIGLU_LIMIT)
        x_lin1 = jnp.clip(ln1, 1.0 - SWIGLU_LIMIT, 1.0 + SWIGLU_LIMIT)
        act = x_glu * (1.0 + jnp.tanh((0.5 * SWIGLU_ALPHA) * x_glu)) * x_lin1
        return jnp.dot(act.astype(BF16), wd_bf[...], preferred_element_type=F32) + bd_ref[...]

    def stage(y_of_slab):
        for s in range(half):
            ybuf[y_cur, pl.ds(s, tile, stride=half), :] = _pack_bf16_pair(y_of_slab(s), y_of_slab(s + half))

    @pl.when(v == 0)
    def _():
        start_gather(idx0_ref, own_tok, 0)

    @pl.when(jnp.logical_and(valid, new_expert))
    def _():
        wd_bf[...] = (0.5 * wd_ref[...]).astype(BF16)

    @pl.when(jnp.logical_and(valid, first_visit))
    def _():
        wait_gather(g_cur)

        @pl.when(t >= 3)
        def _():
            wait_scatter(y_cur)

    @pl.when(jnp.logical_and(valid, straight))
    def _():
        start_scatter(prev_slot, y_prev)
        begin_tile()
        y = ffn()
        stage(lambda s: y[:, s * LANES:(s + 1) * LANES])

    @pl.when(jnp.logical_and(valid, jnp.logical_not(straight)))
    def _():
        @pl.when(first_visit)
        def _():
            @pl.when(t >= 1)
            def _():
                start_scatter(prev_slot, y_prev)

            begin_tile()

        y = ffn()

        @pl.when(first_visit)
        def _():
            yacc[...] = y

        @pl.when(jnp.logical_not(first_visit))
        def _():
            row = lax.broadcasted_iota(I32, (tile, 1), 0)
            mine = jnp.logical_and(row >= lo_ref[v], row < hi_ref[v])
            yacc[...] = jnp.where(mine, y, yacc[...])

        @pl.when(last_visit)
        def _():
            stage(lambda s: yacc[:, s * LANES:(s + 1) * LANES])

    @pl.when(v == n_visits - 1)
    def _():
        start_scatter(own_slot, y_cur)
        wait_gather(1 - g_cur)
        wait_scatter(y_cur)

        @pl.when(t >= 1)
        def _():
            wait_scatter(y_prev)

        @pl.when(t >= 2)
        def _():
            wait_scatter(lax.rem(t + 1, 3))


def _experts(h_slab, plan, layer, wg, wl, wd, bg, bl, bd, n_tok):
    row_token, row_slot, t_of, e_of, lo, hi, n_visits = plan
    n_tiles, _, tile = row_token.shape
    _, n_exp, f, d = wg.shape
    slabs = d // LANES
    max_visits = t_of.shape[0]

    def expert_block(v, tl, ex, lo_, hi_, nv):
        return (layer, ex[v], 0, 0)

    body = functools.partial(_expert_body, tile=tile, slabs=slabs)
    nxt = jnp.minimum(jnp.arange(n_tiles) + 1, n_tiles - 1)
    prv = jnp.maximum(jnp.arange(n_tiles) - 1, 0)
    tile_idx = jnp.concatenate([row_token, row_token[nxt], row_slot[prv], row_slot], axis=2)
    smem_tile = functools.partial(pl.BlockSpec, (1, 1, 4 * tile), memory_space=pltpu.SMEM)
    grid_spec = pltpu.PrefetchScalarGridSpec(
        num_scalar_prefetch=5,
        grid=(max_visits,),
        in_specs=[
            smem_tile(lambda v, tl, ex, lo_, hi_, nv: (tl[v], 0, 0)),
            smem_tile(lambda v, tl, ex, lo_, hi_, nv: (0, 0, 0)),
            pl.BlockSpec(memory_space=pl.ANY),
            pl.BlockSpec((None, None, f, d), expert_block),
            pl.BlockSpec((None, None, f, d), expert_block),
            pl.BlockSpec((None, None, f, d), expert_block),
            pl.BlockSpec((None, None, 1, f), expert_block),
            pl.BlockSpec((None, None, 1, f), expert_block),
            pl.BlockSpec((None, None, 1, d), expert_block),
        ],
        out_specs=pl.BlockSpec(memory_space=pl.ANY),
        scratch_shapes=[
            pltpu.VMEM((2, tile * slabs // 2, LANES), U32),
            pltpu.VMEM((tile, d), BF16),
            pltpu.VMEM((tile, d), F32),
            pltpu.VMEM((3, tile * slabs // 2, LANES), U32),
            pltpu.VMEM((f, d), BF16),
            pltpu.SemaphoreType.DMA((2,)),
            pltpu.SemaphoreType.DMA((3,)),
        ],
    )
    return pl.pallas_call(
        body,
        grid_spec=grid_spec,
        out_shape=jax.ShapeDtypeStruct((TOP_K * n_tok * slabs // 2, LANES), U32),
        compiler_params=_params("arbitrary"),
        name="moe_experts",
    )(t_of, e_of, lo, hi, n_visits, tile_idx, tile_idx, h_slab, wg, wl, wd, bg, bl, bd)


def _combine_rows(x_ref, g2_ref, gate_ref, y_refs, out_ref, tm, slabs):
    gates = gate_ref[...]
    g2 = g2_ref[0]
    half = slabs // 2
    for s in range(half):
        acc_lo = acc_hi = None
        for k in range(TOP_K):
            lo, hi = _unpack_bf16_pair(y_refs[k][pl.ds(s, tm, stride=half), :])
            acc_lo = gates[:, k:k + 1] * lo if acc_lo is None else acc_lo + gates[:, k:k + 1] * lo
            acc_hi = gates[:, k:k + 1] * hi if acc_hi is None else acc_hi + gates[:, k:k + 1] * hi
        for acc, slab in ((acc_lo, s), (acc_hi, s + half)):
            cols = slice(slab * LANES, (slab + 1) * LANES)
            out_ref[:, cols] = x_ref[:, cols] + g2[:, cols] * acc


def _combine_final_body(x_ref, g2_ref, gate_ref, *refs, tm, slabs):
    y_refs, fg_ref, out_ref = refs[:TOP_K], refs[TOP_K], refs[TOP_K + 1]
    _combine_rows(x_ref, g2_ref, gate_ref, y_refs, out_ref, tm, slabs)
    x = out_ref[...]
    ms = jnp.mean(x * x, axis=-1, keepdims=True)
    out_ref[...] = x * lax.rsqrt(ms + EPS) * fg_ref[...]


def _combine_qkv_body(x_ref, g2_ref, gate_ref, *refs, tm, slabs, d):
    y_refs = refs[:TOP_K]
    sh_ref, sc_ref, g_ref, w_ref, cos_ref, sin_ref, xo_ref, o_ref = refs[TOP_K:]
    _combine_rows(x_ref, g2_ref, gate_ref, y_refs, xo_ref, tm, slabs)
    _qkv_rows(xo_ref[...], sh_ref, sc_ref, g_ref, w_ref, cos_ref, sin_ref, o_ref, d)


def _combine_cproj_body(x_ref, g2_ref, gate_ref, *refs, tm, slabs, d):
    y_refs = refs[:TOP_K]
    sh_ref, sc_ref, g_ref, w_ref, xo_ref, gb_ref, u_ref = refs[TOP_K:]
    _combine_rows(x_ref, g2_ref, gate_ref, y_refs, xo_ref, tm, slabs)
    _cproj_rows(xo_ref[...], sh_ref, sc_ref, g_ref, w_ref, gb_ref, u_ref, d)


def _combine(xa, y_slab, gates_t, mod, r, layer, n_rows, n_moe_rows, n_lat_rows, n_seq, b, final_g=None, nxt=None):
    d = xa.shape[1]
    slabs = d // LANES
    tm = ROW_TILE if nxt is None else FUSED_ROW_TILE
    n_tiles = n_rows // tm
    n_lat_tiles = n_lat_rows // tm
    y_tiles_per_k = n_moe_rows // tm
    y_specs = [pl.BlockSpec((tm * slabs // 2, LANES), functools.partial(lambda i, k: (k * y_tiles_per_k + i, 0), k=k))
               for k in range(TOP_K)]
    row_spec = pl.BlockSpec((tm, d), lambda i: (i, 0))
    in_specs = [row_spec, _mod_spec(layer, 5, r, d, n_lat_tiles, tm, n_seq, b),
                pl.BlockSpec((tm, TOP_K), lambda i: (i, 0))] + y_specs
    args = [xa, mod, gates_t] + [y_slab] * TOP_K
    x_shape = jax.ShapeDtypeStruct((n_rows, d), F32)
    if nxt is None:
        return pl.pallas_call(
            functools.partial(_combine_final_body, tm=tm, slabs=slabs),
            grid=(n_tiles,),
            in_specs=in_specs + [pl.BlockSpec((1, d), lambda i: (0, 0))],
            out_specs=row_spec,
            out_shape=x_shape,
            compiler_params=_params("parallel"),
            name="moe_combine_final",
        )(*args, final_g[None, :])
    kind, g, w = nxt
    in_specs += [_mod_spec(layer + 1, 0, r, d, n_lat_tiles, tm, n_seq, b),
                 _mod_spec(layer + 1, 1, r, d, n_lat_tiles, tm, n_seq, b),
                 pl.BlockSpec((1, d), lambda i: (0, 0)),
                 pl.BlockSpec((d, 3 * d), lambda i: (0, 0))]
    args += [mod, mod, g[None, :], w.astype(BF16)]
    if kind == "attn":
        tiles_per_seq = n_seq // tm
        cos_t, sin_t = _rope_tables(n_seq, tm)

        def rope_index(i):
            return (jnp.where(i < n_lat_tiles, i % tiles_per_seq, tiles_per_seq), 0)

        return pl.pallas_call(
            functools.partial(_combine_qkv_body, tm=tm, slabs=slabs, d=d),
            grid=(n_tiles,),
            in_specs=in_specs + [pl.BlockSpec((tm, LANES), rope_index)] * 2,
            out_specs=[row_spec, pl.BlockSpec((tm, 3 * d), lambda i: (i, 0))],
            out_shape=[x_shape, jax.ShapeDtypeStruct((n_rows, 3 * d), BF16)],
            compiler_params=_params("parallel"),
            name="moe_combine_qkv_proj",
        )(*args, cos_t, sin_t)
    xa_new, gb, u = pl.pallas_call(
        functools.partial(_combine_cproj_body, tm=tm, slabs=slabs, d=d),
        grid=(n_tiles,),
        in_specs=in_specs,
        out_specs=[row_spec, row_spec, row_spec],
        out_shape=[x_shape, jax.ShapeDtypeStruct((n_rows, d), BF16), jax.ShapeDtypeStruct((n_rows, d), BF16)],
        compiler_params=_params("parallel"),
        name="moe_combine_conv_in_proj",
    )(*args)
    return xa_new, (gb, u)


def _wsplit_body(w_ref, wg_ref, wl_ref, ts):
    d, f2 = w_ref.shape
    half = LANES // 2
    for c in range(f2 // LANES):
        for j in range(d // LANES):
            blk = (c * (d // LANES) + j) * LANES
            ts[pl.ds(blk, LANES), :] = w_ref[j * LANES:(j + 1) * LANES, c * LANES:(c + 1) * LANES].T
            rows = slice(c * half, (c + 1) * half)
            cols = slice(j * LANES, (j + 1) * LANES)
            wg_ref[rows, cols] = ts[pl.ds(blk, half, stride=2), :].astype(BF16)
            wl_ref[rows, cols] = ts[pl.ds(blk + 1, half, stride=2), :].astype(BF16)


def _split_gate_up(w_gu):
    depth, n_exp, d, f2 = w_gu.shape
    f = f2 // 2
    out = jax.ShapeDtypeStruct((depth, n_exp, f, d), BF16)
    return pl.pallas_call(
        _wsplit_body,
        grid=(depth, n_exp),
        in_specs=[pl.BlockSpec((None, None, d, f2), lambda l, e: (l, e, 0, 0))],
        out_specs=[pl.BlockSpec((None, None, f, d), lambda l, e: (l, e, 0, 0))] * 2,
        out_shape=[out, out],
        scratch_shapes=[pltpu.VMEM((d * f2 // LANES, LANES), F32)],
        compiler_params=_params("parallel", "parallel"),
        name="moe_split_gate_up",
    )(w_gu)


def _moe_experts(routed, layer, expert_w, n_rows, d):
    h_slab, top_idx, gates, rank, counts = routed
    slabs = d // LANES
    plan = _dispatch_plan(top_idx, rank, counts, MOE_TILE, slabs // 2, slabs // 2)
    return _experts(h_slab, plan, layer, *expert_w, n_rows), gates.T


def kernel(x, c, ctx, c_ctx, mod_w, mod_b, norm_g, attn_w_qkv, attn_w_o, attn_lambda, attn_subln_g, conv_w_in, conv_w, conv_w_out, router_w, router_b, moe_w_gu, moe_b_gu, moe_w_down, moe_b_down, final_g):
    b, n_seq, d = x.shape
    n_ctx = ctx.shape[1]
    depth = mod_w.shape[0]
    n_lat_rows = b * n_seq
    n_all_rows = n_lat_rows + b * n_ctx
    assert d % LANES == 0 and d == moe_w_down.shape[-1]
    assert n_seq % ROW_TILE == 0 and (b * n_ctx) % ROW_TILE == 0 and n_seq % GRID_W == 0
    assert ROW_TILE % n_ctx == 0 or n_ctx % ROW_TILE == 0
    assert n_lat_rows % n_ctx == 0 and n_ctx % BF16_SUBLANES == 0
    assert (TOP_K * n_lat_rows) % MOE_TILE == 0 and (TOP_K * n_all_rows) % MOE_TILE == 0

    xa = jnp.concatenate([x.reshape(n_lat_rows, d), ctx.reshape(b * n_ctx, d)], axis=0)
    mod, r = _modulation(c, c_ctx, mod_w, mod_b)
    wg, wl = _split_gate_up(moe_w_gu)
    expert_w = (wg, wl, moe_w_down, moe_b_gu[:, :, None, 0::2], moe_b_gu[:, :, None, 1::2] + 1.0,
                moe_b_down[:, :, None, :])
    proj = None
    for i in range(depth):
        update_ctx = i < depth - 1
        attn_layer = i % N_MIXERS == 0
        j = i // N_MIXERS
        n_out = n_all_rows if update_ctx else n_lat_rows
        if attn_layer:
            lam_init = 0.8 - 0.6 * math.exp(-0.3 * i)
            qkv = proj if proj is not None else _qkv_proj(xa, mod, r, i, norm_g[i, 0], attn_w_qkv[j], n_all_rows,
                                                           n_lat_rows, n_seq, b)
            o_lat, o_ctx = _attention(qkv, attn_lambda[j], attn_subln_g[j], lam_init, b, n_seq, n_ctx, n_lat_rows,
                                      update_ctx)
            xa, routed = _out_proj(o_lat, o_ctx, attn_w_o[j], xa, mod, r, i, norm_g[i, 1], router_w[i], router_b[i],
                                   n_lat_rows, n_seq, b)
        else:
            gb, u = proj if proj is not None else _conv_proj(xa, mod, r, i, norm_g[i, 0], conv_w_in[j], n_out,
                                                             n_lat_rows, n_seq, b)
            xa, routed = _conv_out(u, gb, conv_w[j], conv_w_out[j], xa, mod, r, i, norm_g[i, 1], router_w[i],
                                   router_b[i], n_lat_rows, n_seq, n_ctx, b)
        y_slab, gates_t = _moe_experts(routed, i, expert_w, n_out, d)
        if i == depth - 1:
            xa = _combine(xa, y_slab, gates_t, mod, r, i, n_out, n_out, n_lat_rows, n_seq, b, final_g=final_g)
        else:
            nxt_attn = (i + 1) % N_MIXERS == 0
            jn = (i + 1) // N_MIXERS
            n_next = n_all_rows if (nxt_attn or i + 1 < depth - 1) else n_lat_rows
            nxt = ("attn", norm_g[i + 1, 0], attn_w_qkv[jn]) if nxt_attn else ("conv", norm_g[i + 1, 0], conv_w_in[jn])
            xa, proj = _combine(xa, y_slab, gates_t, mod, r, i, n_next, n_out, n_lat_rows, n_seq, b, nxt=nxt)
    return xa.reshape(b, n_seq, d)
```

```python
import functools
import math

import jax
import jax.numpy as jnp
from jax import lax
from jax.experimental import pallas as pl
from jax.experimental.pallas import tpu as pltpu

F32 = jnp.float32
BF16 = jnp.bfloat16
I32 = jnp.int32
U32 = jnp.uint32

GRID_W = 64
HEAD_DIM = 64
ROPE_BASE = 10000.0
N_MIXERS = 2
TOP_K = 4
SWIGLU_LIMIT = 7.0
SWIGLU_ALPHA = 1.702
N_MOD = 6
EPS = 1e-5

LANES = 128
SUBLANES = 8
BF16_SUBLANES = 16
VMEM_LIMIT_BYTES = 48 * 1024 * 1024

ROW_TILE = 512
FUSED_ROW_TILE = 512
ATTN_Q_TILE = 1024
MOE_TILE = 256

_NT = (((1,), (1,)), ((), ()))


def _round_up(a, b):
    return (a + b - 1) // b * b


def _params(*sem):
    return pltpu.CompilerParams(dimension_semantics=sem, vmem_limit_bytes=VMEM_LIMIT_BYTES)


def _pack_bf16_pair(lo, hi):
    lo_bits = lax.bitcast_convert_type(lo.astype(jnp.bfloat16).astype(F32), U32)
    hi_bits = lax.bitcast_convert_type(hi.astype(jnp.bfloat16).astype(F32), U32)
    return hi_bits | (lo_bits >> 16)


def _unpack_bf16_pair(w):
    return lax.bitcast_convert_type(w << 16, F32), lax.bitcast_convert_type(w & jnp.uint32(0xFFFF0000), F32)


def _norm_mod(x, g, shift, scale):
    ms = jnp.mean(x * x, axis=-1, keepdims=True)
    return (x * lax.rsqrt(ms + EPS) * g) * (1.0 + scale) + shift


def _mod_body(c_ref, w_ref, b_ref, o_ref):
    c = c_ref[...]
    s = c * jax.nn.sigmoid(c)
    o_ref[0] = jnp.dot(s, w_ref[0], preferred_element_type=F32, precision=lax.Precision.HIGHEST) + b_ref[0]


def _modulation(c, c_ctx, mod_w, mod_b):
    depth, d, nd = mod_w.shape
    b = c.shape[0]
    r = _round_up(b + 1, SUBLANES)
    cv = jnp.concatenate([c, c_ctx[None, :], jnp.zeros((r - b - 1, d), F32)], axis=0)
    tn = d
    out = pl.pallas_call(
        _mod_body,
        grid=(depth, nd // tn),
        in_specs=[
            pl.BlockSpec((r, d), lambda l, j: (0, 0)),
            pl.BlockSpec((1, d, tn), lambda l, j: (l, 0, j)),
            pl.BlockSpec((1, 1, tn), lambda l, j: (l, 0, j)),
        ],
        out_specs=pl.BlockSpec((1, r, tn), lambda l, j: (l, 0, j)),
        out_shape=jax.ShapeDtypeStruct((depth, r, nd), F32),
        compiler_params=_params("parallel", "parallel"),
        name="modulation",
    )(cv, mod_w, mod_b.reshape(depth, 1, nd))
    return out.reshape(depth * r, 1, nd), r


def _mod_spec(layer, chunk, r, d, n_lat_tiles, tm, n_seq, b):
    def index(i, *_):
        row = jnp.where(i < n_lat_tiles, (i * tm) // n_seq, b)
        return (layer * r + row, 0, chunk)

    return pl.BlockSpec((1, 1, d), index)


def _rope_tables(n, tm):
    rows = n // GRID_W
    row_pos = jnp.repeat(jnp.arange(rows, dtype=F32), GRID_W)
    col_pos = jnp.tile(jnp.arange(GRID_W, dtype=F32), rows)
    axis_dims = HEAD_DIM // 2
    inv_freq = ROPE_BASE ** (-jnp.arange(0, axis_dims, 2, dtype=F32) / axis_dims)
    ang = jnp.concatenate([row_pos[:, None] * inv_freq, col_pos[:, None] * inv_freq], axis=-1)
    cos, sin = jnp.cos(ang), jnp.sin(ang)
    reps = LANES // (HEAD_DIM // 2)
    cos_t = jnp.tile(cos, (1, reps))
    sign = jnp.where((jnp.arange(LANES) % HEAD_DIM) < HEAD_DIM // 2, -1.0, 1.0).astype(F32)
    sin_t = jnp.tile(sin, (1, reps)) * sign
    cos_t = jnp.concatenate([cos_t, jnp.ones((tm, LANES), F32)], axis=0)
    sin_t = jnp.concatenate([sin_t, jnp.zeros((tm, LANES), F32)], axis=0)
    return cos_t, sin_t


def _qkv_rows(x, sh_ref, sc_ref, g_ref, w_ref, cos_ref, sin_ref, o_ref, d, rows=slice(None)):
    h = _norm_mod(x, g_ref[...], sh_ref[0], sc_ref[0])
    acc = jnp.dot(h.astype(BF16), w_ref[...], preferred_element_type=F32)
    cos = cos_ref[rows, :]
    sin = sin_ref[rows, :]
    lane = lax.broadcasted_iota(I32, (1, LANES), 1)
    first_half = (lane % HEAD_DIM) < HEAD_DIM // 2
    q_scale = HEAD_DIM**-0.5 * math.log2(math.e)
    for gi in range(2 * d // LANES):
        xg = acc[:, gi * LANES:(gi + 1) * LANES]
        partner = jnp.where(first_half, pltpu.roll(xg, LANES - HEAD_DIM // 2, 1), pltpu.roll(xg, HEAD_DIM // 2, 1))
        r = xg * cos + partner * sin
        if gi < d // LANES:
            r = r * q_scale
        o_ref[rows, gi * LANES:(gi + 1) * LANES] = r.astype(BF16)
    o_ref[rows, 2 * d:] = acc[:, 2 * d:].astype(BF16)


def _qkv_body(xl_ref, xc_ref, sh_ref, sc_ref, g_ref, w_ref, cos_ref, sin_ref, o_ref, *, d, n_lat_tiles):
    @pl.when(pl.program_id(0) < n_lat_tiles)
    def _():
        _qkv_rows(xl_ref[...], sh_ref, sc_ref, g_ref, w_ref, cos_ref, sin_ref, o_ref, d)

    @pl.when(pl.program_id(0) >= n_lat_tiles)
    def _():
        _qkv_rows(xc_ref[...], sh_ref, sc_ref, g_ref, w_ref, cos_ref, sin_ref, o_ref, d)


def _lat_ctx_specs(tm, d, n_lat_tiles):
    return [pl.BlockSpec((tm, d), lambda i: (jnp.minimum(i, n_lat_tiles - 1), 0)),
            pl.BlockSpec((tm, d), lambda i: (jnp.maximum(i - n_lat_tiles, 0), 0))]


def _qkv_proj(x_lat, x_ctx, mod, r, layer, g, w, n_rows, n_lat_rows, n_seq, b):
    d = x_lat.shape[1]
    tm = ROW_TILE
    n_lat_tiles = n_lat_rows // tm
    tiles_per_seq = n_seq // tm
    cos_t, sin_t = _rope_tables(n_seq, tm)

    def rope_index(i):
        return (jnp.where(i < n_lat_tiles, i % tiles_per_seq, tiles_per_seq), 0)

    return pl.pallas_call(
        functools.partial(_qkv_body, d=d, n_lat_tiles=n_lat_tiles),
        grid=(n_rows // tm,),
        in_specs=_lat_ctx_specs(tm, d, n_lat_tiles) + [
            _mod_spec(layer, 0, r, d, n_lat_tiles, tm, n_seq, b),
            _mod_spec(layer, 1, r, d, n_lat_tiles, tm, n_seq, b),
            pl.BlockSpec((1, d), lambda i: (0, 0)),
            pl.BlockSpec((d, 3 * d), lambda i: (0, 0)),
            pl.BlockSpec((tm, LANES), rope_index),
            pl.BlockSpec((tm, LANES), rope_index),
        ],
        out_specs=pl.BlockSpec((tm, 3 * d), lambda i: (i, 0)),
        out_shape=jax.ShapeDtypeStruct((n_rows, 3 * d), BF16),
        compiler_params=_params("parallel"),
        name="qkv_proj",
    )(x_lat, x_ctx, mod, mod, g[None, :], w.astype(BF16), cos_t, sin_t)


def _diff_attention(lam_ref, sg_ref, q_ref, k_refs, va_refs, o_ref, lam_init):
    q = q_ref[...]
    lane = lax.broadcasted_iota(I32, (1, LANES), 1)
    lp = lam_ref[...]
    lam = (jnp.exp(jnp.sum(lp[0:1] * lp[1:2], axis=-1, keepdims=True))
           - jnp.exp(jnp.sum(lp[2:3] * lp[3:4], axis=-1, keepdims=True)) + lam_init)
    scores = []
    for comp in range(2):
        in_comp = (lane < HEAD_DIM) if comp == 0 else (lane >= HEAD_DIM)
        qc = jnp.where(in_comp, q, jnp.zeros_like(q))
        scores.append([lax.dot_general(qc, k[...], _NT, preferred_element_type=F32) for k in k_refs])
    outs = []
    for ss in scores:
        m = functools.reduce(jnp.maximum, [jnp.max(s, axis=-1, keepdims=True) for s in ss])
        acc = None
        for s, va in zip(ss, va_refs):
            part = jnp.dot(jnp.exp2(s - m).astype(BF16), va[...], preferred_element_type=F32)
            acc = part if acc is None else acc + part
        outs.append(acc[:, :LANES] / acc[:, LANES:])
    o = outs[0] - lam * outs[1]
    ms = jnp.mean(o * o, axis=-1, keepdims=True)
    o_ref[...] = (o * lax.rsqrt(ms + EPS) * sg_ref[...] * (1.0 - lam_init)).astype(BF16)


def _fill_value_ones(va, v_ref):
    va[:, :LANES] = v_ref[...]
    va[:, LANES:] = jnp.ones((va.shape[0], LANES), BF16)


def _attn_lat_body(lam_ref, sg_ref, q_ref, kc_ref, kl_ref, vc_ref, vl_ref, o_ref, vac, val, *, lam_init):
    @pl.when(pl.program_id(2) == 0)
    def _():
        _fill_value_ones(vac, vc_ref)
        _fill_value_ones(val, vl_ref)

    _diff_attention(lam_ref, sg_ref, q_ref, [kc_ref, kl_ref], [vac, val], o_ref, lam_init)


def _attn_ctx_body(lam_ref, sg_ref, q_ref, kc_ref, vc_ref, o_ref, vac, *, lam_init):
    _fill_value_ones(vac, vc_ref)
    _diff_attention(lam_ref, sg_ref, q_ref, [kc_ref], [vac], o_ref, lam_init)


def _attention(qkv, lam_p, subln_g, lam_init, b, n_seq, n_ctx, n_lat_rows, update_ctx):
    d = qkv.shape[1] // 3
    heads = d // LANES
    tq = min(ATTN_Q_TILE, n_seq)
    assert n_seq % tq == 0
    q_tiles = n_seq // tq
    ctx_blk0 = n_lat_rows // n_ctx
    small = [pl.BlockSpec((4, HEAD_DIM), lambda *_: (0, 0)), pl.BlockSpec((1, LANES), lambda *_: (0, 0))]
    o_lat = pl.pallas_call(
        functools.partial(_attn_lat_body, lam_init=lam_init),
        grid=(b, heads, q_tiles),
        in_specs=small + [
            pl.BlockSpec((tq, LANES), lambda bi, h, i: (bi * q_tiles + i, h)),
            pl.BlockSpec((n_ctx, LANES), lambda bi, h, i: (ctx_blk0 + bi, heads + h)),
            pl.BlockSpec((n_seq, LANES), lambda bi, h, i: (bi, heads + h)),
            pl.BlockSpec((n_ctx, LANES), lambda bi, h, i: (ctx_blk0 + bi, 2 * heads + h)),
            pl.BlockSpec((n_seq, LANES), lambda bi, h, i: (bi, 2 * heads + h)),
        ],
        out_specs=pl.BlockSpec((tq, LANES), lambda bi, h, i: (bi * q_tiles + i, h)),
        out_shape=jax.ShapeDtypeStruct((n_lat_rows, d), BF16),
        scratch_shapes=[pltpu.VMEM((n_ctx, 2 * LANES), BF16), pltpu.VMEM((n_seq, 2 * LANES), BF16)],
        compiler_params=_params("parallel", "parallel", "arbitrary"),
        name="attn_latent",
    )(lam_p, subln_g[None, :], qkv, qkv, qkv, qkv, qkv)
    if not update_ctx:
        return o_lat, None
    o_ctx = pl.pallas_call(
        functools.partial(_attn_ctx_body, lam_init=lam_init),
        grid=(b, heads),
        in_specs=small + [
            pl.BlockSpec((n_ctx, LANES), lambda bi, h: (ctx_blk0 + bi, h)),
            pl.BlockSpec((n_ctx, LANES), lambda bi, h: (ctx_blk0 + bi, heads + h)),
            pl.BlockSpec((n_ctx, LANES), lambda bi, h: (ctx_blk0 + bi, 2 * heads + h)),
        ],
        out_specs=pl.BlockSpec((n_ctx, LANES), lambda bi, h: (bi, h)),
        out_shape=jax.ShapeDtypeStruct((b * n_ctx, d), BF16),
        scratch_shapes=[pltpu.VMEM((n_ctx, 2 * LANES), BF16)],
        compiler_params=_params("parallel", "parallel"),
        name="attn_context",
    )(lam_p, subln_g[None, :], qkv, qkv, qkv)
    return o_lat, o_ctx


def _oproj_body(ol_ref, oc_ref, w_ref, xl_ref, xc_ref, g1_ref, *refs, n_lat_tiles, tm, d, n_exp):
    router_in, out_ref, router_out = refs[:N_ROUTER_IN], refs[N_ROUTER_IN], refs[N_ROUTER_IN + 1:]

    def finish(o_ref, x_ref):
        y = jnp.dot(o_ref[...], w_ref[...], preferred_element_type=F32)
        out_ref[...] = x_ref[...] + g1_ref[0] * y

    @pl.when(pl.program_id(0) < n_lat_tiles)
    def _():
        finish(ol_ref, xl_ref)

    @pl.when(pl.program_id(0) >= n_lat_tiles)
    def _():
        finish(oc_ref, xc_ref)

    _router_rows(out_ref[...], *router_in, *router_out, tm=tm, d=d, n_exp=n_exp)


def _out_proj(o_lat, o_ctx, w, x_lat, x_ctx, mod, r, layer, g_moe, router_w, router_b, n_lat_rows, n_seq, b):
    d = o_lat.shape[1]
    tm = ROW_TILE
    n_lat_tiles = n_lat_rows // tm
    if o_ctx is None:
        o_ctx = o_lat
    n_rows = n_lat_rows + (0 if o_ctx is o_lat else o_ctx.shape[0])
    r_in, r_args, r_out, r_shape, n_exp = _router_io(d, mod, r, layer, g_moe, router_w, router_b, n_rows, n_lat_rows,
                                                     n_seq, b, tm)
    if x_ctx is x_lat:
        x_specs = [pl.BlockSpec((tm, d), lambda i: (i, 0))] * 2
    else:
        x_specs = _lat_ctx_specs(tm, d, n_lat_tiles)
    outs = pl.pallas_call(
        functools.partial(_oproj_body, n_lat_tiles=n_lat_tiles, tm=tm, d=d, n_exp=n_exp),
        grid=(n_rows // tm,),
        in_specs=_lat_ctx_specs(tm, d, n_lat_tiles) + [pl.BlockSpec((d, d), lambda i: (0, 0))] + x_specs + [
            _mod_spec(layer, 2, r, d, n_lat_tiles, tm, n_seq, b),
        ] + r_in,
        out_specs=[pl.BlockSpec((tm, d), lambda i: (i, 0))] + r_out,
        out_shape=[jax.ShapeDtypeStruct((n_rows, d), F32)] + r_shape,
        compiler_params=_params("arbitrary"),
        name="attn_out_proj_router",
    )(o_lat, o_ctx, w.astype(BF16), x_lat, x_ctx, mod, *r_args)
    return outs[0], outs[1:]


def _cproj_rows(x, sh_ref, sc_ref, g_ref, w_ref, gb_ref, u_ref, d, rows=slice(None)):
    h = _norm_mod(x, g_ref[...], sh_ref[0], sc_ref[0])
    acc = jnp.dot(h.astype(BF16), w_ref[...], preferred_element_type=F32)
    gb_ref[rows, :] = acc[:, :d].astype(BF16)
    u_ref[rows, :] = (acc[:, d:2 * d] * acc[:, 2 * d:]).astype(BF16)


def _cproj_body(x_ref, sh_ref, sc_ref, g_ref, w_ref, gb_ref, u_ref, *, d):
    _cproj_rows(x_ref[...], sh_ref, sc_ref, g_ref, w_ref, gb_ref, u_ref, d)


def _conv_proj(xa, mod, r, layer, g, w, n_rows, n_lat_rows, n_seq, b):
    d = xa.shape[1]
    tm = ROW_TILE
    n_lat_tiles = n_lat_rows // tm
    return pl.pallas_call(
        functools.partial(_cproj_body, d=d),
        grid=(n_rows // tm,),
        in_specs=[
            pl.BlockSpec((tm, d), lambda i: (i, 0)),
            _mod_spec(layer, 0, r, d, n_lat_tiles, tm, n_seq, b),
            _mod_spec(layer, 1, r, d, n_lat_tiles, tm, n_seq, b),
            pl.BlockSpec((1, d), lambda i: (0, 0)),
            pl.BlockSpec((d, 3 * d), lambda i: (0, 0)),
        ],
        out_specs=[pl.BlockSpec((tm, d), lambda i: (i, 0)), pl.BlockSpec((tm, d), lambda i: (i, 0))],
        out_shape=[jax.ShapeDtypeStruct((n_rows, d), BF16), jax.ShapeDtypeStruct((n_rows, d), BF16)],
        compiler_params=_params("parallel"),
        name="conv_in_proj",
    )(xa, mod, mod, g[None, :], w.astype(BF16))


def _cout_body(u_ref, up_ref, un_ref, gb_ref, cw_ref, w_ref, x_ref, g1_ref, *refs, tm, n_lat_rows, n_seq, n_ctx, d,
               n_exp):
    router_in, out_ref, router_out = refs[:N_ROUTER_IN], refs[N_ROUTER_IN], refs[N_ROUTER_IN + 1:]
    i = pl.program_id(0)
    u = u_ref[...].astype(F32)
    prev_last = up_ref[BF16_SUBLANES - 1:BF16_SUBLANES, :].astype(F32)
    next_first = un_ref[0:1, :].astype(F32)
    row = lax.broadcasted_iota(I32, (tm, 1), 0)
    grow = i * tm + row
    is_lat = grow < n_lat_rows
    pos = jnp.where(is_lat, lax.rem(grow, n_seq), lax.rem(jnp.maximum(grow - n_lat_rows, 0), n_ctx))
    seq_len = jnp.where(is_lat, n_seq, n_ctx)
    um = jnp.where(row == 0, prev_last, pltpu.roll(u, 1, 0))
    um = jnp.where(pos == 0, 0.0, um)
    un = jnp.where(row == tm - 1, next_first, pltpu.roll(u, tm - 1, 0))
    un = jnp.where(pos == seq_len - 1, 0.0, un)
    cw = cw_ref[...]
    conv = um * cw[0:1] + u * cw[1:2] + un * cw[2:3]
    z = gb_ref[...].astype(F32) * conv
    y = jnp.dot(z.astype(BF16), w_ref[...], preferred_element_type=F32)
    out_ref[...] = x_ref[...] + g1_ref[0] * y
    _router_rows(out_ref[...], *router_in, *router_out, tm=tm, d=d, n_exp=n_exp)


def _conv_out(u, gb, conv_w, w, xa, mod, r, layer, g_moe, router_w, router_b, n_lat_rows, n_seq, n_ctx, b):
    n_rows, d = u.shape
    tm = ROW_TILE
    halo = BF16_SUBLANES
    n_halo_blocks = n_rows // halo
    r_in, r_args, r_out, r_shape, n_exp = _router_io(d, mod, r, layer, g_moe, router_w, router_b, n_rows, n_lat_rows,
                                                     n_seq, b, tm)
    body = functools.partial(_cout_body, tm=tm, n_lat_rows=n_lat_rows, n_seq=n_seq, n_ctx=n_ctx, d=d, n_exp=n_exp)
    outs = pl.pallas_call(
        body,
        grid=(n_rows // tm,),
        in_specs=[
            pl.BlockSpec((tm, d), lambda i: (i, 0)),
            pl.BlockSpec((halo, d), lambda i: (jnp.maximum(i * (tm // halo) - 1, 0), 0)),
            pl.BlockSpec((halo, d), lambda i: (jnp.minimum((i + 1) * (tm // halo), n_halo_blocks - 1), 0)),
            pl.BlockSpec((tm, d), lambda i: (i, 0)),
            pl.BlockSpec((3, d), lambda i: (0, 0)),
            pl.BlockSpec((d, d), lambda i: (0, 0)),
            pl.BlockSpec((tm, d), lambda i: (i, 0)),
            _mod_spec(layer, 2, r, d, n_lat_rows // tm, tm, n_seq, b),
        ] + r_in,
        out_specs=[pl.BlockSpec((tm, d), lambda i: (i, 0))] + r_out,
        out_shape=[jax.ShapeDtypeStruct((n_rows, d), F32)] + r_shape,
        compiler_params=_params("arbitrary"),
        name="conv_out_proj_router",
    )(u, u, u, gb, conv_w, w.astype(BF16), xa, mod, *r_args)
    return outs[0], outs[1:]


def _router_rows(x, sh_ref, sc_ref, g_ref, rw_hi_ref, rw_lo_ref, rb_ref, earlier_ref, h_ref, idx_ref, gate_ref,
                 rank_ref, count_ref, *, tm, d, n_exp):
    @pl.when(pl.program_id(0) == 0)
    def _():
        count_ref[...] = jnp.zeros_like(count_ref)

    h = _norm_mod(x, g_ref[...], sh_ref[0], sc_ref[0])
    half = d // LANES // 2
    for s in range(half):
        h_ref[pl.ds(s, tm, stride=half), :] = _pack_bf16_pair(h[:, s * LANES:(s + 1) * LANES],
                                                              h[:, (s + half) * LANES:(s + half + 1) * LANES])
    h_hi = h.astype(BF16)
    h_lo = (h - h_hi.astype(F32)).astype(BF16)
    logits = (lax.dot_general(rw_hi_ref[...], h_hi, _NT, preferred_element_type=F32)
              + lax.dot_general(rw_hi_ref[...], h_lo, _NT, preferred_element_type=F32)
              + lax.dot_general(rw_lo_ref[...], h_hi, _NT, preferred_element_type=F32)) + rb_ref[...]
    eidx = lax.broadcasted_iota(I32, (n_exp, tm), 0)
    tops, sels = [], []
    for _ in range(TOP_K):
        m = jnp.max(logits, axis=0, keepdims=True)
        sel = jnp.min(jnp.where(logits == m, eidx, n_exp), axis=0, keepdims=True)
        tops.append(m)
        sels.append(sel)
        logits = jnp.where(eidx == sel, -jnp.inf, logits)
    es = [jnp.exp(m - tops[0]) for m in tops]
    den = functools.reduce(jnp.add, es)
    idx_ref[...] = jnp.concatenate(sels, axis=0)
    gate_ref[...] = jnp.concatenate([e / den for e in es], axis=0)
    onehots = [(eidx == sel).astype(F32) for sel in sels]
    prefix = jnp.dot(jnp.concatenate(onehots, axis=0).astype(BF16), earlier_ref[...], preferred_element_type=F32)
    offset = count_ref[...]
    ranks = []
    for k in range(TOP_K):
        within = prefix[k * n_exp:(k + 1) * n_exp] + offset
        ranks.append(jnp.sum(onehots[k] * within, axis=0, keepdims=True))
        offset = offset + jnp.sum(onehots[k], axis=1, keepdims=True)
    rank_ref[...] = jnp.concatenate(ranks, axis=0).astype(I32)
    count_ref[...] = offset


N_ROUTER_IN = 7
N_ROUTER_OUT = 5


def _router_io(d, mod, r, layer, g, router_w, router_b, n_rows, n_lat_rows, n_seq, b, tm):
    n_exp = router_w.shape[1]
    slabs = d // LANES
    n_lat_tiles = n_lat_rows // tm
    rw_t = router_w.T
    rw_hi = rw_t.astype(BF16)
    rw_lo = (rw_t - rw_hi.astype(F32)).astype(BF16)
    earlier = (jnp.arange(tm)[:, None] < jnp.arange(tm)[None, :]).astype(BF16)
    in_specs = [
        _mod_spec(layer, 3, r, d, n_lat_tiles, tm, n_seq, b),
        _mod_spec(layer, 4, r, d, n_lat_tiles, tm, n_seq, b),
        pl.BlockSpec((1, d), lambda i: (0, 0)),
        pl.BlockSpec((n_exp, d), lambda i: (0, 0)),
        pl.BlockSpec((n_exp, d), lambda i: (0, 0)),
        pl.BlockSpec((n_exp, 1), lambda i: (0, 0)),
        pl.BlockSpec((tm, tm), lambda i: (0, 0)),
    ]
    args = [mod, mod, g[None, :], rw_hi, rw_lo, router_b[:, None], earlier]
    out_specs = [
        pl.BlockSpec((tm * slabs // 2, LANES), lambda i: (i, 0)),
        pl.BlockSpec((TOP_K, tm), lambda i: (0, i)),
        pl.BlockSpec((TOP_K, tm), lambda i: (0, i)),
        pl.BlockSpec((TOP_K, tm), lambda i: (0, i)),
        pl.BlockSpec((n_exp, 1), lambda i: (0, 0)),
    ]
    out_shape = [
        jax.ShapeDtypeStruct((n_rows * slabs // 2, LANES), U32),
        jax.ShapeDtypeStruct((TOP_K, n_rows), I32),
        jax.ShapeDtypeStruct((TOP_K, n_rows), F32),
        jax.ShapeDtypeStruct((TOP_K, n_rows), I32),
        jax.ShapeDtypeStruct((n_exp, 1), F32),
    ]
    assert len(in_specs) == N_ROUTER_IN and len(out_specs) == N_ROUTER_OUT
    return in_specs, args, out_specs, out_shape, n_exp


PLAN_TOKEN_TILE = 2048


def _plan_body(cols_ref, idx_ref, rank_ref, pos_ref, vis_ref, *, n_exp, tile, tm):
    cols = cols_ref[...]
    start, end, first_tile, visit_start, visit_end = (cols[:, c:c + 1] for c in range(5))
    idx = idx_ref[...]
    experts = lax.broadcasted_iota(I32, (n_exp, tm), 0)
    pos_ref[...] = rank_ref[...] + jnp.concatenate(
        [jnp.sum(jnp.where(experts == idx[k:k + 1], start, 0), axis=0, keepdims=True) for k in range(TOP_K)], axis=0)

    @pl.when(pl.program_id(0) == 0)
    def _():
        width = vis_ref.shape[1]
        v = lax.broadcasted_iota(I32, (1, width), 1)
        n_visits = jnp.max(visit_end, axis=0, keepdims=True)
        e_of = jnp.minimum(jnp.sum((v >= visit_end).astype(I32), axis=0, keepdims=True), n_exp - 1)
        mine = lax.broadcasted_iota(I32, (n_exp, width), 0) == e_of

        def pick(col):
            return jnp.sum(jnp.where(mine, col, 0), axis=0, keepdims=True)

        t_of = pick(first_tile) + v - pick(visit_start)
        lo = jnp.maximum(pick(start), t_of * tile) - t_of * tile
        hi = jnp.minimum(pick(end), (t_of + 1) * tile) - t_of * tile
        valid = v < n_visits
        at_last = v == jnp.maximum(n_visits - 1, 0)
        t_last = jnp.sum(jnp.where(at_last, t_of, 0), axis=1, keepdims=True)
        e_last = jnp.sum(jnp.where(at_last, e_of, 0), axis=1, keepdims=True)
        vis_ref[0:1, :] = jnp.where(valid, t_of, t_last)
        vis_ref[1:2, :] = jnp.where(valid, e_of, e_last)
        vis_ref[2:3, :] = jnp.where(valid, lo, 0)
        vis_ref[3:4, :] = jnp.where(valid, hi, 0)
        vis_ref[4:SUBLANES, :] = jnp.broadcast_to(n_visits, (SUBLANES - 4, width))


def _dispatch_plan(top_idx, rank, counts, tile, in_slabs, out_slabs):
    n_tok = top_idx.shape[1]
    n_exp = counts.shape[0]
    n_assign = TOP_K * n_tok
    n_tiles = n_assign // tile
    max_visits = n_tiles + n_exp - 1
    counts = counts[:, 0].astype(I32)
    end = jnp.cumsum(counts)
    start = end - counts
    first_tile = start // tile
    visits = jnp.where(counts > 0, (end - 1) // tile - first_tile + 1, 0)
    visit_end = jnp.cumsum(visits)
    cols = jnp.stack([start, end, first_tile, visit_end - visits, visit_end] + [end] * (SUBLANES - 5), axis=1)
    tm = math.gcd(n_tok, PLAN_TOKEN_TILE)
    width = _round_up(max_visits, LANES)
    pos, vis = pl.pallas_call(
        functools.partial(_plan_body, n_exp=n_exp, tile=tile, tm=tm),
        grid=(n_tok // tm,),
        in_specs=[
            pl.BlockSpec((n_exp, SUBLANES), lambda i: (0, 0)),
            pl.BlockSpec((TOP_K, tm), lambda i: (0, i)),
            pl.BlockSpec((TOP_K, tm), lambda i: (0, i)),
        ],
        out_specs=[pl.BlockSpec((TOP_K, tm), lambda i: (0, i)), pl.BlockSpec((SUBLANES, width), lambda i: (0, 0))],
        out_shape=[jax.ShapeDtypeStruct((TOP_K, n_tok), I32), jax.ShapeDtypeStruct((SUBLANES, width), I32)],
        compiler_params=_params("arbitrary"),
        name="moe_dispatch_plan",
    )(cols, top_idx, rank)
    slot_of_row = jnp.zeros((n_assign,), I32).at[pos.reshape(-1)].add(jnp.arange(n_assign, dtype=I32))
    row_token = ((slot_of_row % n_tok) * in_slabs).reshape(n_tiles, 1, tile)
    row_slot = (slot_of_row * out_slabs).reshape(n_tiles, 1, tile)
    return (row_token, row_slot, vis[0, :max_visits], vis[1, :max_visits], vis[2, :max_visits], vis[3, :max_visits],
            vis[4, :1])


def _expert_body(tile_ref, exp_ref, lo_ref, hi_ref, nv_ref,
                 idx_ref, idx0_ref, h_hbm, wg_ref, wl_ref, wd_ref, bg_ref, bl1_ref, bd_ref,
                 y_hbm, gbuf, xs, yacc, ybuf, wd_bf, gsem, ssem, *, tile, slabs):
    v = pl.program_id(0)
    n_visits = nv_ref[0]
    t = tile_ref[v]
    new_expert = jnp.logical_or(v == 0, exp_ref[jnp.maximum(v - 1, 0)] != exp_ref[v])
    first_visit = jnp.logical_or(v == 0, tile_ref[jnp.maximum(v - 1, 0)] != t)
    last_visit = jnp.logical_or(v == n_visits - 1, tile_ref[jnp.minimum(v + 1, pl.num_programs(0) - 1)] != t)
    valid = v < n_visits
    straight = jnp.logical_and(jnp.logical_and(first_visit, last_visit), t >= 1)
    g_cur = lax.rem(t, 2)
    y_cur = lax.rem(t, 3)
    y_prev = lax.rem(t + 2, 3)
    half = slabs // 2
    rows = tile * half

    own_tok, next_tok, prev_slot, own_slot = range(4)

    def start_gather(ref, which, g):
        for r in range(tile):
            src = pl.multiple_of(ref[0, 0, which * tile + r], half)
            pltpu.make_async_copy(h_hbm.at[pl.ds(src, half), :], gbuf.at[g, pl.ds(r * half, half), :],
                                  gsem.at[g]).start()

    def wait_gather(g):
        pltpu.make_async_copy(h_hbm.at[pl.ds(0, rows), :], gbuf.at[g], gsem.at[g]).wait()

    def start_scatter(which, yb):
        for r in range(tile):
            dst = pl.multiple_of(idx_ref[0, 0, which * tile + r], half)
            pltpu.make_async_copy(ybuf.at[yb, pl.ds(r * half, half), :], y_hbm.at[pl.ds(dst, half), :],
                                  ssem.at[yb]).start()

    def wait_scatter(yb):
        pltpu.make_async_copy(ybuf.at[yb], y_hbm.at[pl.ds(0, rows), :], ssem.at[yb]).wait()

    def begin_tile():
        start_gather(idx_ref, next_tok, 1 - g_cur)
        for s in range(half):
            lo, hi = _unpack_bf16_pair(gbuf[g_cur, pl.ds(s, tile, stride=half), :])
            xs[:, s * LANES:(s + 1) * LANES] = lo.astype(BF16)
            xs[:, (s + half) * LANES:(s + half + 1) * LANES] = hi.astype(BF16)

    def ffn():
        x = xs[...]
        gl = lax.dot_general(x, wg_ref[...], _NT, preferred_element_type=F32) + bg_ref[...]
        ln1 = lax.dot_general(x, wl_ref[...], _NT, preferred_element_type=F32) + bl1_ref[...]
        x_glu = jnp.minimum(gl, SWIGLU_LIMIT)
        x_lin1 = jnp.clip(ln1, 1.0 - SWIGLU_LIMIT, 1.0 + SWIGLU_LIMIT)
        act = x_glu * (1.0 + jnp.tanh((0.5 * SWIGLU_ALPHA) * x_glu)) * x_lin1
        return jnp.dot(act.astype(BF16), wd_bf[...], preferred_element_type=F32) + bd_ref[...]

    def stage(y_of_slab):
        for s in range(half):
            ybuf[y_cur, pl.ds(s, tile, stride=half), :] = _pack_bf16_pair(y_of_slab(s), y_of_slab(s + half))

    @pl.when(v == 0)
    def _():
        start_gather(idx0_ref, own_tok, 0)

    @pl.when(jnp.logical_and(valid, new_expert))
    def _():
        wd_bf[...] = (0.5 * wd_ref[...]).astype(BF16)

    @pl.when(jnp.logical_and(valid, first_visit))
    def _():
        wait_gather(g_cur)

        @pl.when(t >= 3)
        def _():
            wait_scatter(y_cur)

    @pl.when(jnp.logical_and(valid, straight))
    def _():
        start_scatter(prev_slot, y_prev)
        begin_tile()
        y = ffn()
        stage(lambda s: y[:, s * LANES:(s + 1) * LANES])

    @pl.when(jnp.logical_and(valid, jnp.logical_not(straight)))
    def _():
        @pl.when(first_visit)
        def _():
            @pl.when(t >= 1)
            def _():
                start_scatter(prev_slot, y_prev)

            begin_tile()

        y = ffn()

        @pl.when(first_visit)
        def _():
            yacc[...] = y

        @pl.when(jnp.logical_not(first_visit))
        def _():
            row = lax.broadcasted_iota(I32, (tile, 1), 0)
            mine = jnp.logical_and(row >= lo_ref[v], row < hi_ref[v])
            yacc[...] = jnp.where(mine, y, yacc[...])

        @pl.when(last_visit)
        def _():
            stage(lambda s: yacc[:, s * LANES:(s + 1) * LANES])

    @pl.when(v == n_visits - 1)
    def _():
        start_scatter(own_slot, y_cur)
        wait_gather(1 - g_cur)
        wait_scatter(y_cur)

        @pl.when(t >= 1)
        def _():
            wait_scatter(y_prev)

        @pl.when(t >= 2)
        def _():
            wait_scatter(lax.rem(t + 1, 3))


def _experts(h_slab, plan, layer, wg, wl, wd, bg, bl, bd, n_tok):
    row_token, row_slot, t_of, e_of, lo, hi, n_visits = plan
    n_tiles, _, tile = row_token.shape
    _, n_exp, f, d = wg.shape
    slabs = d // LANES
    max_visits = t_of.shape[0]

    def expert_block(v, tl, ex, lo_, hi_, nv):
        return (layer, ex[v], 0, 0)

    body = functools.partial(_expert_body, tile=tile, slabs=slabs)
    nxt = jnp.minimum(jnp.arange(n_tiles) + 1, n_tiles - 1)
    prv = jnp.maximum(jnp.arange(n_tiles) - 1, 0)
    tile_idx = jnp.concatenate([row_token, row_token[nxt], row_slot[prv], row_slot], axis=2)
    smem_tile = functools.partial(pl.BlockSpec, (1, 1, 4 * tile), memory_space=pltpu.SMEM)
    grid_spec = pltpu.PrefetchScalarGridSpec(
        num_scalar_prefetch=5,
        grid=(max_visits,),
        in_specs=[
            smem_tile(lambda v, tl, ex, lo_, hi_, nv: (tl[v], 0, 0)),
            smem_tile(lambda v, tl, ex, lo_, hi_, nv: (0, 0, 0)),
            pl.BlockSpec(memory_space=pl.ANY),
            pl.BlockSpec((None, None, f, d), expert_block),
            pl.BlockSpec((None, None, f, d), expert_block),
            pl.BlockSpec((None, None, f, d), expert_block),
            pl.BlockSpec((None, None, 1, f), expert_block),
            pl.BlockSpec((None, None, 1, f), expert_block),
            pl.BlockSpec((None, None, 1, d), expert_block),
        ],
        out_specs=pl.BlockSpec(memory_space=pl.ANY),
        scratch_shapes=[
            pltpu.VMEM((2, tile * slabs // 2, LANES), U32),
            pltpu.VMEM((tile, d), BF16),
            pltpu.VMEM((tile, d), F32),
            pltpu.VMEM((3, tile * slabs // 2, LANES), U32),
            pltpu.VMEM((f, d), BF16),
            pltpu.SemaphoreType.DMA((2,)),
            pltpu.SemaphoreType.DMA((3,)),
        ],
    )
    return pl.pallas_call(
        body,
        grid_spec=grid_spec,
        out_shape=jax.ShapeDtypeStruct((TOP_K * n_tok * slabs // 2, LANES), U32),
        compiler_params=_params("arbitrary"),
        name="moe_experts",
    )(t_of, e_of, lo, hi, n_visits, tile_idx, tile_idx, h_slab, wg, wl, wd, bg, bl, bd)


def _combine_rows(x_ref, g2_ref, gate_ref, y_refs, out_ref, tm, slabs, row0=0):
    rows = slice(row0, row0 + tm)
    gates = gate_ref[rows, :]
    g2 = g2_ref[0]
    half = slabs // 2
    for s in range(half):
        acc_lo = acc_hi = None
        for k in range(TOP_K):
            lo, hi = _unpack_bf16_pair(y_refs[k][pl.ds(row0 * half + s, tm, stride=half), :])
            acc_lo = gates[:, k:k + 1] * lo if acc_lo is None else acc_lo + gates[:, k:k + 1] * lo
            acc_hi = gates[:, k:k + 1] * hi if acc_hi is None else acc_hi + gates[:, k:k + 1] * hi
        for acc, slab in ((acc_lo, s), (acc_hi, s + half)):
            cols = slice(slab * LANES, (slab + 1) * LANES)
            out_ref[rows, cols] = x_ref[rows, cols] + g2[:, cols] * acc


def _combine_final_body(x_ref, g2_ref, gate_ref, *refs, tm, slabs):
    y_refs, fg_ref, out_ref = refs[:TOP_K], refs[TOP_K], refs[TOP_K + 1]
    _combine_rows(x_ref, g2_ref, gate_ref, y_refs, out_ref, tm, slabs)
    x = out_ref[...]
    ms = jnp.mean(x * x, axis=-1, keepdims=True)
    out_ref[...] = x * lax.rsqrt(ms + EPS) * fg_ref[...]


def _combine_qkv_body(x_ref, g2_ref, gate_ref, *refs, tm, slabs, d):
    y_refs = refs[:TOP_K]
    sh_ref, sc_ref, g_ref, w_ref, cos_ref, sin_ref, xo_ref, o_ref = refs[TOP_K:]
    halves = [slice(h * tm // 2, (h + 1) * tm // 2) for h in range(2)]
    for rows in halves:
        _combine_rows(x_ref, g2_ref, gate_ref, y_refs, xo_ref, tm // 2, slabs, rows.start)
    for rows in halves:
        _qkv_rows(xo_ref[rows, :], sh_ref, sc_ref, g_ref, w_ref, cos_ref, sin_ref, o_ref, d, rows)


def _combine_cproj_body(x_ref, g2_ref, gate_ref, *refs, tm, slabs, d):
    y_refs = refs[:TOP_K]
    sh_ref, sc_ref, g_ref, w_ref, xo_ref, gb_ref, u_ref = refs[TOP_K:]
    halves = [slice(h * tm // 2, (h + 1) * tm // 2) for h in range(2)]
    for rows in halves:
        _combine_rows(x_ref, g2_ref, gate_ref, y_refs, xo_ref, tm // 2, slabs, rows.start)
    for rows in halves:
        _cproj_rows(xo_ref[rows, :], sh_ref, sc_ref, g_ref, w_ref, gb_ref, u_ref, d, rows)


def _combine(xa, y_slab, gates_t, mod, r, layer, n_rows, n_moe_rows, n_lat_rows, n_seq, b, final_g=None, nxt=None):
    d = xa.shape[1]
    slabs = d // LANES
    tm = ROW_TILE if nxt is None else FUSED_ROW_TILE
    n_tiles = n_rows // tm
    n_lat_tiles = n_lat_rows // tm
    y_tiles_per_k = n_moe_rows // tm
    y_specs = [pl.BlockSpec((tm * slabs // 2, LANES), functools.partial(lambda i, k: (k * y_tiles_per_k + i, 0), k=k))
               for k in range(TOP_K)]
    row_spec = pl.BlockSpec((tm, d), lambda i: (i, 0))
    in_specs = [row_spec, _mod_spec(layer, 5, r, d, n_lat_tiles, tm, n_seq, b),
                pl.BlockSpec((tm, TOP_K), lambda i: (i, 0))] + y_specs
    args = [xa, mod, gates_t] + [y_slab] * TOP_K
    x_shape = jax.ShapeDtypeStruct((n_rows, d), F32)
    if nxt is None:
        return pl.pallas_call(
            functools.partial(_combine_final_body, tm=tm, slabs=slabs),
            grid=(n_tiles,),
            in_specs=in_specs + [pl.BlockSpec((1, d), lambda i: (0, 0))],
            out_specs=row_spec,
            out_shape=x_shape,
            compiler_params=_params("parallel"),
            name="moe_combine_final",
        )(*args, final_g[None, :])
    kind, g, w = nxt
    in_specs += [_mod_spec(layer + 1, 0, r, d, n_lat_tiles, tm, n_seq, b),
                 _mod_spec(layer + 1, 1, r, d, n_lat_tiles, tm, n_seq, b),
                 pl.BlockSpec((1, d), lambda i: (0, 0)),
                 pl.BlockSpec((d, 3 * d), lambda i: (0, 0))]
    args += [mod, mod, g[None, :], w.astype(BF16)]
    if kind == "attn":
        tiles_per_seq = n_seq // tm
        cos_t, sin_t = _rope_tables(n_seq, tm)

        def rope_index(i):
            return (jnp.where(i < n_lat_tiles, i % tiles_per_seq, tiles_per_seq), 0)

        return pl.pallas_call(
            functools.partial(_combine_qkv_body, tm=tm, slabs=slabs, d=d),
            grid=(n_tiles,),
            in_specs=in_specs + [pl.BlockSpec((tm, LANES), rope_index)] * 2,
            out_specs=[row_spec, pl.BlockSpec((tm, 3 * d), lambda i: (i, 0))],
            out_shape=[x_shape, jax.ShapeDtypeStruct((n_rows, 3 * d), BF16)],
            compiler_params=_params("parallel"),
            name="moe_combine_qkv_proj",
        )(*args, cos_t, sin_t)
    xa_new, gb, u = pl.pallas_call(
        functools.partial(_combine_cproj_body, tm=tm, slabs=slabs, d=d),
        grid=(n_tiles,),
        in_specs=in_specs,
        out_specs=[row_spec, row_spec, row_spec],
        out_shape=[x_shape, jax.ShapeDtypeStruct((n_rows, d), BF16), jax.ShapeDtypeStruct((n_rows, d), BF16)],
        compiler_params=_params("parallel"),
        name="moe_combine_conv_in_proj",
    )(*args)
    return xa_new, (gb, u)


def _wsplit_body(w_ref, wg_ref, wl_ref, ts):
    d, f2 = w_ref.shape
    half = LANES // 2
    for c in range(f2 // LANES):
        for j in range(d // LANES):
            blk = (c * (d // LANES) + j) * LANES
            ts[pl.ds(blk, LANES), :] = w_ref[j * LANES:(j + 1) * LANES, c * LANES:(c + 1) * LANES].T
            rows = slice(c * half, (c + 1) * half)
            cols = slice(j * LANES, (j + 1) * LANES)
            wg_ref[rows, cols] = ts[pl.ds(blk, half, stride=2), :].astype(BF16)
            wl_ref[rows, cols] = ts[pl.ds(blk + 1, half, stride=2), :].astype(BF16)


def _split_gate_up(w_gu):
    depth, n_exp, d, f2 = w_gu.shape
    f = f2 // 2
    out = jax.ShapeDtypeStruct((depth, n_exp, f, d), BF16)
    return pl.pallas_call(
        _wsplit_body,
        grid=(depth, n_exp),
        in_specs=[pl.BlockSpec((None, None, d, f2), lambda l, e: (l, e, 0, 0))],
        out_specs=[pl.BlockSpec((None, None, f, d), lambda l, e: (l, e, 0, 0))] * 2,
        out_shape=[out, out],
        scratch_shapes=[pltpu.VMEM((d * f2 // LANES, LANES), F32)],
        compiler_params=_params("parallel", "parallel"),
        name="moe_split_gate_up",
    )(w_gu)


def _moe_experts(routed, layer, expert_w, n_rows, d):
    h_slab, top_idx, gates, rank, counts = routed
    slabs = d // LANES
    plan = _dispatch_plan(top_idx, rank, counts, MOE_TILE, slabs // 2, slabs // 2)
    return _experts(h_slab, plan, layer, *expert_w, n_rows), gates.T


def kernel(x, c, ctx, c_ctx, mod_w, mod_b, norm_g, attn_w_qkv, attn_w_o, attn_lambda, attn_subln_g, conv_w_in, conv_w, conv_w_out, router_w, router_b, moe_w_gu, moe_b_gu, moe_w_down, moe_b_down, final_g):
    b, n_seq, d = x.shape
    n_ctx = ctx.shape[1]
    depth = mod_w.shape[0]
    n_lat_rows = b * n_seq
    n_all_rows = n_lat_rows + b * n_ctx
    assert d % LANES == 0 and d == moe_w_down.shape[-1]
    assert n_seq % ROW_TILE == 0 and (b * n_ctx) % ROW_TILE == 0 and n_seq % GRID_W == 0
    assert ROW_TILE % n_ctx == 0 or n_ctx % ROW_TILE == 0
    assert n_lat_rows % n_ctx == 0 and n_ctx % BF16_SUBLANES == 0
    assert (TOP_K * n_lat_rows) % MOE_TILE == 0 and (TOP_K * n_all_rows) % MOE_TILE == 0

    assert N_MIXERS == 2
    xa = None
    mod, r = _modulation(c, c_ctx, mod_w, mod_b)
    wg, wl = _split_gate_up(moe_w_gu)
    expert_w = (wg, wl, moe_w_down, moe_b_gu[:, :, None, 0::2], moe_b_gu[:, :, None, 1::2] + 1.0,
                moe_b_down[:, :, None, :])
    proj = None
    for i in range(depth):
        update_ctx = i < depth - 1
        attn_layer = i % N_MIXERS == 0
        j = i // N_MIXERS
        n_out = n_all_rows if update_ctx else n_lat_rows
        if attn_layer:
            lam_init = 0.8 - 0.6 * math.exp(-0.3 * i)
            x_rows = (x.reshape(n_lat_rows, d), ctx.reshape(b * n_ctx, d)) if i == 0 else (xa, xa)
            qkv = proj if proj is not None else _qkv_proj(*x_rows, mod, r, i, norm_g[i, 0], attn_w_qkv[j], n_all_rows,
                                                           n_lat_rows, n_seq, b)
            o_lat, o_ctx = _attention(qkv, attn_lambda[j], attn_subln_g[j], lam_init, b, n_seq, n_ctx, n_lat_rows,
                                      update_ctx)
            xa, routed = _out_proj(o_lat, o_ctx, attn_w_o[j], *x_rows, mod, r, i, norm_g[i, 1], router_w[i],
                                   router_b[i], n_lat_rows, n_seq, b)
        else:
            gb, u = proj if proj is not None else _conv_proj(xa, mod, r, i, norm_g[i, 0], conv_w_in[j], n_out,
                                                             n_lat_rows, n_seq, b)
            xa, routed = _conv_out(u, gb, conv_w[j], conv_w_out[j], xa, mod, r, i, norm_g[i, 1], router_w[i],
                                   router_b[i], n_lat_rows, n_seq, n_ctx, b)
        y_slab, gates_t = _moe_experts(routed, i, expert_w, n_out, d)
        if i == depth - 1:
            xa = _combine(xa, y_slab, gates_t, mod, r, i, n_out, n_out, n_lat_rows, n_seq, b, final_g=final_g)
        else:
            nxt_attn = (i + 1) % N_MIXERS == 0
            jn = (i + 1) // N_MIXERS
            n_next = n_all_rows if (nxt_attn or i + 1 < depth - 1) else n_lat_rows
            nxt = ("attn", norm_g[i + 1, 0], attn_w_qkv[jn]) if nxt_attn else ("conv", norm_g[i + 1, 0], conv_w_in[jn])
            xa, proj = _combine(xa, y_slab, gates_t, mod, r, i, n_next, n_out, n_lat_rows, n_seq, b, nxt=nxt)
    return xa.reshape(b, n_seq, d)
```
